```python
import math
import jax, jax.numpy as jnp
from jax import lax
import numpy as np

D_MODEL = 2048
BATCH = 4
SEQ = 8192
DEPTH = 1
DEC_BATCH = 2
DEC_SEQ = 4096
PAST_LEN = 128

N_MEM = 256
DIL_PAIRS = ((128, 1), (512, 4), (2048, 16))
N_DIL = 3
A_HEADS = 4
A_HD = 128
A_WIDTH = A_HEADS * A_HD
B_HEADS = 8
Q_LORA = 512
KV_LORA = 512
D_NOPE = 128
D_ROPE = 64
D_V = 128
B_WIDTH = B_HEADS * D_V
ROPE_BASE = 10000.0
Q_BLOCK = 128
C_HEADS = 4
C_HD = 128
C_WIDTH = C_HEADS * C_HD
D_MIX = A_WIDTH + B_WIDTH + C_WIDTH
N_BUCKETS = 32
MAX_DISTANCE = 1024
EPS = 1e-6
NEG = -1e30
IN_SIZES = (N_DIL * A_WIDTH, A_WIDTH, A_WIDTH, A_WIDTH,
            Q_LORA, KV_LORA, D_ROPE, B_WIDTH,
            C_WIDTH, C_WIDTH)
D_IN = sum(IN_SIZES)

kernel_name = 'hybrid_dilated_mla_memory_encoder'


def rmsnorm(x, g):
    xf = x.astype(jnp.float32)
    y = xf * lax.rsqrt(jnp.mean(xf * xf, axis=-1, keepdims=True) + EPS)
    return (y * g.astype(jnp.float32)).astype(x.dtype)


def t5_bucket(rel):
    nb = N_BUCKETS // 2
    max_exact = nb // 2
    bucket = jnp.where(rel > 0, nb, 0)
    n = jnp.abs(rel)
    nf = jnp.maximum(n, 1).astype(jnp.float32)
    large = max_exact + (jnp.log(nf / max_exact) / math.log(MAX_DISTANCE / max_exact)
                         * (nb - max_exact)).astype(jnp.int32)
    large = jnp.minimum(large, nb - 1)
    return bucket + jnp.where(n < max_exact, n, large)


def dilated_attention(q, k, v, bias_table, window, dil):
    Bn, S, H, E = q.shape
    blk = window // (2 * dil)
    L = S // dil
    nb = -(-L // blk)
    Lp = nb * blk

    def to_res(t):
        return t.reshape(Bn, L, dil, H, E).transpose(0, 2, 1, 3, 4)

    qr = jnp.pad(to_res(q), ((0, 0), (0, 0), (0, Lp - L), (0, 0), (0, 0))).reshape(Bn, dil, nb, blk, H, E)

    def windows(t):
        tp = jnp.pad(to_res(t), ((0, 0), (0, 0), (blk, Lp - L + blk), (0, 0), (0, 0)))
        tp = tp.reshape(Bn, dil, nb + 2, blk, H, E)
        return jnp.concatenate([tp[:, :, 0:nb], tp[:, :, 1:nb + 1], tp[:, :, 2:nb + 2]], axis=3)

    kw, vw = windows(k), windows(v)
    qi = jnp.arange(blk)[:, None]
    kj = jnp.arange(3 * blk)[None, :]
    rel = kj - blk - qi
    bias = bias_table.astype(jnp.float32)[t5_bucket(rel * dil)].transpose(2, 0, 1)
    key_l = jnp.arange(nb)[:, None] * blk - blk + kj
    valid = (jnp.abs(rel)[None] <= blk) & ((key_l >= 0) & (key_l < L))[:, None, :]
    s = jnp.einsum('bdnqhe,bdnkhe->bdnhqk', qr, kw, preferred_element_type=jnp.float32) * (E ** -0.5)
    s = jnp.where(valid[None, None, :, None], s + bias[None, None, None], NEG)
    m = jnp.max(s, axis=-1, keepdims=True)
    p = jnp.exp(s - m)
    den = jnp.sum(p, axis=-1, keepdims=True)
    o = jnp.einsum('bdnhqk,bdnkhe->bdnqhe', (p / den).astype(v.dtype), vw)
    lse = (m + jnp.log(den))[..., 0].swapaxes(3, 4)

    def from_res(t):
        t = t.reshape(Bn, dil, Lp, *t.shape[4:])[:, :, :L]
        return t.swapaxes(1, 2).reshape(Bn, S, *t.shape[3:])

    return from_res(o), from_res(lse)


def apply_rope(t):
    S = t.shape[1]
    inv = ROPE_BASE ** (-jnp.arange(0, D_ROPE, 2, dtype=jnp.float32) / D_ROPE)
    ang = jnp.arange(S, dtype=jnp.float32)[:, None] * inv[None, :]
    shape = (1, S) + (1,) * (t.ndim - 3) + (D_ROPE // 2,)
    cos, sin = jnp.cos(ang).reshape(shape), jnp.sin(ang).reshape(shape)
    tf = t.astype(jnp.float32)
    t1, t2 = tf[..., :D_ROPE // 2], tf[..., D_ROPE // 2:]
    return jnp.concatenate([t1 * cos - t2 * sin, t1 * sin + t2 * cos], axis=-1).astype(t.dtype)


def mla_attention(q_nope, q_rope, k_nope, k_rope, v):
    Bn, S, H, _ = q_nope.shape
    nq = S // Q_BLOCK
    scale = (D_NOPE + D_ROPE) ** -0.5

    def blocks(t):
        return t.reshape(Bn, nq, Q_BLOCK, *t.shape[2:]).swapaxes(0, 1)

    def one_block(args):
        qn, qr = args
        s = (jnp.einsum('bqhe,bkhe->bhqk', qn, k_nope, preferred_element_type=jnp.float32)
             + jnp.einsum('bqhe,bke->bhqk', qr, k_rope, preferred_element_type=jnp.float32)) * scale
        p = jax.nn.softmax(s, axis=-1)
        return jnp.einsum('bhqk,bkhe->bqhe', p.astype(v.dtype), v)

    o = lax.map(one_block, (blocks(q_nope), blocks(q_rope)))
    return o.swapaxes(0, 1).reshape(Bn, S, H, D_V)


def memory_attention(q, mem, mem_gain, w_mem_kv):
    Bn, S = q.shape[:2]
    M = mem.shape[1]
    mkv = (rmsnorm(mem, mem_gain) @ w_mem_kv).reshape(Bn, M, 2, C_HEADS, C_HD)
    mk, mv = mkv[:, :, 0], mkv[:, :, 1]
    s = jnp.einsum('bshe,bmhe->bhsm', q, mk, preferred_element_type=jnp.float32) * (C_HD ** -0.5)
    p = jax.nn.softmax(s, axis=-1)
    return jnp.einsum('bhsm,bmhe->bshe', p.astype(mv.dtype), mv).reshape(Bn, S, C_WIDTH)


def hybrid_layer(x, mem, pre_gain, w_in, q_gain, w_uq, kv_gain, w_ukv, mem_gain, w_mem_kv,
                 w_out, post_gain, rel_bias):
    Bn, S, _ = x.shape
    h = rmsnorm(x, pre_gain)
    offsets = [int(v) for v in np.cumsum(IN_SIZES)[:-1]]
    a_q, a_k, a_v, a_g, b_cq, b_ckv, b_kr, b_g, c_q, c_g = jnp.split(h @ w_in, offsets, axis=-1)

    qa = a_q.reshape(Bn, S, N_DIL, A_HEADS, A_HD)
    ka = a_k.reshape(Bn, S, A_HEADS, A_HD)
    va = a_v.reshape(Bn, S, A_HEADS, A_HD)
    outs, lses = [], []
    for g, (win, dil) in enumerate(DIL_PAIRS):
        o, l = dilated_attention(qa[:, :, g], ka, va, rel_bias[:, g * A_HEADS:(g + 1) * A_HEADS], win, dil)
        outs.append(o)
        lses.append(l)
    alpha = jax.nn.softmax(jnp.stack(lses, axis=0), axis=0)
    oa = jnp.einsum('gbsh,gbshe->bshe', alpha, jnp.stack(outs, axis=0).astype(jnp.float32))
    ya = oa.reshape(Bn, S, A_WIDTH).astype(x.dtype) * jax.nn.silu(a_g)

    qb = (rmsnorm(b_cq, q_gain) @ w_uq).reshape(Bn, S, B_HEADS, D_NOPE + D_ROPE)
    q_nope, q_rope = qb[..., :D_NOPE], apply_rope(qb[..., D_NOPE:])
    kvb = (rmsnorm(b_ckv, kv_gain) @ w_ukv).reshape(Bn, S, B_HEADS, D_NOPE + D_V)
    k_nope, vb = kvb[..., :D_NOPE], kvb[..., D_NOPE:]
    k_rope = apply_rope(b_kr)
    yb = mla_attention(q_nope, q_rope, k_nope, k_rope, vb).reshape(Bn, S, B_WIDTH) * jax.nn.silu(b_g)

    yc = memory_attention(c_q.reshape(Bn, S, C_HEADS, C_HD), mem, mem_gain, w_mem_kv) * jax.nn.silu(c_g)

    y = jnp.concatenate([ya, yb, yc], axis=-1) @ w_out
    return x + rmsnorm(y, post_gain)


def setup_inputs(seed: int = 0) -> dict:
    key = jax.random.key(seed)
    ks = jax.random.split(key, 16)

    def w(k, shape, fan_in):
        return jax.random.normal(k, shape, jnp.float32) * (fan_in ** -0.5)

    def gain(k, dim):
        return 1.0 + 0.01 * jax.random.normal(k, (DEPTH, dim), jnp.float32)

    return {
        'x_prompt': jax.random.normal(ks[0], (BATCH, SEQ, D_MODEL), jnp.float32),
        'x_sample': jax.random.normal(ks[1], (DEC_BATCH, DEC_SEQ, D_MODEL), jnp.float32),
        'mem_prompt': jax.random.normal(ks[2], (BATCH, N_MEM, D_MODEL), jnp.float32),
        'mem_sample': jax.random.normal(ks[3], (DEC_BATCH, N_MEM, D_MODEL), jnp.float32),
        'pre_gain': gain(ks[4], D_MODEL),
        'w_in': w(ks[5], (DEPTH, D_MODEL, D_IN), D_MODEL),
        'q_gain': gain(ks[6], Q_LORA),
        'w_uq': w(ks[7], (DEPTH, Q_LORA, B_HEADS * (D_NOPE + D_ROPE)), Q_LORA),
        'kv_gain': gain(ks[8], KV_LORA),
        'w_ukv': w(ks[9], (DEPTH, KV_LORA, B_HEADS * (D_NOPE + D_V)), KV_LORA),
        'mem_gain': gain(ks[10], D_MODEL),
        'w_mem_kv': w(ks[11], (DEPTH, D_MODEL, 2 * C_WIDTH), D_MODEL),
        'w_out': w(ks[12], (DEPTH, D_MIX, D_MODEL), D_MIX),
        'post_gain': gain(ks[13], D_MODEL),
        'rel_bias': 0.5 * jax.random.normal(ks[14], (N_BUCKETS, N_DIL * A_HEADS), jnp.float32),
    }


def reference(x_prompt, x_sample, mem_prompt, mem_sample, pre_gain, w_in, q_gain, w_uq, kv_gain, w_ukv,
              mem_gain, w_mem_kv, w_out, post_gain, rel_bias):
    def trunk(x, mem):
        for l in range(DEPTH):
            x = hybrid_layer(x, mem, pre_gain[l], w_in[l], q_gain[l], w_uq[l], kv_gain[l], w_ukv[l],
                             mem_gain[l], w_mem_kv[l], w_out[l], post_gain[l], rel_bias)
        return x

    y_prompt = trunk(x_prompt, mem_prompt)
    y_sample = trunk(x_sample, mem_sample)
    return (y_prompt, y_sample)
```

```python
import functools
import math

import jax
import jax.numpy as jnp
import numpy as np
from jax import lax
from jax.experimental import pallas as pl
from jax.experimental.pallas import tpu as pltpu

D_MODEL = 2048
DIL_PAIRS = ((128, 1), (512, 4), (2048, 16))
N_DIL = 3
DIL_HALF = 64
A_HEADS = 4
A_HD = 128
A_WIDTH = A_HEADS * A_HD
B_HEADS = 8
Q_LORA = 512
KV_LORA = 512
D_NOPE = 128
D_ROPE = 64
D_V = 128
B_WIDTH = B_HEADS * D_V
ROPE_BASE = 10000.0
C_HEADS = 4
C_HD = 128
C_WIDTH = C_HEADS * C_HD
D_MIX = A_WIDTH + B_WIDTH + C_WIDTH
N_BUCKETS = 32
MAX_DISTANCE = 1024
EPS = 1e-6
NEG = -1e30

LANES = 128
VMEM_LIMIT_BYTES = 56 * 1024 * 1024

G_AQ = 0
G_AK = 12
G_AV = 16
G_AG = 20
G_BCQ = 24
G_BCKV = 28
G_CQ = 32
G_CG = 36
G_BG = 40
G_KR = 48
N_GROUPS = 49
D_INP = N_GROUPS * LANES
QK_DIM = 2 * LANES

F32 = jnp.float32
BF16 = jnp.bfloat16
_NT = (((1,), (1,)), ((), ()))


def _tiles(S):
    return dict(
        tm_in=min(1024, S),
        tn_in=7 * LANES,
        tm_lat=min(512, S),
        tq_mla=min(512, S),
        tk_mla=min(512, S),
        tq_dil=128,
        tm_out=min(512, S),
    )


def _params(sem):
    return pltpu.CompilerParams(dimension_semantics=sem, vmem_limit_bytes=VMEM_LIMIT_BYTES)


def _silu(g):
    return g * (1.0 / (1.0 + jnp.exp(-g)))


def _inproj_kernel(x_ref, g_ref, w_ref, o_ref, h_ref, *, rows):
    @pl.when(pl.program_id(1) == 0)
    def _():
        tm = x_ref.shape[0]
        for r0 in range(0, tm, rows):
            x = x_ref[r0:r0 + rows, :]
            ms = jnp.mean(x * x, axis=-1, keepdims=True)
            h_ref[r0:r0 + rows, :] = (x * lax.rsqrt(ms + EPS) * g_ref[...]).astype(BF16)

    o_ref[...] = jnp.dot(h_ref[...], w_ref[...], preferred_element_type=F32).astype(o_ref.dtype)


def _inproj(x2d, gain, w_bf16, tm, tn):
    T = x2d.shape[0]
    return pl.pallas_call(
        functools.partial(_inproj_kernel, rows=min(256, tm)),
        grid=(T // tm, D_INP // tn),
        in_specs=[
            pl.BlockSpec((tm, D_MODEL), lambda i, j: (i, 0)),
            pl.BlockSpec((1, D_MODEL), lambda i, j: (0, 0)),
            pl.BlockSpec((D_MODEL, tn), lambda i, j: (0, j)),
        ],
        out_specs=pl.BlockSpec((tm, tn), lambda i, j: (i, j)),
        out_shape=jax.ShapeDtypeStruct((T, D_INP), BF16),
        scratch_shapes=[pltpu.VMEM((tm, D_MODEL), BF16)],
        compiler_params=_params(("parallel", "arbitrary")),
        name="inproj",
    )(x2d, gain, w_bf16)


def _latent_kernel(cq_ref, ckv_ref, kr_ref, qg_ref, kvg_ref, wq_ref, wkv_ref, cos_ref, sin_ref,
                   q_ref, k_ref, v_ref, *, q_scale):
    def rms(c_ref, gain_ref):
        c = c_ref[...].astype(F32)
        ms = jnp.mean(c * c, axis=-1, keepdims=True)
        return (c * lax.rsqrt(ms + EPS) * gain_ref[...]).astype(BF16)

    cos = cos_ref[...]
    sin = sin_ref[...]

    def rotary(g):
        return g * cos + pltpu.roll(g, D_ROPE, 1) * sin

    qf = jnp.dot(rms(cq_ref, qg_ref), wq_ref[...], preferred_element_type=F32)
    kvf = jnp.dot(rms(ckv_ref, kvg_ref), wkv_ref[...], preferred_element_type=F32)
    k_rot = rotary(kr_ref[...].astype(F32)).astype(BF16)
    for h in range(B_HEADS):
        c0 = h * QK_DIM
        q_ref[h, :, 0:LANES] = (qf[:, c0:c0 + LANES] * q_scale).astype(BF16)
        q_ref[h, :, LANES:QK_DIM] = (rotary(qf[:, c0 + LANES:c0 + QK_DIM]) * q_scale).astype(BF16)
        k_ref[h, :, 0:LANES] = kvf[:, c0:c0 + LANES].astype(BF16)
        k_ref[h, :, LANES:QK_DIM] = k_rot
        v_ref[h, :, :] = kvf[:, c0 + LANES:c0 + QK_DIM].astype(BF16)


def _latent(proj, q_gain, kv_gain, wq_ext, wkv, cos_tab, sin_tab, B, S, tm):
    q_scale = (D_NOPE + D_ROPE) ** -0.5 * math.log2(math.e)
    ns = S // tm
    col = lambda g: (lambda b, s: (b, s, g))
    row = lambda b, s: (0, 0)
    return pl.pallas_call(
        functools.partial(_latent_kernel, q_scale=q_scale),
        grid=(B, ns),
        in_specs=[
            pl.BlockSpec((None, tm, Q_LORA), col(G_BCQ * LANES // Q_LORA)),
            pl.BlockSpec((None, tm, KV_LORA), col(G_BCKV * LANES // KV_LORA)),
            pl.BlockSpec((None, tm, LANES), col(G_KR)),
            pl.BlockSpec((1, Q_LORA), row),
            pl.BlockSpec((1, KV_LORA), row),
            pl.BlockSpec((Q_LORA, B_HEADS * QK_DIM), row),
            pl.BlockSpec((KV_LORA, B_HEADS * QK_DIM), row),
            pl.BlockSpec((tm, LANES), lambda b, s: (s, 0)),
            pl.BlockSpec((tm, LANES), lambda b, s: (s, 0)),
        ],
        out_specs=[
            pl.BlockSpec((None, B_HEADS, tm, QK_DIM), lambda b, s: (b, 0, s, 0)),
            pl.BlockSpec((None, B_HEADS, tm, QK_DIM), lambda b, s: (b, 0, s, 0)),
            pl.BlockSpec((None, B_HEADS, tm, D_V), lambda b, s: (b, 0, s, 0)),
        ],
        out_shape=[
            jax.ShapeDtypeStruct((B, B_HEADS, S, QK_DIM), BF16),
            jax.ShapeDtypeStruct((B, B_HEADS, S, QK_DIM), BF16),
            jax.ShapeDtypeStruct((B, B_HEADS, S, D_V), BF16),
        ],
        compiler_params=_params(("parallel", "parallel")),
        name="latent_proj",
    )(proj, proj, proj, q_gain, kv_gain, wq_ext, wkv, cos_tab, sin_tab)


def _mla_kernel(q_ref, k_ref, v_ref, g_ref, o_ref, m_ref, l_ref, acc_ref, *, nk, tk):
    q = q_ref[...]
    m_ref[...] = jnp.full(m_ref.shape, NEG, F32)
    l_ref[...] = jnp.zeros(l_ref.shape, F32)
    acc_ref[...] = jnp.zeros(acc_ref.shape, F32)

    def body(j, carry):
        k0 = pl.multiple_of(j * tk, tk)
        s = lax.dot_general(q, k_ref[pl.ds(k0, tk), :], _NT, preferred_element_type=F32)
        m_old = m_ref[...]
        m_new = jnp.maximum(m_old, jnp.max(s, axis=1, keepdims=True))
        alpha = jnp.exp2(m_old - m_new)
        p = jnp.exp2(s - m_new)
        l_ref[...] = alpha * l_ref[...] + jnp.sum(p, axis=1, keepdims=True)
        acc_ref[...] = alpha * acc_ref[...] + jnp.dot(p.astype(BF16), v_ref[pl.ds(k0, tk), :],
                                                      preferred_element_type=F32)
        m_ref[...] = m_new
        return carry

    lax.fori_loop(0, nk, body, 0)
    o = acc_ref[...] / l_ref[...]
    o_ref[...] = (o * _silu(g_ref[...].astype(F32))).astype(o_ref.dtype)


def _mla(q, k, v, proj, B, S, tq, tk):
    nq = S // tq
    return pl.pallas_call(
        functools.partial(_mla_kernel, nk=S // tk, tk=tk),
        grid=(B, B_HEADS, nq),
        in_specs=[
            pl.BlockSpec((None, None, tq, QK_DIM), lambda b, h, i: (b, h, i, 0)),
            pl.BlockSpec((None, None, S, QK_DIM), lambda b, h, i: (b, h, 0, 0)),
            pl.BlockSpec((None, None, S, D_V), lambda b, h, i: (b, h, 0, 0)),
            pl.BlockSpec((None, tq, LANES), lambda b, h, i: (b, i, G_BG + h)),
        ],
        out_specs=pl.BlockSpec((None, tq, D_V), lambda b, h, i: (b, i, h)),
        out_shape=jax.ShapeDtypeStruct((B, S, B_WIDTH), BF16),
        scratch_shapes=[pltpu.VMEM((tq, 1), F32), pltpu.VMEM((tq, 1), F32), pltpu.VMEM((tq, D_V), F32)],
        compiler_params=_params(("parallel", "parallel", "arbitrary")),
        name="latent_attn",
    )(q, k, v, proj)


def _t5_bucket(rel):
    nb = N_BUCKETS // 2
    max_exact = nb // 2
    bucket = jnp.where(rel > 0, nb, 0)
    n = jnp.abs(rel)
    nf = jnp.maximum(n, 1).astype(F32)
    large = max_exact + (jnp.log(nf / max_exact) / math.log(MAX_DISTANCE / max_exact)
                         * (nb - max_exact)).astype(jnp.int32)
    large = jnp.minimum(large, nb - 1)
    return bucket + jnp.where(n < max_exact, n, large)


def _dil_bias_tiles(rel_bias, g, dil, tq, tk):
    qi = jnp.arange(tq)[:, None]
    kj = jnp.arange(tk)[None, :]
    table = rel_bias.astype(F32)[:, g * A_HEADS:(g + 1) * A_HEADS]
    tiles = []
    for off in (0, DIL_HALF, 2 * DIL_HALF):
        rel = kj - qi - off
        b = table[_t5_bucket(rel * dil)]
        b = jnp.where((jnp.abs(rel) <= DIL_HALF)[:, :, None], b, NEG)
        tiles.append(b.transpose(2, 0, 1))
    return jnp.stack(tiles)


def _dil_kernel(q_ref, k_ref, v_ref, bias_ref, o_ref, lse_ref, *, L, tq, tk):
    h = pl.program_id(2)
    nb = L // tq
    scale = A_HD ** -0.5
    lane = lax.broadcasted_iota(jnp.int32, (tq, LANES), 1)
    seg = (lane // (LANES // A_HEADS)) == h

    def body(n, carry):
        q0 = pl.multiple_of(n * tq, tq)
        ks = pl.multiple_of(jnp.clip(q0 - DIL_HALF, 0, L - tk), DIL_HALF)
        placement = jnp.where(n == 0, 0, jnp.where(n == nb - 1, 2, 1))
        s = lax.dot_general(q_ref[pl.ds(q0, tq), :], k_ref[pl.ds(ks, tk), :], _NT,
                            preferred_element_type=F32)
        s = s * scale + bias_ref[placement]
        m = jnp.max(s, axis=1, keepdims=True)
        p = jnp.exp(s - m)
        den = jnp.sum(p, axis=1, keepdims=True)
        o = jnp.dot(p.astype(BF16), v_ref[pl.ds(ks, tk), :], preferred_element_type=F32) / den
        o_ref[pl.ds(q0, tq), :] = o.astype(o_ref.dtype)
        lse = jnp.broadcast_to(m + jnp.log(den), (tq, LANES))

        @pl.when(h == 0)
        def _():
            lse_ref[pl.ds(q0, tq), :] = lse

        @pl.when(h != 0)
        def _():
            lse_ref[pl.ds(q0, tq), :] = jnp.where(seg, lse, lse_ref[pl.ds(q0, tq), :])

        return carry

    lax.fori_loop(0, nb, body, 0)


def _dilated(proj, bias, g, dil, B, S, tq):
    L = S // dil
    tk = tq + 2 * DIL_HALF
    assert L >= 2 * tq and L % tq == 0, (S, dil, tq)
    view = proj.reshape(B, L, dil * D_INP)
    col = lambda g0: (lambda b, r, h: (b, 0, r * N_GROUPS + g0 + h))
    o, lse = pl.pallas_call(
        functools.partial(_dil_kernel, L=L, tq=tq, tk=tk),
        grid=(B, dil, A_HEADS),
        in_specs=[
            pl.BlockSpec((None, L, LANES), col(G_AQ + g * A_HEADS)),
            pl.BlockSpec((None, L, LANES), col(G_AK)),
            pl.BlockSpec((None, L, LANES), col(G_AV)),
            pl.BlockSpec((3, None, tq, tk), lambda b, r, h: (0, h, 0, 0)),
        ],
        out_specs=[
            pl.BlockSpec((None, L, LANES), lambda b, r, h: (b, 0, r * A_HEADS + h)),
            pl.BlockSpec((None, L, LANES), lambda b, r, h: (b, 0, r)),
        ],
        out_shape=[
            jax.ShapeDtypeStruct((B, L, dil * A_WIDTH), BF16),
            jax.ShapeDtypeStruct((B, L, dil * LANES), F32),
        ],
        compiler_params=_params(("parallel", "parallel", "arbitrary")),
        name=f"dilated_attn_d{dil}",
    )(view, view, view, bias)
    return o.reshape(B, S, A_WIDTH), lse.reshape(B, S, LANES)


def _memkv_kernel(mem_ref, g_ref, w_ref, o_ref):
    x = mem_ref[...]
    ms = jnp.mean(x * x, axis=-1, keepdims=True)
    hn = (x * lax.rsqrt(ms + EPS) * g_ref[...]).astype(BF16)
    o_ref[...] = jnp.dot(hn, w_ref[...], preferred_element_type=F32).astype(o_ref.dtype)


def _memkv(mem, gain, w_bf16):
    B, M, _ = mem.shape
    return pl.pallas_call(
        _memkv_kernel,
        grid=(B,),
        in_specs=[
            pl.BlockSpec((None, M, D_MODEL), lambda b: (b, 0, 0)),
            pl.BlockSpec((1, D_MODEL), lambda b: (0, 0)),
            pl.BlockSpec((D_MODEL, 2 * C_WIDTH), lambda b: (0, 0)),
        ],
        out_specs=pl.BlockSpec((None, M, 2 * C_WIDTH), lambda b: (b, 0, 0)),
        out_shape=jax.ShapeDtypeStruct((B, M, 2 * C_WIDTH), BF16),
        compiler_params=_params(("parallel",)),
        name="mem_kv_proj",
    )(mem, gain, w_bf16)


def _out_kernel(oa0_ref, oa1_ref, oa2_ref, l0_ref, l1_ref, l2_ref, ag_ref, yb_ref, cq_ref, cg_ref,
                mkv_ref, w_ref, pg_ref, x_ref, o_ref, y_ref):
    tm = x_ref.shape[0]
    lses = (l0_ref[...], l1_ref[...], l2_ref[...])
    mx = jnp.maximum(jnp.maximum(lses[0], lses[1]), lses[2])
    es = [jnp.exp(l - mx) for l in lses]
    inv = 1.0 / (es[0] + es[1] + es[2])
    alphas = [e * inv for e in es]
    oas = (oa0_ref, oa1_ref, oa2_ref)
    seg = LANES // A_HEADS
    for h in range(A_HEADS):
        cs = slice(h * A_HD, (h + 1) * A_HD)
        oa = jnp.zeros((tm, A_HD), F32)
        for g in range(N_DIL):
            a = jnp.broadcast_to(alphas[g][:, h * seg:h * seg + 1], (tm, A_HD))
            oa = oa + a * oas[g][:, cs].astype(F32)
        y_ref[:, cs] = (oa * _silu(ag_ref[:, cs].astype(F32))).astype(BF16)
    y_ref[:, A_WIDTH:A_WIDTH + B_WIDTH] = yb_ref[...]
    scale = C_HD ** -0.5
    for h in range(C_HEADS):
        cs = slice(h * C_HD, (h + 1) * C_HD)
        s = lax.dot_general(cq_ref[:, cs], mkv_ref[:, cs], _NT, preferred_element_type=F32) * scale
        m = jnp.max(s, axis=1, keepdims=True)
        p = jnp.exp(s - m)
        den = jnp.sum(p, axis=1, keepdims=True)
        mv = mkv_ref[:, C_WIDTH + h * C_HD:C_WIDTH + (h + 1) * C_HD]
        oc = jnp.dot(p.astype(BF16), mv, preferred_element_type=F32) / den
        y_ref[:, A_WIDTH + B_WIDTH + h * C_HD:A_WIDTH + B_WIDTH + (h + 1) * C_HD] = (
            oc * _silu(cg_ref[:, cs].astype(F32))).astype(BF16)
    y = jnp.dot(y_ref[...], w_ref[...], preferred_element_type=F32)
    ms = jnp.mean(y * y, axis=-1, keepdims=True)
    o_ref[...] = x_ref[...] + y * lax.rsqrt(ms + EPS) * pg_ref[...]


def _outproj(oas, lses, proj, yb, mkv, w_out, post_gain, x, tm):
    B, S, _ = x.shape
    M = mkv.shape[1]
    tok = lambda width, blk: pl.BlockSpec((None, tm, width), lambda b, s: (b, s, blk))
    const = lambda shape: pl.BlockSpec(shape, lambda b, s: (0,) * len(shape))
    return pl.pallas_call(
        _out_kernel,
        grid=(B, S // tm),
        in_specs=[
            tok(A_WIDTH, 0), tok(A_WIDTH, 0), tok(A_WIDTH, 0),
            tok(LANES, 0), tok(LANES, 0), tok(LANES, 0),
            tok(A_WIDTH, G_AG * LANES // A_WIDTH),
            tok(B_WIDTH, 0),
            tok(C_WIDTH, G_CQ * LANES // C_WIDTH),
            tok(C_WIDTH, G_CG * LANES // C_WIDTH),
            pl.BlockSpec((None, M, 2 * C_WIDTH), lambda b, s: (b, 0, 0)),
            const((D_MIX, D_MODEL)),
            const((1, D_MODEL)),
            tok(D_MODEL, 0),
        ],
        out_specs=tok(D_MODEL, 0),
        out_shape=jax.ShapeDtypeStruct((B, S, D_MODEL), x.dtype),
        scratch_shapes=[pltpu.VMEM((tm, D_MIX), BF16)],
        compiler_params=_params(("parallel", "parallel")),
        name="outproj",
    )(*oas, *lses, proj, yb, proj, proj, mkv, w_out, post_gain, x)


def _swap_halves(w):
    half = D_ROPE // 2
    return jnp.concatenate([w[..., half:], w[..., :half]], axis=-1)


def _prep_weights(w_in, w_uq, w_ukv, w_mem_kv, w_out):
    o_kr = N_DIL * A_WIDTH + 3 * A_WIDTH + Q_LORA + KV_LORA
    o_bg = o_kr + D_ROPE
    o_cq = o_bg + B_WIDTH
    kr = w_in[:, o_kr:o_bg]
    w_in_r = jnp.concatenate(
        [w_in[:, :o_kr], w_in[:, o_cq:], w_in[:, o_bg:o_cq], kr, _swap_halves(kr)], axis=1).astype(BF16)
    wq = w_uq.reshape(Q_LORA, B_HEADS, D_NOPE + D_ROPE)
    rope = wq[:, :, D_NOPE:]
    wq_ext = jnp.concatenate([wq[:, :, :D_NOPE], rope, _swap_halves(rope)], axis=-1)
    wq_ext = wq_ext.reshape(Q_LORA, B_HEADS * QK_DIM).astype(BF16)
    return w_in_r, wq_ext, w_ukv.astype(BF16), w_mem_kv.astype(BF16), w_out.astype(BF16)


def _rope_tables(S):
    inv = ROPE_BASE ** (-jnp.arange(0, D_ROPE, 2, dtype=F32) / D_ROPE)
    ang = jnp.arange(S, dtype=F32)[:, None] * inv[None, :]
    cos, sin = jnp.cos(ang), jnp.sin(ang)
    zeros = jnp.zeros((S, LANES - D_ROPE), F32)
    return (jnp.concatenate([cos, cos, zeros], axis=1),
            jnp.concatenate([-sin, sin, zeros], axis=1))


def _layer(x, mem, pre_gain, q_gain, kv_gain, mem_gain, post_gain, rel_bias, weights):
    w_in_r, wq_ext, wkv, wmem, wout = weights
    B, S, _ = x.shape
    t = _tiles(S)
    proj = _inproj(x.reshape(B * S, D_MODEL), pre_gain, w_in_r, t["tm_in"], t["tn_in"])
    proj = proj.reshape(B, S, D_INP)
    cos_tab, sin_tab = _rope_tables(S)
    q, k, v = _latent(proj, q_gain, kv_gain, wq_ext, wkv, cos_tab, sin_tab, B, S, t["tm_lat"])
    yb = _mla(q, k, v, proj, B, S, t["tq_mla"], t["tk_mla"])
    oas, lses = [], []
    for g, (_, dil) in enumerate(DIL_PAIRS):
        tq = t["tq_dil"]
        bias = _dil_bias_tiles(rel_bias, g, dil, tq, tq + 2 * DIL_HALF)
        o, lse = _dilated(proj, bias, g, dil, B, S, tq)
        oas.append(o)
        lses.append(lse)
    mkv = _memkv(mem, mem_gain, wmem)
    return _outproj(oas, lses, proj, yb, mkv, wout, post_gain, x, t["tm_out"])


def kernel(x_prompt, x_sample, mem_prompt, mem_sample, pre_gain, w_in, q_gain, w_uq, kv_gain, w_ukv,
           mem_gain, w_mem_kv, w_out, post_gain, rel_bias):
    depth = pre_gain.shape[0]

    def trunk(x, mem):
        for l in range(depth):
            weights = _prep_weights(w_in[l], w_uq[l], w_ukv[l], w_mem_kv[l], w_out[l])
            x = _layer(x, mem, pre_gain[l][None], q_gain[l][None], kv_gain[l][None], mem_gain[l][None],
                       post_gain[l][None], rel_bias, weights)
        return x

    return trunk(x_prompt, mem_prompt), trunk(x_sample, mem_sample)
```

```python
import functools
import math

import jax
import jax.numpy as jnp
import numpy as np
from jax import lax
from jax.experimental import pallas as pl
from jax.experimental.pallas import tpu as pltpu

D_MODEL = 2048
DIL_PAIRS = ((128, 1), (512, 4), (2048, 16))
N_DIL = 3
DIL_HALF = 64
A_HEADS = 4
A_HD = 128
A_WIDTH = A_HEADS * A_HD
B_HEADS = 8
Q_LORA = 512
KV_LORA = 512
D_NOPE = 128
D_ROPE = 64
D_V = 128
B_WIDTH = B_HEADS * D_V
ROPE_BASE = 10000.0
C_HEADS = 4
C_HD = 128
C_WIDTH = C_HEADS * C_HD
D_MIX = A_WIDTH + B_WIDTH + C_WIDTH
N_BUCKETS = 32
MAX_DISTANCE = 1024
EPS = 1e-6
NEG = -1e30

LANES = 128
VMEM_LIMIT_BYTES = 56 * 1024 * 1024

G_AQ = (0, 4, 16)
G_AK = 8
G_AV = 12
G_AG = 20
DIL_SLICE = {4: (4, 16), 16: (8, 20)}
G_BCQ = 24
G_BCKV = 28
G_CQ = 32
G_CG = 36
G_BG = 40
G_KR = 48
N_GROUPS = 49
D_INP = N_GROUPS * LANES
QK_DIM = 2 * LANES

F32 = jnp.float32
BF16 = jnp.bfloat16
_NT = (((1,), (1,)), ((), ()))


def _tiles(S):
    return dict(
        tm_in=min(1024, S),
        tn_in=7 * LANES,
        tm_lat=min(512, S),
        tq_mla=min(512, S),
        tk_mla=min(512, S),
        tq_dil=128,
        tm_out=min(512, S),
    )


def _params(sem):
    return pltpu.CompilerParams(dimension_semantics=sem, vmem_limit_bytes=VMEM_LIMIT_BYTES)


def _silu(g):
    return g * (1.0 / (1.0 + jnp.exp(-g)))


def _inproj_kernel(x_ref, g_ref, w_ref, o_ref, h_ref, *, rows):
    @pl.when(pl.program_id(1) == 0)
    def _():
        tm = x_ref.shape[0]
        for r0 in range(0, tm, rows):
            x = x_ref[r0:r0 + rows, :]
            ms = jnp.mean(x * x, axis=-1, keepdims=True)
            h_ref[r0:r0 + rows, :] = (x * lax.rsqrt(ms + EPS) * g_ref[...]).astype(BF16)

    o_ref[...] = jnp.dot(h_ref[...], w_ref[...], preferred_element_type=F32).astype(o_ref.dtype)


def _inproj(x2d, gain, w_bf16, tm, tn):
    T = x2d.shape[0]
    return pl.pallas_call(
        functools.partial(_inproj_kernel, rows=min(256, tm)),
        grid=(T // tm, D_INP // tn),
        in_specs=[
            pl.BlockSpec((tm, D_MODEL), lambda i, j: (i, 0)),
            pl.BlockSpec((1, D_MODEL), lambda i, j: (0, 0)),
            pl.BlockSpec((D_MODEL, tn), lambda i, j: (0, j)),
        ],
        out_specs=pl.BlockSpec((tm, tn), lambda i, j: (i, j)),
        out_shape=jax.ShapeDtypeStruct((T, D_INP), BF16),
        scratch_shapes=[pltpu.VMEM((tm, D_MODEL), BF16)],
        compiler_params=_params(("parallel", "arbitrary")),
        name="inproj",
    )(x2d, gain, w_bf16)


def _latent_kernel(cq_ref, ckv_ref, kr_ref, qg_ref, kvg_ref, wq_ref, wkv_ref, cos_ref, sin_ref,
                   q_ref, k_ref, v_ref, *, q_scale):
    def rms(c_ref, gain_ref):
        c = c_ref[...].astype(F32)
        ms = jnp.mean(c * c, axis=-1, keepdims=True)
        return (c * lax.rsqrt(ms + EPS) * gain_ref[...]).astype(BF16)

    cos = cos_ref[...]
    sin = sin_ref[...]

    def rotary(g):
        return g * cos + pltpu.roll(g, D_ROPE, 1) * sin

    qf = jnp.dot(rms(cq_ref, qg_ref), wq_ref[...], preferred_element_type=F32)
    kvf = jnp.dot(rms(ckv_ref, kvg_ref), wkv_ref[...], preferred_element_type=F32)
    k_rot = rotary(kr_ref[...].astype(F32)).astype(BF16)
    for h in range(B_HEADS):
        c0 = h * QK_DIM
        q_ref[h, :, 0:LANES] = (qf[:, c0:c0 + LANES] * q_scale).astype(BF16)
        q_ref[h, :, LANES:QK_DIM] = (rotary(qf[:, c0 + LANES:c0 + QK_DIM]) * q_scale).astype(BF16)
        k_ref[h, :, 0:LANES] = kvf[:, c0:c0 + LANES].astype(BF16)
        k_ref[h, :, LANES:QK_DIM] = k_rot
        v_ref[h, :, :] = kvf[:, c0 + LANES:c0 + QK_DIM].astype(BF16)


def _latent(proj, q_gain, kv_gain, wq_ext, wkv, cos_tab, sin_tab, B, S, tm):
    q_scale = (D_NOPE + D_ROPE) ** -0.5 * math.log2(math.e)
    ns = S // tm
    col = lambda g: (lambda b, s: (b, s, g))
    row = lambda b, s: (0, 0)
    return pl.pallas_call(
        functools.partial(_latent_kernel, q_scale=q_scale),
        grid=(B, ns),
        in_specs=[
            pl.BlockSpec((None, tm, Q_LORA), col(G_BCQ * LANES // Q_LORA)),
            pl.BlockSpec((None, tm, KV_LORA), col(G_BCKV * LANES // KV_LORA)),
            pl.BlockSpec((None, tm, LANES), col(G_KR)),
            pl.BlockSpec((1, Q_LORA), row),
            pl.BlockSpec((1, KV_LORA), row),
            pl.BlockSpec((Q_LORA, B_HEADS * QK_DIM), row),
            pl.BlockSpec((KV_LORA, B_HEADS * QK_DIM), row),
            pl.BlockSpec((tm, LANES), lambda b, s: (s, 0)),
            pl.BlockSpec((tm, LANES), lambda b, s: (s, 0)),
        ],
        out_specs=[
            pl.BlockSpec((None, B_HEADS, tm, QK_DIM), lambda b, s: (b, 0, s, 0)),
            pl.BlockSpec((None, B_HEADS, tm, QK_DIM), lambda b, s: (b, 0, s, 0)),
            pl.BlockSpec((None, B_HEADS, tm, D_V), lambda b, s: (b, 0, s, 0)),
        ],
        out_shape=[
            jax.ShapeDtypeStruct((B, B_HEADS, S, QK_DIM), BF16),
            jax.ShapeDtypeStruct((B, B_HEADS, S, QK_DIM), BF16),
            jax.ShapeDtypeStruct((B, B_HEADS, S, D_V), BF16),
        ],
        compiler_params=_params(("parallel", "parallel")),
        name="latent_proj",
    )(proj, proj, proj, q_gain, kv_gain, wq_ext, wkv, cos_tab, sin_tab)


def _mla_kernel(q_ref, k_ref, v_ref, g_ref, o_ref, s_ref, mrun_ref, mfin_ref, *, nk, tk):
    tq = q_ref.shape[0]
    nt = tk // LANES

    @pl.when(pl.program_id(0) == 0)
    def _():
        s_ref[...] = jnp.zeros(s_ref.shape, F32)
        mfin_ref[...] = jnp.zeros(mfin_ref.shape, F32)

    q = q_ref[...]
    m_prev = mfin_ref[...]
    l_run = None
    acc = None
    for j in range(nk):
        ks = slice(j * tk, (j + 1) * tk)
        ps = []
        for t in range(nt):
            p = jnp.exp2(s_ref[:, j * tk + t * LANES:j * tk + (t + 1) * LANES] - m_prev)
            l_run = p if l_run is None else l_run + p
            ps.append(p.astype(BF16))
        pv = jnp.dot(jnp.concatenate(ps, axis=1), v_ref[ks, :], preferred_element_type=F32)
        acc = pv if acc is None else acc + pv
        s = lax.dot_general(q, k_ref[ks, :], _NT, preferred_element_type=F32)
        s_ref[:, ks] = s
        m_blk = s[:, 0:LANES]
        for t in range(1, nt):
            m_blk = jnp.maximum(m_blk, s[:, t * LANES:(t + 1) * LANES])
        mrun_ref[...] = m_blk if j == 0 else jnp.maximum(mrun_ref[...], m_blk)
    mfin_ref[...] = jnp.broadcast_to(jnp.max(mrun_ref[...], axis=1, keepdims=True), (tq, LANES))
    o = acc / jnp.sum(l_run, axis=1, keepdims=True)
    o_ref[...] = (o * _silu(g_ref[...].astype(F32))).astype(o_ref.dtype)


def _mla(q, k, v, proj, B, S, tq, tk):
    nq = S // tq
    n_tiles = B * B_HEADS * nq

    def tile(n):
        return n // (B_HEADS * nq), (n // nq) % B_HEADS, n % nq

    def cur(n):
        return tile(jnp.minimum(n, n_tiles - 1))

    def prev(n):
        return tile(jnp.maximum(n - 1, 0))

    def q_map(n):
        b, h, i = cur(n)
        return b, h, i, 0

    def k_map(n):
        b, h, _ = cur(n)
        return b, h, 0, 0

    def v_map(n):
        b, h, _ = prev(n)
        return b, h, 0, 0

    def g_map(n):
        b, h, i = prev(n)
        return b, i, G_BG + h

    def o_map(n):
        b, h, i = prev(n)
        return b, i, h

    return pl.pallas_call(
        functools.partial(_mla_kernel, nk=S // tk, tk=tk),
        grid=(n_tiles + 1,),
        in_specs=[
            pl.BlockSpec((None, None, tq, QK_DIM), q_map),
            pl.BlockSpec((None, None, S, QK_DIM), k_map),
            pl.BlockSpec((None, None, S, D_V), v_map),
            pl.BlockSpec((None, tq, LANES), g_map),
        ],
        out_specs=pl.BlockSpec((None, tq, D_V), o_map),
        out_shape=jax.ShapeDtypeStruct((B, S, B_WIDTH), BF16),
        scratch_shapes=[pltpu.VMEM((tq, S), F32), pltpu.VMEM((tq, LANES), F32), pltpu.VMEM((tq, LANES), F32)],
        compiler_params=_params(("arbitrary",)),
        name="latent_attn",
    )(q, k, v, proj)


def _t5_bucket(rel):
    nb = N_BUCKETS // 2
    max_exact = nb // 2
    bucket = jnp.where(rel > 0, nb, 0)
    n = jnp.abs(rel)
    nf = jnp.maximum(n, 1).astype(F32)
    large = max_exact + (jnp.log(nf / max_exact) / math.log(MAX_DISTANCE / max_exact)
                         * (nb - max_exact)).astype(jnp.int32)
    large = jnp.minimum(large, nb - 1)
    return bucket + jnp.where(n < max_exact, n, large)


def _dil_bias_tiles(rel_bias, g, dil, tq, tk):
    qi = jnp.arange(tq)[:, None]
    kj = jnp.arange(tk)[None, :]
    table = rel_bias.astype(F32)[:, g * A_HEADS:(g + 1) * A_HEADS]
    rel = jnp.stack([kj - qi - off for off in (0, DIL_HALF, 2 * DIL_HALF)])
    bucket = _t5_bucket(rel * dil)
    b = jnp.zeros((3, A_HEADS, tq, tk), F32)
    for i in range(N_BUCKETS):
        b = jnp.where((bucket == i)[:, None], table[i][None, :, None, None], b)
    return jnp.where((jnp.abs(rel) <= DIL_HALF)[:, None], b, NEG)


def _dil_kernel(q_ref, k_ref, v_ref, bias_ref, o_ref, lse_ref, *, L, tq, tk):
    h = pl.program_id(2)
    nb = L // tq
    scale = A_HD ** -0.5
    lane = lax.broadcasted_iota(jnp.int32, (tq, LANES), 1)
    seg = (lane // (LANES // A_HEADS)) == h

    def body(n, carry):
        q0 = pl.multiple_of(n * tq, tq)
        ks = pl.multiple_of(jnp.clip(q0 - DIL_HALF, 0, L - tk), DIL_HALF)
        placement = jnp.where(n == 0, 0, jnp.where(n == nb - 1, 2, 1))
        s = lax.dot_general(q_ref[pl.ds(q0, tq), :], k_ref[pl.ds(ks, tk), :], _NT,
                            preferred_element_type=F32)
        s = s * scale + bias_ref[placement]
        m = jnp.max(s, axis=1, keepdims=True)
        p = jnp.exp(s - m)
        den = jnp.sum(p, axis=1, keepdims=True)
        o = jnp.dot(p.astype(BF16), v_ref[pl.ds(ks, tk), :], preferred_element_type=F32) / den
        o_ref[pl.ds(q0, tq), :] = o.astype(o_ref.dtype)
        lse = jnp.broadcast_to(m + jnp.log(den), (tq, LANES))

        @pl.when(h == 0)
        def _():
            lse_ref[pl.ds(q0, tq), :] = lse

        @pl.when(h != 0)
        def _():
            lse_ref[pl.ds(q0, tq), :] = jnp.where(seg, lse, lse_ref[pl.ds(q0, tq), :])

        return carry

    lax.fori_loop(0, nb, body, 0)


def _dilated(proj, bias, g, dil, B, S, tq):
    L = S // dil
    tk = tq + 2 * DIL_HALF
    assert L >= 2 * tq and L % tq == 0, (S, dil, tq)
    g_lo, g_hi = DIL_SLICE.get(dil, (0, N_GROUPS))
    ng = g_hi - g_lo
    view = proj[:, :, g_lo * LANES:g_hi * LANES].reshape(B, L, dil * ng * LANES)
    col = lambda g0: (lambda b, r, h: (b, 0, r * ng + g0 - g_lo + h))
    o, lse = pl.pallas_call(
        functools.partial(_dil_kernel, L=L, tq=tq, tk=tk),
        grid=(B, dil, A_HEADS),
        in_specs=[
            pl.BlockSpec((None, L, LANES), col(G_AQ[g])),
            pl.BlockSpec((None, L, LANES), col(G_AK)),
            pl.BlockSpec((None, L, LANES), col(G_AV)),
            pl.BlockSpec((3, None, tq, tk), lambda b, r, h: (0, h, 0, 0)),
        ],
        out_specs=[
            pl.BlockSpec((None, L, LANES), lambda b, r, h: (b, 0, r * A_HEADS + h)),
            pl.BlockSpec((None, L, LANES), lambda b, r, h: (b, 0, r)),
        ],
        out_shape=[
            jax.ShapeDtypeStruct((B, L, dil * A_WIDTH), BF16),
            jax.ShapeDtypeStruct((B, L, dil * LANES), F32),
        ],
        compiler_params=_params(("parallel", "parallel", "arbitrary")),
        name=f"dilated_attn_d{dil}",
    )(view, view, view, bias)
    return o.reshape(B, S, A_WIDTH), lse.reshape(B, S, LANES)


def _memkv_kernel(mem_ref, g_ref, w_ref, o_ref):
    x = mem_ref[...]
    ms = jnp.mean(x * x, axis=-1, keepdims=True)
    hn = (x * lax.rsqrt(ms + EPS) * g_ref[...]).astype(BF16)
    o_ref[...] = jnp.dot(hn, w_ref[...], preferred_element_type=F32).astype(o_ref.dtype)


def _memkv(mem, gain, w_bf16):
    B, M, _ = mem.shape
    return pl.pallas_call(
        _memkv_kernel,
        grid=(B,),
        in_specs=[
            pl.BlockSpec((None, M, D_MODEL), lambda b: (b, 0, 0)),
            pl.BlockSpec((1, D_MODEL), lambda b: (0, 0)),
            pl.BlockSpec((D_MODEL, 2 * C_WIDTH), lambda b: (0, 0)),
        ],
        out_specs=pl.BlockSpec((None, M, 2 * C_WIDTH), lambda b: (b, 0, 0)),
        out_shape=jax.ShapeDtypeStruct((B, M, 2 * C_WIDTH), BF16),
        compiler_params=_params(("parallel",)),
        name="mem_kv_proj",
    )(mem, gain, w_bf16)


def _out_kernel(oa0_ref, oa1_ref, oa2_ref, l0_ref, l1_ref, l2_ref, ag_ref, yb_ref, cq_ref, cg_ref,
                mkv_ref, w_ref, pg_ref, x_ref, o_ref, y_ref):
    tm = x_ref.shape[0]
    lses = (l0_ref[...], l1_ref[...], l2_ref[...])
    mx = jnp.maximum(jnp.maximum(lses[0], lses[1]), lses[2])
    es = [jnp.exp(l - mx) for l in lses]
    inv = 1.0 / (es[0] + es[1] + es[2])
    alphas = [e * inv for e in es]
    oas = (oa0_ref, oa1_ref, oa2_ref)
    seg = LANES // A_HEADS
    for h in range(A_HEADS):
        cs = slice(h * A_HD, (h + 1) * A_HD)
        oa = jnp.zeros((tm, A_HD), F32)
        for g in range(N_DIL):
            a = jnp.broadcast_to(alphas[g][:, h * seg:h * seg + 1], (tm, A_HD))
            oa = oa + a * oas[g][:, cs].astype(F32)
        y_ref[:, cs] = (oa * _silu(ag_ref[:, cs].astype(F32))).astype(BF16)
    y_ref[:, A_WIDTH:A_WIDTH + B_WIDTH] = yb_ref[...]
    scale = C_HD ** -0.5
    for h in range(C_HEADS):
        cs = slice(h * C_HD, (h + 1) * C_HD)
        s = lax.dot_general(cq_ref[:, cs], mkv_ref[:, cs], _NT, preferred_element_type=F32) * scale
        m = jnp.max(s, axis=1, keepdims=True)
        p = jnp.exp(s - m)
        den = jnp.sum(p, axis=1, keepdims=True)
        mv = mkv_ref[:, C_WIDTH + h * C_HD:C_WIDTH + (h + 1) * C_HD]
        oc = jnp.dot(p.astype(BF16), mv, preferred_element_type=F32) / den
        y_ref[:, A_WIDTH + B_WIDTH + h * C_HD:A_WIDTH + B_WIDTH + (h + 1) * C_HD] = (
            oc * _silu(cg_ref[:, cs].astype(F32))).astype(BF16)
    y = jnp.dot(y_ref[...], w_ref[...], preferred_element_type=F32)
    ms = jnp.mean(y * y, axis=-1, keepdims=True)
    o_ref[...] = x_ref[...] + y * lax.rsqrt(ms + EPS) * pg_ref[...]


def _outproj(oas, lses, proj, yb, mkv, w_out, post_gain, x, tm):
    B, S, _ = x.shape
    M = mkv.shape[1]
    tok = lambda width, blk: pl.BlockSpec((None, tm, width), lambda b, s: (b, s, blk))
    const = lambda shape: pl.BlockSpec(shape, lambda b, s: (0,) * len(shape))
    return pl.pallas_call(
        _out_kernel,
        grid=(B, S // tm),
        in_specs=[
            tok(A_WIDTH, 0), tok(A_WIDTH, 0), tok(A_WIDTH, 0),
            tok(LANES, 0), tok(LANES, 0), tok(LANES, 0),
            tok(A_WIDTH, G_AG * LANES // A_WIDTH),
            tok(B_WIDTH, 0),
            tok(C_WIDTH, G_CQ * LANES // C_WIDTH),
            tok(C_WIDTH, G_CG * LANES // C_WIDTH),
            pl.BlockSpec((None, M, 2 * C_WIDTH), lambda b, s: (b, 0, 0)),
            const((D_MIX, D_MODEL)),
            const((1, D_MODEL)),
            tok(D_MODEL, 0),
        ],
        out_specs=tok(D_MODEL, 0),
        out_shape=jax.ShapeDtypeStruct((B, S, D_MODEL), x.dtype),
        scratch_shapes=[pltpu.VMEM((tm, D_MIX), BF16)],
        compiler_params=_params(("parallel", "parallel")),
        name="outproj",
    )(*oas, *lses, proj, yb, proj, proj, mkv, w_out, post_gain, x)


def _swap_halves(w):
    half = D_ROPE // 2
    return jnp.concatenate([w[..., half:], w[..., :half]], axis=-1)


def _prep_weights(w_in, w_uq, w_ukv, w_mem_kv, w_out):
    o_kr = N_DIL * A_WIDTH + 3 * A_WIDTH + Q_LORA + KV_LORA
    o_bg = o_kr + D_ROPE
    o_cq = o_bg + B_WIDTH
    kr = w_in[:, o_kr:o_bg]
    aq = [w_in[:, g * A_WIDTH:(g + 1) * A_WIDTH] for g in range(N_DIL)]
    o_ak = N_DIL * A_WIDTH
    w_in_r = jnp.concatenate(
        [aq[0], aq[1], w_in[:, o_ak:o_ak + 2 * A_WIDTH], aq[2], w_in[:, o_ak + 2 * A_WIDTH:o_kr],
         w_in[:, o_cq:], w_in[:, o_bg:o_cq], kr, _swap_halves(kr)], axis=1).astype(BF16)
    wq = w_uq.reshape(Q_LORA, B_HEADS, D_NOPE + D_ROPE)
    rope = wq[:, :, D_NOPE:]
    wq_ext = jnp.concatenate([wq[:, :, :D_NOPE], rope, _swap_halves(rope)], axis=-1)
    wq_ext = wq_ext.reshape(Q_LORA, B_HEADS * QK_DIM).astype(BF16)
    return w_in_r, wq_ext, w_ukv.astype(BF16), w_mem_kv.astype(BF16), w_out.astype(BF16)


def _rope_tables(S):
    inv = ROPE_BASE ** (-jnp.arange(0, D_ROPE, 2, dtype=F32) / D_ROPE)
    ang = jnp.arange(S, dtype=F32)[:, None] * inv[None, :]
    cos, sin = jnp.cos(ang), jnp.sin(ang)
    zeros = jnp.zeros((S, LANES - D_ROPE), F32)
    return (jnp.concatenate([cos, cos, zeros], axis=1),
            jnp.concatenate([-sin, sin, zeros], axis=1))


def _dil_biases(rel_bias, tq):
    return [_dil_bias_tiles(rel_bias, g, dil, tq, tq + 2 * DIL_HALF) for g, (_, dil) in enumerate(DIL_PAIRS)]


def _layer(x, mem, pre_gain, q_gain, kv_gain, mem_gain, post_gain, biases, weights):
    w_in_r, wq_ext, wkv, wmem, wout = weights
    B, S, _ = x.shape
    t = _tiles(S)
    proj = _inproj(x.reshape(B * S, D_MODEL), pre_gain, w_in_r, t["tm_in"], t["tn_in"])
    proj = proj.reshape(B, S, D_INP)
    cos_tab, sin_tab = _rope_tables(S)
    q, k, v = _latent(proj, q_gain, kv_gain, wq_ext, wkv, cos_tab, sin_tab, B, S, t["tm_lat"])
    yb = _mla(q, k, v, proj, B, S, t["tq_mla"], t["tk_mla"])
    oas, lses = [], []
    for g, (_, dil) in enumerate(DIL_PAIRS):
        o, lse = _dilated(proj, biases[g], g, dil, B, S, t["tq_dil"])
        oas.append(o)
        lses.append(lse)
    mkv = _memkv(mem, mem_gain, wmem)
    return _outproj(oas, lses, proj, yb, mkv, wout, post_gain, x, t["tm_out"])


def kernel(x_prompt, x_sample, mem_prompt, mem_sample, pre_gain, w_in, q_gain, w_uq, kv_gain, w_ukv,
           mem_gain, w_mem_kv, w_out, post_gain, rel_bias):
    depth = pre_gain.shape[0]
    weights = [_prep_weights(w_in[l], w_uq[l], w_ukv[l], w_mem_kv[l], w_out[l]) for l in range(depth)]
    biases = _dil_biases(rel_bias, _tiles(x_prompt.shape[1])["tq_dil"])

    def trunk(x, mem):
        for l in range(depth):
            x = _layer(x, mem, pre_gain[l][None], q_gain[l][None], kv_gain[l][None], mem_gain[l][None],
                       post_gain[l][None], biases, weights[l])
        return x

    return trunk(x_prompt, mem_prompt), trunk(x_sample, mem_sample)
```

```python
import functools
import math

import jax
import jax.numpy as jnp
import numpy as np
from jax import lax
from jax.experimental import pallas as pl
from jax.experimental.pallas import tpu as pltpu

D_MODEL = 2048
DIL_PAIRS = ((128, 1), (512, 4), (2048, 16))
N_DIL = 3
DIL_HALF = 64
DIL_UNROLL = 4
A_HEADS = 4
A_HD = 128
A_WIDTH = A_HEADS * A_HD
B_HEADS = 8
Q_LORA = 512
KV_LORA = 512
D_NOPE = 128
D_ROPE = 64
D_V = 128
B_WIDTH = B_HEADS * D_V
ROPE_BASE = 10000.0
C_HEADS = 4
C_HD = 128
C_WIDTH = C_HEADS * C_HD
D_MIX = A_WIDTH + B_WIDTH + C_WIDTH
N_BUCKETS = 32
MAX_DISTANCE = 1024
EPS = 1e-6
NEG = -1e30

LANES = 128
VMEM_LIMIT_BYTES = 56 * 1024 * 1024

G_AQ = (0, 4, 16)
G_AK = 8
G_AV = 12
G_AG = 20
DIL_SLICE = {4: (4, 16), 16: (8, 20)}
G_BCQ = 24
G_BCKV = 28
G_CQ = 32
G_CG = 36
G_BG = 40
G_KR = 48
N_GROUPS = 49
D_INP = N_GROUPS * LANES
QK_DIM = 2 * LANES

F32 = jnp.float32
BF16 = jnp.bfloat16
_NT = (((1,), (1,)), ((), ()))


def _tiles(S):
    return dict(
        tm_in=min(1024, S),
        tn_in=7 * LANES,
        tm_lat=min(512, S),
        tq_mla=min(512, S),
        tk_mla=min(512, S),
        tq_dil=128,
        tm_out=min(512, S),
    )


def _params(sem):
    return pltpu.CompilerParams(dimension_semantics=sem, vmem_limit_bytes=VMEM_LIMIT_BYTES)


def _silu(g):
    return g * (1.0 / (1.0 + jnp.exp(-g)))


def _inproj_kernel(x_ref, g_ref, w_ref, o_ref, h_ref, *, rows):
    @pl.when(pl.program_id(1) == 0)
    def _():
        tm = x_ref.shape[0]
        for r0 in range(0, tm, rows):
            x = x_ref[r0:r0 + rows, :]
            ms = jnp.mean(x * x, axis=-1, keepdims=True)
            h_ref[r0:r0 + rows, :] = (x * lax.rsqrt(ms + EPS) * g_ref[...]).astype(BF16)

    o_ref[...] = jnp.dot(h_ref[...], w_ref[...], preferred_element_type=F32).astype(o_ref.dtype)


def _inproj(x2d, gain, w_bf16, tm, tn):
    T = x2d.shape[0]
    return pl.pallas_call(
        functools.partial(_inproj_kernel, rows=min(256, tm)),
        grid=(T // tm, D_INP // tn),
        in_specs=[
            pl.BlockSpec((tm, D_MODEL), lambda i, j: (i, 0)),
            pl.BlockSpec((1, D_MODEL), lambda i, j: (0, 0)),
            pl.BlockSpec((D_MODEL, tn), lambda i, j: (0, j)),
        ],
        out_specs=pl.BlockSpec((tm, tn), lambda i, j: (i, j)),
        out_shape=jax.ShapeDtypeStruct((T, D_INP), BF16),
        scratch_shapes=[pltpu.VMEM((tm, D_MODEL), BF16)],
        compiler_params=_params(("parallel", "arbitrary")),
        name="inproj",
    )(x2d, gain, w_bf16)


def _latent_kernel(cq_ref, ckv_ref, kr_ref, qg_ref, kvg_ref, wq_ref, wkv_ref, cos_ref, sin_ref,
                   q_ref, k_ref, v_ref, *, q_scale):
    def rms(c_ref, gain_ref):
        c = c_ref[...].astype(F32)
        ms = jnp.mean(c * c, axis=-1, keepdims=True)
        return (c * lax.rsqrt(ms + EPS) * gain_ref[...]).astype(BF16)

    cos = cos_ref[...]
    sin = sin_ref[...]

    def rotary(g):
        return g * cos + pltpu.roll(g, D_ROPE, 1) * sin

    qf = jnp.dot(rms(cq_ref, qg_ref), wq_ref[...], preferred_element_type=F32)
    kvf = jnp.dot(rms(ckv_ref, kvg_ref), wkv_ref[...], preferred_element_type=F32)
    k_rot = rotary(kr_ref[...].astype(F32)).astype(BF16)
    for h in range(B_HEADS):
        c0 = h * QK_DIM
        q_ref[h, :, 0:LANES] = (qf[:, c0:c0 + LANES] * q_scale).astype(BF16)
        q_ref[h, :, LANES:QK_DIM] = (rotary(qf[:, c0 + LANES:c0 + QK_DIM]) * q_scale).astype(BF16)
        k_ref[h, :, 0:LANES] = kvf[:, c0:c0 + LANES].astype(BF16)
        k_ref[h, :, LANES:QK_DIM] = k_rot
        v_ref[h, :, :] = kvf[:, c0 + LANES:c0 + QK_DIM].astype(BF16)


def _latent(proj, q_gain, kv_gain, wq_ext, wkv, cos_tab, sin_tab, B, S, tm):
    q_scale = (D_NOPE + D_ROPE) ** -0.5 * math.log2(math.e)
    ns = S // tm
    col = lambda g: (lambda b, s: (b, s, g))
    row = lambda b, s: (0, 0)
    return pl.pallas_call(
        functools.partial(_latent_kernel, q_scale=q_scale),
        grid=(B, ns),
        in_specs=[
            pl.BlockSpec((None, tm, Q_LORA), col(G_BCQ * LANES // Q_LORA)),
            pl.BlockSpec((None, tm, KV_LORA), col(G_BCKV * LANES // KV_LORA)),
            pl.BlockSpec((None, tm, LANES), col(G_KR)),
            pl.BlockSpec((1, Q_LORA), row),
            pl.BlockSpec((1, KV_LORA), row),
            pl.BlockSpec((Q_LORA, B_HEADS * QK_DIM), row),
            pl.BlockSpec((KV_LORA, B_HEADS * QK_DIM), row),
            pl.BlockSpec((tm, LANES), lambda b, s: (s, 0)),
            pl.BlockSpec((tm, LANES), lambda b, s: (s, 0)),
        ],
        out_specs=[
            pl.BlockSpec((None, B_HEADS, tm, QK_DIM), lambda b, s: (b, 0, s, 0)),
            pl.BlockSpec((None, B_HEADS, tm, QK_DIM), lambda b, s: (b, 0, s, 0)),
            pl.BlockSpec((None, B_HEADS, tm, D_V), lambda b, s: (b, 0, s, 0)),
        ],
        out_shape=[
            jax.ShapeDtypeStruct((B, B_HEADS, S, QK_DIM), BF16),
            jax.ShapeDtypeStruct((B, B_HEADS, S, QK_DIM), BF16),
            jax.ShapeDtypeStruct((B, B_HEADS, S, D_V), BF16),
        ],
        compiler_params=_params(("parallel", "parallel")),
        name="latent_proj",
    )(proj, proj, proj, q_gain, kv_gain, wq_ext, wkv, cos_tab, sin_tab)


def _mla_kernel(q_ref, k_ref, v_ref, g_ref, o_ref, s_ref, mrun_ref, mfin_ref, *, nk, tk):
    tq = q_ref.shape[0]
    nt = tk // LANES

    @pl.when(pl.program_id(0) == 0)
    def _():
        s_ref[...] = jnp.zeros(s_ref.shape, F32)
        mfin_ref[...] = jnp.zeros(mfin_ref.shape, F32)

    q = q_ref[...]
    m_prev = mfin_ref[...]
    l_run = None
    acc = None
    for j in range(nk):
        ks = slice(j * tk, (j + 1) * tk)
        ps = []
        for t in range(nt):
            p = jnp.exp2(s_ref[:, j * tk + t * LANES:j * tk + (t + 1) * LANES] - m_prev)
            l_run = p if l_run is None else l_run + p
            ps.append(p.astype(BF16))
        pv = jnp.dot(jnp.concatenate(ps, axis=1), v_ref[ks, :], preferred_element_type=F32)
        acc = pv if acc is None else acc + pv
        s = lax.dot_general(q, k_ref[ks, :], _NT, preferred_element_type=F32)
        s_ref[:, ks] = s
        m_blk = s[:, 0:LANES]
        for t in range(1, nt):
            m_blk = jnp.maximum(m_blk, s[:, t * LANES:(t + 1) * LANES])
        mrun_ref[...] = m_blk if j == 0 else jnp.maximum(mrun_ref[...], m_blk)
    mfin_ref[...] = jnp.broadcast_to(jnp.max(mrun_ref[...], axis=1, keepdims=True), (tq, LANES))
    o = acc / jnp.sum(l_run, axis=1, keepdims=True)
    o_ref[...] = (o * _silu(g_ref[...].astype(F32))).astype(o_ref.dtype)


def _mla(q, k, v, proj, B, S, tq, tk):
    nq = S // tq
    n_tiles = B * B_HEADS * nq

    def tile(n):
        return n // (B_HEADS * nq), (n // nq) % B_HEADS, n % nq

    def cur(n):
        return tile(jnp.minimum(n, n_tiles - 1))

    def prev(n):
        return tile(jnp.maximum(n - 1, 0))

    def q_map(n):
        b, h, i = cur(n)
        return b, h, i, 0

    def k_map(n):
        b, h, _ = cur(n)
        return b, h, 0, 0

    def v_map(n):
        b, h, _ = prev(n)
        return b, h, 0, 0

    def g_map(n):
        b, h, i = prev(n)
        return b, i, G_BG + h

    def o_map(n):
        b, h, i = prev(n)
        return b, i, h

    return pl.pallas_call(
        functools.partial(_mla_kernel, nk=S // tk, tk=tk),
        grid=(n_tiles + 1,),
        in_specs=[
            pl.BlockSpec((None, None, tq, QK_DIM), q_map),
            pl.BlockSpec((None, None, S, QK_DIM), k_map),
            pl.BlockSpec((None, None, S, D_V), v_map),
            pl.BlockSpec((None, tq, LANES), g_map),
        ],
        out_specs=pl.BlockSpec((None, tq, D_V), o_map),
        out_shape=jax.ShapeDtypeStruct((B, S, B_WIDTH), BF16),
        scratch_shapes=[pltpu.VMEM((tq, S), F32), pltpu.VMEM((tq, LANES), F32), pltpu.VMEM((tq, LANES), F32)],
        compiler_params=_params(("arbitrary",)),
        name="latent_attn",
    )(q, k, v, proj)


def _t5_bucket(rel):
    nb = N_BUCKETS // 2
    max_exact = nb // 2
    bucket = jnp.where(rel > 0, nb, 0)
    n = jnp.abs(rel)
    nf = jnp.maximum(n, 1).astype(F32)
    large = max_exact + (jnp.log(nf / max_exact) / math.log(MAX_DISTANCE / max_exact)
                         * (nb - max_exact)).astype(jnp.int32)
    large = jnp.minimum(large, nb - 1)
    return bucket + jnp.where(n < max_exact, n, large)


def _dil_bias_tiles(rel_bias, g, dil, tq, tk):
    qi = jnp.arange(tq)[:, None]
    kj = jnp.arange(tk)[None, :]
    table = rel_bias.astype(F32)[:, g * A_HEADS:(g + 1) * A_HEADS]
    rel = jnp.stack([kj - qi - off for off in (0, DIL_HALF, 2 * DIL_HALF)])
    bucket = _t5_bucket(rel * dil)
    b = jnp.zeros((3, A_HEADS, tq, tk), F32)
    for i in range(N_BUCKETS):
        b = jnp.where((bucket == i)[:, None], table[i][None, :, None, None], b)
    return jnp.where((jnp.abs(rel) <= DIL_HALF)[:, None], b, NEG)


def _dil_kernel(q_ref, k_ref, v_ref, bias_ref, o_ref, lse_ref, *, L, dil, tq, tk, unroll):
    h = pl.program_id(1)
    r = pl.program_id(2)
    nb = L // tq
    scale = A_HD ** -0.5
    lane = lax.broadcasted_iota(jnp.int32, (tq, LANES), 1)
    seg = (lane // (LANES // A_HEADS)) == h

    def body(n, carry):
        q0 = pl.multiple_of(n * tq, tq)
        ks = pl.multiple_of(jnp.clip(q0 - DIL_HALF, 0, L - tk), DIL_HALF)
        placement = jnp.where(n == 0, 0, jnp.where(n == nb - 1, 2, 1))
        s = lax.dot_general(q_ref[pl.ds(q0, tq), :], k_ref[pl.ds(ks, tk), :], _NT,
                            preferred_element_type=F32)
        s = s * scale + bias_ref[placement]
        m = jnp.max(s, axis=1, keepdims=True)
        p = jnp.exp(s - m)
        den = jnp.sum(p, axis=1, keepdims=True)
        o = jnp.dot(p.astype(BF16), v_ref[pl.ds(ks, tk), :], preferred_element_type=F32) / den
        rows = pl.ds(r + q0 * dil, tq, stride=dil) if dil > 1 else pl.ds(q0, tq)
        o_ref[rows, :] = o
        lse = jnp.broadcast_to(m + jnp.log(den), (tq, LANES))
        lse_ref[rows, :] = jnp.where(seg, lse, lse_ref[rows, :])
        return carry

    @pl.when((h == 0) & (r == 0))
    def _():
        lse_ref[...] = jnp.zeros(lse_ref.shape, F32)

    lax.fori_loop(0, nb, body, 0, unroll=min(unroll, nb))


def _dilated(proj, bias, g, dil, B, S, tq):
    L = S // dil
    tk = tq + 2 * DIL_HALF
    assert L >= 2 * tq and L % tq == 0, (S, dil, tq)
    g_lo, g_hi = DIL_SLICE.get(dil, (0, N_GROUPS))
    ng = g_hi - g_lo
    view = proj[:, :, g_lo * LANES:g_hi * LANES].reshape(B, L, dil * ng * LANES)
    col = lambda g0: (lambda b, h, r: (b, 0, r * ng + g0 - g_lo + h))
    return pl.pallas_call(
        functools.partial(_dil_kernel, L=L, dil=dil, tq=tq, tk=tk, unroll=DIL_UNROLL),
        grid=(B, A_HEADS, dil),
        in_specs=[
            pl.BlockSpec((None, L, LANES), col(G_AQ[g])),
            pl.BlockSpec((None, L, LANES), col(G_AK)),
            pl.BlockSpec((None, L, LANES), col(G_AV)),
            pl.BlockSpec((3, None, tq, tk), lambda b, h, r: (0, h, 0, 0)),
        ],
        out_specs=[
            pl.BlockSpec((None, S, LANES), lambda b, h, r: (b, 0, h)),
            pl.BlockSpec((None, S, LANES), lambda b, h, r: (b, 0, 0)),
        ],
        out_shape=[
            jax.ShapeDtypeStruct((B, S, A_WIDTH), F32),
            jax.ShapeDtypeStruct((B, S, LANES), F32),
        ],
        compiler_params=_params(("parallel", "arbitrary", "arbitrary")),
        name=f"dilated_attn_d{dil}",
    )(view, view, view, bias)


def _memkv_kernel(mem_ref, g_ref, w_ref, o_ref):
    x = mem_ref[...]
    ms = jnp.mean(x * x, axis=-1, keepdims=True)
    hn = (x * lax.rsqrt(ms + EPS) * g_ref[...]).astype(BF16)
    o_ref[...] = jnp.dot(hn, w_ref[...], preferred_element_type=F32).astype(o_ref.dtype)


def _memkv(mem, gain, w_bf16):
    B, M, _ = mem.shape
    return pl.pallas_call(
        _memkv_kernel,
        grid=(B,),
        in_specs=[
            pl.BlockSpec((None, M, D_MODEL), lambda b: (b, 0, 0)),
            pl.BlockSpec((1, D_MODEL), lambda b: (0, 0)),
            pl.BlockSpec((D_MODEL, 2 * C_WIDTH), lambda b: (0, 0)),
        ],
        out_specs=pl.BlockSpec((None, M, 2 * C_WIDTH), lambda b: (b, 0, 0)),
        out_shape=jax.ShapeDtypeStruct((B, M, 2 * C_WIDTH), BF16),
        compiler_params=_params(("parallel",)),
        name="mem_kv_proj",
    )(mem, gain, w_bf16)


def _out_kernel(oa0_ref, oa1_ref, oa2_ref, l0_ref, l1_ref, l2_ref, ag_ref, yb_ref, cq_ref, cg_ref,
                mkv_ref, w_ref, pg_ref, x_ref, o_ref, y_ref):
    tm = x_ref.shape[0]
    lses = (l0_ref[...], l1_ref[...], l2_ref[...])
    mx = jnp.maximum(jnp.maximum(lses[0], lses[1]), lses[2])
    es = [jnp.exp(l - mx) for l in lses]
    inv = 1.0 / (es[0] + es[1] + es[2])
    alphas = [e * inv for e in es]
    oas = (oa0_ref, oa1_ref, oa2_ref)
    seg = LANES // A_HEADS
    for h in range(A_HEADS):
        cs = slice(h * A_HD, (h + 1) * A_HD)
        oa = jnp.zeros((tm, A_HD), F32)
        for g in range(N_DIL):
            a = jnp.broadcast_to(alphas[g][:, h * seg:h * seg + 1], (tm, A_HD))
            oa = oa + a * oas[g][:, cs].astype(F32)
        y_ref[:, cs] = (oa * _silu(ag_ref[:, cs].astype(F32))).astype(BF16)
    y_ref[:, A_WIDTH:A_WIDTH + B_WIDTH] = yb_ref[...]
    scale = C_HD ** -0.5
    for h in range(C_HEADS):
        cs = slice(h * C_HD, (h + 1) * C_HD)
        s = lax.dot_general(cq_ref[:, cs], mkv_ref[:, cs], _NT, preferred_element_type=F32) * scale
        m = jnp.max(s, axis=1, keepdims=True)
        p = jnp.exp(s - m)
        den = jnp.sum(p, axis=1, keepdims=True)
        mv = mkv_ref[:, C_WIDTH + h * C_HD:C_WIDTH + (h + 1) * C_HD]
        oc = jnp.dot(p.astype(BF16), mv, preferred_element_type=F32) / den
        y_ref[:, A_WIDTH + B_WIDTH + h * C_HD:A_WIDTH + B_WIDTH + (h + 1) * C_HD] = (
            oc * _silu(cg_ref[:, cs].astype(F32))).astype(BF16)
    y = jnp.dot(y_ref[...], w_ref[...], preferred_element_type=F32)
    ms = jnp.mean(y * y, axis=-1, keepdims=True)
    o_ref[...] = x_ref[...] + y * lax.rsqrt(ms + EPS) * pg_ref[...]


def _outproj(oas, lses, proj, yb, mkv, w_out, post_gain, x, tm):
    B, S, _ = x.shape
    M = mkv.shape[1]
    tok = lambda width, blk: pl.BlockSpec((None, tm, width), lambda b, s: (b, s, blk))
    const = lambda shape: pl.BlockSpec(shape, lambda b, s: (0,) * len(shape))
    return pl.pallas_call(
        _out_kernel,
        grid=(B, S // tm),
        in_specs=[
            tok(A_WIDTH, 0), tok(A_WIDTH, 0), tok(A_WIDTH, 0),
            tok(LANES, 0), tok(LANES, 0), tok(LANES, 0),
            tok(A_WIDTH, G_AG * LANES // A_WIDTH),
            tok(B_WIDTH, 0),
            tok(C_WIDTH, G_CQ * LANES // C_WIDTH),
            tok(C_WIDTH, G_CG * LANES // C_WIDTH),
            pl.BlockSpec((None, M, 2 * C_WIDTH), lambda b, s: (b, 0, 0)),
            const((D_MIX, D_MODEL)),
            const((1, D_MODEL)),
            tok(D_MODEL, 0),
        ],
        out_specs=tok(D_MODEL, 0),
        out_shape=jax.ShapeDtypeStruct((B, S, D_MODEL), x.dtype),
        scratch_shapes=[pltpu.VMEM((tm, D_MIX), BF16)],
        compiler_params=_params(("parallel", "parallel")),
        name="outproj",
    )(*oas, *lses, proj, yb, proj, proj, mkv, w_out, post_gain, x)


def _swap_halves(w):
    half = D_ROPE // 2
    return jnp.concatenate([w[..., half:], w[..., :half]], axis=-1)


def _prep_weights(w_in, w_uq, w_ukv, w_mem_kv, w_out):
    o_kr = N_DIL * A_WIDTH + 3 * A_WIDTH + Q_LORA + KV_LORA
    o_bg = o_kr + D_ROPE
    o_cq = o_bg + B_WIDTH
    kr = w_in[:, o_kr:o_bg]
    aq = [w_in[:, g * A_WIDTH:(g + 1) * A_WIDTH] for g in range(N_DIL)]
    o_ak = N_DIL * A_WIDTH
    w_in_r = jnp.concatenate(
        [aq[0], aq[1], w_in[:, o_ak:o_ak + 2 * A_WIDTH], aq[2], w_in[:, o_ak + 2 * A_WIDTH:o_kr],
         w_in[:, o_cq:], w_in[:, o_bg:o_cq], kr, _swap_halves(kr)], axis=1).astype(BF16)
    wq = w_uq.reshape(Q_LORA, B_HEADS, D_NOPE + D_ROPE)
    rope = wq[:, :, D_NOPE:]
    wq_ext = jnp.concatenate([wq[:, :, :D_NOPE], rope, _swap_halves(rope)], axis=-1)
    wq_ext = wq_ext.reshape(Q_LORA, B_HEADS * QK_DIM).astype(BF16)
    return w_in_r, wq_ext, w_ukv.astype(BF16), w_mem_kv.astype(BF16), w_out.astype(BF16)


def _rope_tables(S):
    inv = ROPE_BASE ** (-jnp.arange(0, D_ROPE, 2, dtype=F32) / D_ROPE)
    ang = jnp.arange(S, dtype=F32)[:, None] * inv[None, :]
    cos, sin = jnp.cos(ang), jnp.sin(ang)
    zeros = jnp.zeros((S, LANES - D_ROPE), F32)
    return (jnp.concatenate([cos, cos, zeros], axis=1),
            jnp.concatenate([-sin, sin, zeros], axis=1))


def _dil_biases(rel_bias, tq):
    return [_dil_bias_tiles(rel_bias, g, dil, tq, tq + 2 * DIL_HALF) for g, (_, dil) in enumerate(DIL_PAIRS)]


def _layer(x, mem, pre_gain, q_gain, kv_gain, mem_gain, post_gain, biases, weights):
    w_in_r, wq_ext, wkv, wmem, wout = weights
    B, S, _ = x.shape
    t = _tiles(S)
    proj = _inproj(x.reshape(B * S, D_MODEL), pre_gain, w_in_r, t["tm_in"], t["tn_in"])
    proj = proj.reshape(B, S, D_INP)
    cos_tab, sin_tab = _rope_tables(S)
    q, k, v = _latent(proj, q_gain, kv_gain, wq_ext, wkv, cos_tab, sin_tab, B, S, t["tm_lat"])
    yb = _mla(q, k, v, proj, B, S, t["tq_mla"], t["tk_mla"])
    oas, lses = [], []
    for g, (_, dil) in enumerate(DIL_PAIRS):
        o, lse = _dilated(proj, biases[g], g, dil, B, S, t["tq_dil"])
        oas.append(o)
        lses.append(lse)
    mkv = _memkv(mem, mem_gain, wmem)
    return _outproj(oas, lses, proj, yb, mkv, wout, post_gain, x, t["tm_out"])


def kernel(x_prompt, x_sample, mem_prompt, mem_sample, pre_gain, w_in, q_gain, w_uq, kv_gain, w_ukv,
           mem_gain, w_mem_kv, w_out, post_gain, rel_bias):
    depth = pre_gain.shape[0]
    weights = [_prep_weights(w_in[l], w_uq[l], w_ukv[l], w_mem_kv[l], w_out[l]) for l in range(depth)]
    biases = _dil_biases(rel_bias, _tiles(x_prompt.shape[1])["tq_dil"])

    def trunk(x, mem):
        for l in range(depth):
            x = _layer(x, mem, pre_gain[l][None], q_gain[l][None], kv_gain[l][None], mem_gain[l][None],
                       post_gain[l][None], biases, weights[l])
        return x

    return trunk(x_prompt, mem_prompt), trunk(x_sample, mem_sample)
```

```python
import functools
import math

import jax
import jax.numpy as jnp
import numpy as np
from jax import lax
from jax.experimental import pallas as pl
from jax.experimental.pallas import tpu as pltpu

D_MODEL = 2048
DIL_PAIRS = ((128, 1), (512, 4), (2048, 16))
N_DIL = 3
DIL_HALF = 64
DIL_UNROLL = 4
A_HEADS = 4
A_HD = 128
A_WIDTH = A_HEADS * A_HD
B_HEADS = 8
Q_LORA = 512
KV_LORA = 512
D_NOPE = 128
D_ROPE = 64
D_V = 128
B_WIDTH = B_HEADS * D_V
ROPE_BASE = 10000.0
C_HEADS = 4
C_HD = 128
C_WIDTH = C_HEADS * C_HD
D_MIX = A_WIDTH + B_WIDTH + C_WIDTH
N_BUCKETS = 32
MAX_DISTANCE = 1024
EPS = 1e-6
NEG = -1e30

LANES = 128
VMEM_LIMIT_BYTES = 56 * 1024 * 1024

G_AQ = (0, 4, 16)
G_AK = 8
G_AV = 12
G_AG = 20
G_BCQ = 24
G_BCKV = 28
G_CQ = 32
G_CG = 36
G_BG = 40
G_KR = 48
N_GROUPS = 49
D_INP = N_GROUPS * LANES
QK_DIM = 2 * LANES

F32 = jnp.float32
BF16 = jnp.bfloat16
_NT = (((1,), (1,)), ((), ()))


def _tiles(S):
    return dict(
        tm_in=min(1024, S),
        tn_in=7 * LANES,
        tm_lat=min(512, S),
        tq_mla=min(512, S),
        tk_mla=min(512, S),
        tq_dil=128,
        tm_out=min(512, S),
    )


def _params(sem):
    return pltpu.CompilerParams(dimension_semantics=sem, vmem_limit_bytes=VMEM_LIMIT_BYTES)


def _silu(g):
    return g * (1.0 / (1.0 + jnp.exp(-g)))


def _inproj_kernel(x_ref, g_ref, w_ref, o_ref, h_ref, *, rows):
    @pl.when(pl.program_id(1) == 0)
    def _():
        tm = x_ref.shape[0]
        for r0 in range(0, tm, rows):
            x = x_ref[r0:r0 + rows, :]
            ms = jnp.mean(x * x, axis=-1, keepdims=True)
            h_ref[r0:r0 + rows, :] = (x * lax.rsqrt(ms + EPS) * g_ref[...]).astype(BF16)

    o_ref[...] = jnp.dot(h_ref[...], w_ref[...], preferred_element_type=F32).astype(o_ref.dtype)


def _inproj(x2d, gain, w_bf16, tm, tn):
    T = x2d.shape[0]
    return pl.pallas_call(
        functools.partial(_inproj_kernel, rows=min(256, tm)),
        grid=(T // tm, D_INP // tn),
        in_specs=[
            pl.BlockSpec((tm, D_MODEL), lambda i, j: (i, 0)),
            pl.BlockSpec((1, D_MODEL), lambda i, j: (0, 0)),
            pl.BlockSpec((D_MODEL, tn), lambda i, j: (0, j)),
        ],
        out_specs=pl.BlockSpec((tm, tn), lambda i, j: (i, j)),
        out_shape=jax.ShapeDtypeStruct((T, D_INP), BF16),
        scratch_shapes=[pltpu.VMEM((tm, D_MODEL), BF16)],
        compiler_params=_params(("parallel", "arbitrary")),
        name="inproj",
    )(x2d, gain, w_bf16)


def _latent_kernel(cq_ref, ckv_ref, kr_ref, qg_ref, kvg_ref, wq_ref, wkv_ref, cos_ref, sin_ref,
                   q_ref, k_ref, v_ref, *, q_scale):
    def rms(c_ref, gain_ref):
        c = c_ref[...].astype(F32)
        ms = jnp.mean(c * c, axis=-1, keepdims=True)
        return (c * lax.rsqrt(ms + EPS) * gain_ref[...]).astype(BF16)

    cos = cos_ref[...]
    sin = sin_ref[...]

    def rotary(g):
        return g * cos + pltpu.roll(g, D_ROPE, 1) * sin

    qf = jnp.dot(rms(cq_ref, qg_ref), wq_ref[...], preferred_element_type=F32)
    kvf = jnp.dot(rms(ckv_ref, kvg_ref), wkv_ref[...], preferred_element_type=F32)
    k_rot = rotary(kr_ref[...].astype(F32)).astype(BF16)
    for h in range(B_HEADS):
        c0 = h * QK_DIM
        q_ref[h, :, 0:LANES] = (qf[:, c0:c0 + LANES] * q_scale).astype(BF16)
        q_ref[h, :, LANES:QK_DIM] = (rotary(qf[:, c0 + LANES:c0 + QK_DIM]) * q_scale).astype(BF16)
        k_ref[h, :, 0:LANES] = kvf[:, c0:c0 + LANES].astype(BF16)
        k_ref[h, :, LANES:QK_DIM] = k_rot
        v_ref[h, :, :] = kvf[:, c0 + LANES:c0 + QK_DIM].astype(BF16)


def _latent(proj, q_gain, kv_gain, wq_ext, wkv, cos_tab, sin_tab, B, S, tm):
    q_scale = (D_NOPE + D_ROPE) ** -0.5 * math.log2(math.e)
    ns = S // tm
    col = lambda g: (lambda b, s: (b, s, g))
    row = lambda b, s: (0, 0)
    return pl.pallas_call(
        functools.partial(_latent_kernel, q_scale=q_scale),
        grid=(B, ns),
        in_specs=[
            pl.BlockSpec((None, tm, Q_LORA), col(G_BCQ * LANES // Q_LORA)),
            pl.BlockSpec((None, tm, KV_LORA), col(G_BCKV * LANES // KV_LORA)),
            pl.BlockSpec((None, tm, LANES), col(G_KR)),
            pl.BlockSpec((1, Q_LORA), row),
            pl.BlockSpec((1, KV_LORA), row),
            pl.BlockSpec((Q_LORA, B_HEADS * QK_DIM), row),
            pl.BlockSpec((KV_LORA, B_HEADS * QK_DIM), row),
            pl.BlockSpec((tm, LANES), lambda b, s: (s, 0)),
            pl.BlockSpec((tm, LANES), lambda b, s: (s, 0)),
        ],
        out_specs=[
            pl.BlockSpec((None, B_HEADS, tm, QK_DIM), lambda b, s: (b, 0, s, 0)),
            pl.BlockSpec((None, B_HEADS, tm, QK_DIM), lambda b, s: (b, 0, s, 0)),
            pl.BlockSpec((None, B_HEADS, tm, D_V), lambda b, s: (b, 0, s, 0)),
        ],
        out_shape=[
            jax.ShapeDtypeStruct((B, B_HEADS, S, QK_DIM), BF16),
            jax.ShapeDtypeStruct((B, B_HEADS, S, QK_DIM), BF16),
            jax.ShapeDtypeStruct((B, B_HEADS, S, D_V), BF16),
        ],
        compiler_params=_params(("parallel", "parallel")),
        name="latent_proj",
    )(proj, proj, proj, q_gain, kv_gain, wq_ext, wkv, cos_tab, sin_tab)


def _mla_kernel(q_ref, k_ref, v_ref, g_ref, o_ref, s_ref, mrun_ref, mfin_ref, *, nk, tk):
    tq = q_ref.shape[0]
    nt = tk // LANES

    @pl.when(pl.program_id(0) == 0)
    def _():
        s_ref[...] = jnp.zeros(s_ref.shape, F32)
        mfin_ref[...] = jnp.zeros(mfin_ref.shape, F32)

    q = q_ref[...]
    m_prev = mfin_ref[...]
    l_run = None
    acc = None
    for j in range(nk):
        ks = slice(j * tk, (j + 1) * tk)
        ps = []
        for t in range(nt):
            p = jnp.exp2(s_ref[:, j * tk + t * LANES:j * tk + (t + 1) * LANES] - m_prev)
            l_run = p if l_run is None else l_run + p
            ps.append(p.astype(BF16))
        pv = jnp.dot(jnp.concatenate(ps, axis=1), v_ref[ks, :], preferred_element_type=F32)
        acc = pv if acc is None else acc + pv
        s = lax.dot_general(q, k_ref[ks, :], _NT, preferred_element_type=F32)
        s_ref[:, ks] = s
        m_blk = s[:, 0:LANES]
        for t in range(1, nt):
            m_blk = jnp.maximum(m_blk, s[:, t * LANES:(t + 1) * LANES])
        mrun_ref[...] = m_blk if j == 0 else jnp.maximum(mrun_ref[...], m_blk)
    mfin_ref[...] = jnp.broadcast_to(jnp.max(mrun_ref[...], axis=1, keepdims=True), (tq, LANES))
    o = acc / jnp.sum(l_run, axis=1, keepdims=True)
    o_ref[...] = (o * _silu(g_ref[...].astype(F32))).astype(o_ref.dtype)


def _mla(q, k, v, proj, B, S, tq, tk):
    nq = S // tq
    n_tiles = B * B_HEADS * nq

    def tile(n):
        return n // (B_HEADS * nq), (n // nq) % B_HEADS, n % nq

    def cur(n):
        return tile(jnp.minimum(n, n_tiles - 1))

    def prev(n):
        return tile(jnp.maximum(n - 1, 0))

    def q_map(n):
        b, h, i = cur(n)
        return b, h, i, 0

    def k_map(n):
        b, h, _ = cur(n)
        return b, h, 0, 0

    def v_map(n):
        b, h, _ = prev(n)
        return b, h, 0, 0

    def g_map(n):
        b, h, i = prev(n)
        return b, i, G_BG + h

    def o_map(n):
        b, h, i = prev(n)
        return b, i, h

    return pl.pallas_call(
        functools.partial(_mla_kernel, nk=S // tk, tk=tk),
        grid=(n_tiles + 1,),
        in_specs=[
            pl.BlockSpec((None, None, tq, QK_DIM), q_map),
            pl.BlockSpec((None, None, S, QK_DIM), k_map),
            pl.BlockSpec((None, None, S, D_V), v_map),
            pl.BlockSpec((None, tq, LANES), g_map),
        ],
        out_specs=pl.BlockSpec((None, tq, D_V), o_map),
        out_shape=jax.ShapeDtypeStruct((B, S, B_WIDTH), BF16),
        scratch_shapes=[pltpu.VMEM((tq, S), F32), pltpu.VMEM((tq, LANES), F32), pltpu.VMEM((tq, LANES), F32)],
        compiler_params=_params(("arbitrary",)),
        name="latent_attn",
    )(q, k, v, proj)


def _t5_bucket(rel):
    nb = N_BUCKETS // 2
    max_exact = nb // 2
    bucket = jnp.where(rel > 0, nb, 0)
    n = jnp.abs(rel)
    nf = jnp.maximum(n, 1).astype(F32)
    large = max_exact + (jnp.log(nf / max_exact) / math.log(MAX_DISTANCE / max_exact)
                         * (nb - max_exact)).astype(jnp.int32)
    large = jnp.minimum(large, nb - 1)
    return bucket + jnp.where(n < max_exact, n, large)


def _dil_bias_tiles(rel_bias, g, dil, tq, tk):
    qi = jnp.arange(tq)[:, None]
    kj = jnp.arange(tk)[None, :]
    table = rel_bias.astype(F32)[:, g * A_HEADS:(g + 1) * A_HEADS]
    rel = jnp.stack([kj - qi - off for off in (0, DIL_HALF, 2 * DIL_HALF)])
    bucket = _t5_bucket(rel * dil)
    b = jnp.zeros((3, A_HEADS, tq, tk), F32)
    for i in range(N_BUCKETS):
        b = jnp.where((bucket == i)[:, None], table[i][None, :, None, None], b)
    return jnp.where((jnp.abs(rel) <= DIL_HALF)[:, None], b, NEG)


def _dil_kernel(q_ref, k_ref, v_ref, gate_ref, bias_ref, o_ref, q32_ref, k32_ref, v32_ref, acc_ref, lse_ref,
                *, S, tq, tk, unroll):
    g = pl.program_id(2)
    scale = A_HD ** -0.5

    def attend(q, k, v, bias):
        s = lax.dot_general(q, k, _NT, preferred_element_type=F32) * scale + bias
        m = jnp.max(s, axis=1, keepdims=True)
        p = jnp.exp(s - m)
        den = jnp.sum(p, axis=1, keepdims=True)
        o = jnp.dot(p.astype(BF16), v, preferred_element_type=F32) / den
        return o, jnp.broadcast_to(m + jnp.log(den), (tq, LANES))

    def window(n, nb, L):
        q0 = pl.multiple_of(n * tq, tq)
        ks = pl.multiple_of(jnp.clip(q0 - DIL_HALF, 0, L - tk), DIL_HALF)
        placement = jnp.where(n == 0, 0, jnp.where(n == nb - 1, 2, 1))
        return q0, ks, placement

    @pl.when(g == 0)
    def _():
        k32_ref[...] = k_ref[...].astype(F32)
        v32_ref[...] = v_ref[...].astype(F32)
        nb = S // tq

        def body(n, carry):
            q0, ks, placement = window(n, nb, S)
            o, lse = attend(q_ref[pl.ds(q0, tq), :], k_ref[pl.ds(ks, tk), :], v_ref[pl.ds(ks, tk), :],
                            bias_ref[placement])
            acc_ref[pl.ds(q0, tq), :] = o
            lse_ref[pl.ds(q0, tq), :] = lse
            return carry

        lax.fori_loop(0, nb, body, 0, unroll=min(unroll, nb))

    def strided_group(dil):
        L = S // dil
        nb = L // tq
        shift = nb.bit_length() - 1
        q32_ref[...] = q_ref[...].astype(F32)

        def body(i, carry):
            r = lax.shift_right_logical(i, shift)
            q0, ks, placement = window(jnp.bitwise_and(i, nb - 1), nb, L)
            q_rows = pl.ds(r + q0 * dil, tq, stride=dil)
            k_rows = pl.ds(r + ks * dil, tk, stride=dil)
            o, lse = attend(q32_ref[q_rows, :].astype(BF16), k32_ref[k_rows, :].astype(BF16),
                            v32_ref[k_rows, :].astype(BF16), bias_ref[placement])
            lse_old = lse_ref[q_rows, :]
            m2 = jnp.maximum(lse_old, lse)
            e_old = jnp.exp(lse_old - m2)
            e_new = jnp.exp(lse - m2)
            den = e_old + e_new
            acc_ref[q_rows, :] = (acc_ref[q_rows, :] * e_old + o * e_new) / den
            lse_ref[q_rows, :] = m2 + jnp.log(den)
            return carry

        lax.fori_loop(0, dil * nb, body, 0, unroll=min(unroll, dil * nb))

    for gi in range(1, N_DIL):
        pl.when(g == gi)(functools.partial(strided_group, DIL_PAIRS[gi][1]))

    @pl.when(g == N_DIL - 1)
    def _():
        o_ref[...] = (acc_ref[...] * _silu(gate_ref[...].astype(F32))).astype(o_ref.dtype)


def _dilated(proj, bias, B, S, tq):
    tk = tq + 2 * DIL_HALF
    for _, dil in DIL_PAIRS:
        nb = S // dil // tq
        assert nb >= 2 and nb & (nb - 1) == 0 and nb * tq * dil == S, (S, dil, tq)
    q_group = lambda g: G_AQ[1] * g + (G_AQ[2] - 2 * G_AQ[1]) * (g // 2)
    col = lambda g0: (lambda b, h, g: (b, 0, g0 + h))
    return pl.pallas_call(
        functools.partial(_dil_kernel, S=S, tq=tq, tk=tk, unroll=DIL_UNROLL),
        grid=(B, A_HEADS, N_DIL),
        in_specs=[
            pl.BlockSpec((None, S, LANES), lambda b, h, g: (b, 0, q_group(g) + h)),
            pl.BlockSpec((None, S, LANES), col(G_AK)),
            pl.BlockSpec((None, S, LANES), col(G_AV)),
            pl.BlockSpec((None, S, LANES), col(G_AG)),
            pl.BlockSpec((None, 3, None, tq, tk), lambda b, h, g: (g, 0, h, 0, 0)),
        ],
        out_specs=pl.BlockSpec((None, S, LANES), lambda b, h, g: (b, 0, h)),
        out_shape=jax.ShapeDtypeStruct((B, S, A_WIDTH), BF16),
        scratch_shapes=[pltpu.VMEM((S, LANES), F32) for _ in range(5)],
        compiler_params=_params(("parallel", "arbitrary", "arbitrary")),
        name="dilated_attn",
    )(proj, proj, proj, proj, bias)


def _memkv_kernel(mem_ref, g_ref, w_ref, o_ref):
    x = mem_ref[...]
    ms = jnp.mean(x * x, axis=-1, keepdims=True)
    hn = (x * lax.rsqrt(ms + EPS) * g_ref[...]).astype(BF16)
    o_ref[...] = jnp.dot(hn, w_ref[...], preferred_element_type=F32).astype(o_ref.dtype)


def _memkv(mem, gain, w_bf16):
    B, M, _ = mem.shape
    return pl.pallas_call(
        _memkv_kernel,
        grid=(B,),
        in_specs=[
            pl.BlockSpec((None, M, D_MODEL), lambda b: (b, 0, 0)),
            pl.BlockSpec((1, D_MODEL), lambda b: (0, 0)),
            pl.BlockSpec((D_MODEL, 2 * C_WIDTH), lambda b: (0, 0)),
        ],
        out_specs=pl.BlockSpec((None, M, 2 * C_WIDTH), lambda b: (b, 0, 0)),
        out_shape=jax.ShapeDtypeStruct((B, M, 2 * C_WIDTH), BF16),
        compiler_params=_params(("parallel",)),
        name="mem_kv_proj",
    )(mem, gain, w_bf16)


def _out_kernel(ya_ref, yb_ref, cq_ref, cg_ref, mkv_ref, w_ref, pg_ref, x_ref, o_ref, y_ref):
    n = pl.program_id(0)
    cur = lax.rem(n, 2)
    prv = 1 - cur

    @pl.when(n == 0)
    def _():
        y_ref[1] = jnp.zeros(y_ref.shape[1:], BF16)

    y = jnp.dot(y_ref[prv], w_ref[...], preferred_element_type=F32)
    ms = jnp.mean(y * y, axis=-1, keepdims=True)
    o_ref[...] = x_ref[...] + y * lax.rsqrt(ms + EPS) * pg_ref[...]

    y_ref[cur, :, 0:A_WIDTH] = ya_ref[...]
    y_ref[cur, :, A_WIDTH:A_WIDTH + B_WIDTH] = yb_ref[...]
    scale = C_HD ** -0.5
    for h in range(C_HEADS):
        cs = slice(h * C_HD, (h + 1) * C_HD)
        s = lax.dot_general(cq_ref[:, cs], mkv_ref[:, cs], _NT, preferred_element_type=F32) * scale
        m = jnp.max(s, axis=1, keepdims=True)
        p = jnp.exp(s - m)
        den = jnp.sum(p, axis=1, keepdims=True)
        mv = mkv_ref[:, C_WIDTH + h * C_HD:C_WIDTH + (h + 1) * C_HD]
        oc = jnp.dot(p.astype(BF16), mv, preferred_element_type=F32) / den
        c0 = A_WIDTH + B_WIDTH + h * C_HD
        y_ref[cur, :, c0:c0 + C_HD] = (oc * _silu(cg_ref[:, cs].astype(F32))).astype(BF16)


def _outproj(ya, yb, proj, mkv, w_out, post_gain, x, tm):
    B, S, _ = x.shape
    M = mkv.shape[1]
    ns = S // tm
    n_tiles = B * ns

    def cur(n):
        t = jnp.minimum(n, n_tiles - 1)
        return t // ns, t % ns

    def prev(n):
        t = jnp.maximum(n - 1, 0)
        return t // ns, t % ns

    def tok(width, blk, which):
        return pl.BlockSpec((None, tm, width), lambda n: (*which(n), blk))

    const = lambda shape: pl.BlockSpec(shape, lambda n: (0,) * len(shape))
    return pl.pallas_call(
        _out_kernel,
        grid=(n_tiles + 1,),
        in_specs=[
            tok(A_WIDTH, 0, cur),
            tok(B_WIDTH, 0, cur),
            tok(C_WIDTH, G_CQ * LANES // C_WIDTH, cur),
            tok(C_WIDTH, G_CG * LANES // C_WIDTH, cur),
            pl.BlockSpec((None, M, 2 * C_WIDTH), lambda n: (cur(n)[0], 0, 0)),
            const((D_MIX, D_MODEL)),
            const((1, D_MODEL)),
            tok(D_MODEL, 0, prev),
        ],
        out_specs=tok(D_MODEL, 0, prev),
        out_shape=jax.ShapeDtypeStruct((B, S, D_MODEL), x.dtype),
        scratch_shapes=[pltpu.VMEM((2, tm, D_MIX), BF16)],
        compiler_params=_params(("arbitrary",)),
        name="outproj",
    )(ya, yb, proj, proj, mkv, w_out, post_gain, x)


def _swap_halves(w):
    half = D_ROPE // 2
    return jnp.concatenate([w[..., half:], w[..., :half]], axis=-1)


def _prep_weights(w_in, w_uq, w_ukv, w_mem_kv, w_out):
    o_kr = N_DIL * A_WIDTH + 3 * A_WIDTH + Q_LORA + KV_LORA
    o_bg = o_kr + D_ROPE
    o_cq = o_bg + B_WIDTH
    kr = w_in[:, o_kr:o_bg]
    aq = [w_in[:, g * A_WIDTH:(g + 1) * A_WIDTH] for g in range(N_DIL)]
    o_ak = N_DIL * A_WIDTH
    w_in_r = jnp.concatenate(
        [aq[0], aq[1], w_in[:, o_ak:o_ak + 2 * A_WIDTH], aq[2], w_in[:, o_ak + 2 * A_WIDTH:o_kr],
         w_in[:, o_cq:], w_in[:, o_bg:o_cq], kr, _swap_halves(kr)], axis=1).astype(BF16)
    wq = w_uq.reshape(Q_LORA, B_HEADS, D_NOPE + D_ROPE)
    rope = wq[:, :, D_NOPE:]
    wq_ext = jnp.concatenate([wq[:, :, :D_NOPE], rope, _swap_halves(rope)], axis=-1)
    wq_ext = wq_ext.reshape(Q_LORA, B_HEADS * QK_DIM).astype(BF16)
    return w_in_r, wq_ext, w_ukv.astype(BF16), w_mem_kv.astype(BF16), w_out.astype(BF16)


def _rope_tables(S):
    inv = ROPE_BASE ** (-jnp.arange(0, D_ROPE, 2, dtype=F32) / D_ROPE)
    ang = jnp.arange(S, dtype=F32)[:, None] * inv[None, :]
    cos, sin = jnp.cos(ang), jnp.sin(ang)
    zeros = jnp.zeros((S, LANES - D_ROPE), F32)
    return (jnp.concatenate([cos, cos, zeros], axis=1),
            jnp.concatenate([-sin, sin, zeros], axis=1))


def _dil_biases(rel_bias, tq):
    return jnp.stack([_dil_bias_tiles(rel_bias, g, dil, tq, tq + 2 * DIL_HALF)
                      for g, (_, dil) in enumerate(DIL_PAIRS)])


def _layer(x, mem, pre_gain, q_gain, kv_gain, mem_gain, post_gain, biases, weights):
    w_in_r, wq_ext, wkv, wmem, wout = weights
    B, S, _ = x.shape
    t = _tiles(S)
    proj = _inproj(x.reshape(B * S, D_MODEL), pre_gain, w_in_r, t["tm_in"], t["tn_in"])
    proj = proj.reshape(B, S, D_INP)
    cos_tab, sin_tab = _rope_tables(S)
    q, k, v = _latent(proj, q_gain, kv_gain, wq_ext, wkv, cos_tab, sin_tab, B, S, t["tm_lat"])
    yb = _mla(q, k, v, proj, B, S, t["tq_mla"], t["tk_mla"])
    ya = _dilated(proj, biases, B, S, t["tq_dil"])
    mkv = _memkv(mem, mem_gain, wmem)
    return _outproj(ya, yb, proj, mkv, wout, post_gain, x, t["tm_out"])


def kernel(x_prompt, x_sample, mem_prompt, mem_sample, pre_gain, w_in, q_gain, w_uq, kv_gain, w_ukv,
           mem_gain, w_mem_kv, w_out, post_gain, rel_bias):
    depth = pre_gain.shape[0]
    weights = [_prep_weights(w_in[l], w_uq[l], w_ukv[l], w_mem_kv[l], w_out[l]) for l in range(depth)]
    biases = _dil_biases(rel_bias, _tiles(x_prompt.shape[1])["tq_dil"])

    def trunk(x, mem):
        for l in range(depth):
            x = _layer(x, mem, pre_gain[l][None], q_gain[l][None], kv_gain[l][None], mem_gain[l][None],
                       post_gain[l][None], biases, weights[l])
        return x

    return trunk(x_prompt, mem_prompt), trunk(x_sample, mem_sample)
```

```python
import functools
import math

import jax
import jax.numpy as jnp
import numpy as np
from jax import lax
from jax.experimental import pallas as pl
from jax.experimental.pallas import tpu as pltpu

D_MODEL = 2048
DIL_PAIRS = ((128, 1), (512, 4), (2048, 16))
N_DIL = 3
DIL_HALF = 64
DIL_UNROLL = 8
A_HEADS = 4
A_HD = 128
A_WIDTH = A_HEADS * A_HD
B_HEADS = 8
Q_LORA = 512
KV_LORA = 512
D_NOPE = 128
D_ROPE = 64
D_V = 128
B_WIDTH = B_HEADS * D_V
ROPE_BASE = 10000.0
C_HEADS = 4
C_HD = 128
C_WIDTH = C_HEADS * C_HD
D_MIX = A_WIDTH + B_WIDTH + C_WIDTH
N_BUCKETS = 32
MAX_DISTANCE = 1024
EPS = 1e-6
NEG = -1e30

LANES = 128
VMEM_LIMIT_BYTES = 56 * 1024 * 1024

G_AQ = (0, 4, 16)
G_AK = 8
G_AV = 12
G_AG = 20
G_BCQ = 24
G_BCKV = 28
G_CQ = 32
G_CG = 36
G_BG = 40
G_KR = 48
N_GROUPS = 50
MXU_COLS = 2 * LANES
D_INP = N_GROUPS * LANES
QK_DIM = 2 * LANES

F32 = jnp.float32
BF16 = jnp.bfloat16
_NT = (((1,), (1,)), ((), ()))


def _tiles(S):
    return dict(
        tm_in=min(1024, S),
        tn_in=5 * MXU_COLS,
        tm_lat=min(512, S),
        tq_mla=min(512, S),
        tk_mla=min(512, S),
        tq_dil=128,
        tm_out=min(512, S),
    )


def _params(sem):
    return pltpu.CompilerParams(dimension_semantics=sem, vmem_limit_bytes=VMEM_LIMIT_BYTES)


def _silu(g):
    return g * (1.0 / (1.0 + jnp.exp(-g)))


def _inproj_kernel(x_ref, g_ref, w_ref, o_ref, h_ref, *, rows):
    @pl.when(pl.program_id(1) == 0)
    def _():
        tm = x_ref.shape[0]
        for r0 in range(0, tm, rows):
            x = x_ref[r0:r0 + rows, :]
            ms = jnp.mean(x * x, axis=-1, keepdims=True)
            h_ref[r0:r0 + rows, :] = (x * lax.rsqrt(ms + EPS) * g_ref[...]).astype(BF16)

    o_ref[...] = jnp.dot(h_ref[...], w_ref[...], preferred_element_type=F32).astype(o_ref.dtype)


def _inproj(x2d, gain, w_bf16, tm, tn):
    T = x2d.shape[0]
    return pl.pallas_call(
        functools.partial(_inproj_kernel, rows=min(256, tm)),
        grid=(T // tm, D_INP // tn),
        in_specs=[
            pl.BlockSpec((tm, D_MODEL), lambda i, j: (i, 0)),
            pl.BlockSpec((1, D_MODEL), lambda i, j: (0, 0)),
            pl.BlockSpec((D_MODEL, tn), lambda i, j: (0, j)),
        ],
        out_specs=pl.BlockSpec((tm, tn), lambda i, j: (i, j)),
        out_shape=jax.ShapeDtypeStruct((T, D_INP), BF16),
        scratch_shapes=[pltpu.VMEM((tm, D_MODEL), BF16)],
        compiler_params=_params(("parallel", "arbitrary")),
        name="inproj",
    )(x2d, gain, w_bf16)


def _latent_kernel(cq_ref, ckv_ref, kr_ref, qg_ref, kvg_ref, wq_ref, wkv_ref, cos_ref, sin_ref,
                   q_ref, k_ref, v_ref, *, q_scale):
    def rms(c_ref, gain_ref):
        c = c_ref[...].astype(F32)
        ms = jnp.mean(c * c, axis=-1, keepdims=True)
        return (c * lax.rsqrt(ms + EPS) * gain_ref[...]).astype(BF16)

    cos = cos_ref[...]
    sin = sin_ref[...]

    def rotary(g):
        return g * cos + pltpu.roll(g, D_ROPE, 1) * sin

    qf = jnp.dot(rms(cq_ref, qg_ref), wq_ref[...], preferred_element_type=F32)
    kvf = jnp.dot(rms(ckv_ref, kvg_ref), wkv_ref[...], preferred_element_type=F32)
    k_rot = rotary(kr_ref[...].astype(F32)).astype(BF16)
    for h in range(B_HEADS):
        c0 = h * QK_DIM
        q_ref[h, :, 0:LANES] = (qf[:, c0:c0 + LANES] * q_scale).astype(BF16)
        q_ref[h, :, LANES:QK_DIM] = (rotary(qf[:, c0 + LANES:c0 + QK_DIM]) * q_scale).astype(BF16)
        k_ref[h, :, 0:LANES] = kvf[:, c0:c0 + LANES].astype(BF16)
        k_ref[h, :, LANES:QK_DIM] = k_rot
        v_ref[h, :, :] = kvf[:, c0 + LANES:c0 + QK_DIM].astype(BF16)


def _latent(proj, q_gain, kv_gain, wq_ext, wkv, cos_tab, sin_tab, B, S, tm):
    q_scale = (D_NOPE + D_ROPE) ** -0.5 * math.log2(math.e)
    ns = S // tm
    col = lambda g: (lambda b, s: (b, s, g))
    row = lambda b, s: (0, 0)
    return pl.pallas_call(
        functools.partial(_latent_kernel, q_scale=q_scale),
        grid=(B, ns),
        in_specs=[
            pl.BlockSpec((None, tm, Q_LORA), col(G_BCQ * LANES // Q_LORA)),
            pl.BlockSpec((None, tm, KV_LORA), col(G_BCKV * LANES // KV_LORA)),
            pl.BlockSpec((None, tm, LANES), col(G_KR)),
            pl.BlockSpec((1, Q_LORA), row),
            pl.BlockSpec((1, KV_LORA), row),
            pl.BlockSpec((Q_LORA, B_HEADS * QK_DIM), row),
            pl.BlockSpec((KV_LORA, B_HEADS * QK_DIM), row),
            pl.BlockSpec((tm, LANES), lambda b, s: (s, 0)),
            pl.BlockSpec((tm, LANES), lambda b, s: (s, 0)),
        ],
        out_specs=[
            pl.BlockSpec((None, B_HEADS, tm, QK_DIM), lambda b, s: (b, 0, s, 0)),
            pl.BlockSpec((None, B_HEADS, tm, QK_DIM), lambda b, s: (b, 0, s, 0)),
            pl.BlockSpec((None, B_HEADS, tm, D_V), lambda b, s: (b, 0, s, 0)),
        ],
        out_shape=[
            jax.ShapeDtypeStruct((B, B_HEADS, S, QK_DIM), BF16),
            jax.ShapeDtypeStruct((B, B_HEADS, S, QK_DIM), BF16),
            jax.ShapeDtypeStruct((B, B_HEADS, S, D_V), BF16),
        ],
        compiler_params=_params(("parallel", "parallel")),
        name="latent_proj",
    )(proj, proj, proj, q_gain, kv_gain, wq_ext, wkv, cos_tab, sin_tab)


def _mla_kernel(q_ref, k_ref, v_ref, g_ref, o_ref, s_ref, mrun_ref, mfin_ref, *, nk, tk):
    tq = q_ref.shape[0]
    nt = tk // LANES

    @pl.when(pl.program_id(0) == 0)
    def _():
        s_ref[...] = jnp.zeros(s_ref.shape, F32)
        mfin_ref[...] = jnp.zeros(mfin_ref.shape, F32)

    q = q_ref[...]
    m_prev = mfin_ref[...]
    l_run = None
    acc = None
    for j in range(nk):
        ks = slice(j * tk, (j + 1) * tk)
        ps = []
        for t in range(nt):
            p = jnp.exp2(s_ref[:, j * tk + t * LANES:j * tk + (t + 1) * LANES] - m_prev)
            l_run = p if l_run is None else l_run + p
            ps.append(p.astype(BF16))
        pv = jnp.dot(jnp.concatenate(ps, axis=1), v_ref[ks, :], preferred_element_type=F32)
        acc = pv if acc is None else acc + pv
        s = lax.dot_general(q, k_ref[ks, :], _NT, preferred_element_type=F32)
        s_ref[:, ks] = s
        m_blk = s[:, 0:LANES]
        for t in range(1, nt):
            m_blk = jnp.maximum(m_blk, s[:, t * LANES:(t + 1) * LANES])
        mrun_ref[...] = m_blk if j == 0 else jnp.maximum(mrun_ref[...], m_blk)
    mfin_ref[...] = jnp.broadcast_to(jnp.max(mrun_ref[...], axis=1, keepdims=True), (tq, LANES))
    o = acc / jnp.sum(l_run, axis=1, keepdims=True)
    o_ref[...] = (o * _silu(g_ref[...].astype(F32))).astype(o_ref.dtype)


def _mla(q, k, v, proj, B, S, tq, tk):
    nq = S // tq
    n_tiles = B * B_HEADS * nq

    def tile(n):
        return n // (B_HEADS * nq), (n // nq) % B_HEADS, n % nq

    def cur(n):
        return tile(jnp.minimum(n, n_tiles - 1))

    def prev(n):
        return tile(jnp.maximum(n - 1, 0))

    def q_map(n):
        b, h, i = cur(n)
        return b, h, i, 0

    def k_map(n):
        b, h, _ = cur(n)
        return b, h, 0, 0

    def v_map(n):
        b, h, _ = prev(n)
        return b, h, 0, 0

    def g_map(n):
        b, h, i = prev(n)
        return b, i, G_BG + h

    def o_map(n):
        b, h, i = prev(n)
        return b, i, h

    return pl.pallas_call(
        functools.partial(_mla_kernel, nk=S // tk, tk=tk),
        grid=(n_tiles + 1,),
        in_specs=[
            pl.BlockSpec((None, None, tq, QK_DIM), q_map),
            pl.BlockSpec((None, None, S, QK_DIM), k_map),
            pl.BlockSpec((None, None, S, D_V), v_map),
            pl.BlockSpec((None, tq, LANES), g_map),
        ],
        out_specs=pl.BlockSpec((None, tq, D_V), o_map),
        out_shape=jax.ShapeDtypeStruct((B, S, B_WIDTH), BF16),
        scratch_shapes=[pltpu.VMEM((tq, S), F32), pltpu.VMEM((tq, LANES), F32), pltpu.VMEM((tq, LANES), F32)],
        compiler_params=_params(("arbitrary",)),
        name="latent_attn",
    )(q, k, v, proj)


def _t5_bucket(rel):
    nb = N_BUCKETS // 2
    max_exact = nb // 2
    bucket = jnp.where(rel > 0, nb, 0)
    n = jnp.abs(rel)
    nf = jnp.maximum(n, 1).astype(F32)
    large = max_exact + (jnp.log(nf / max_exact) / math.log(MAX_DISTANCE / max_exact)
                         * (nb - max_exact)).astype(jnp.int32)
    large = jnp.minimum(large, nb - 1)
    return bucket + jnp.where(n < max_exact, n, large)


def _dil_bias_tiles(rel_bias, g, dil, tq, tk):
    qi = jnp.arange(tq)[:, None]
    kj = jnp.arange(tk)[None, :]
    table = rel_bias.astype(F32)[:, g * A_HEADS:(g + 1) * A_HEADS]
    rel = jnp.stack([kj - qi - off for off in (0, DIL_HALF, 2 * DIL_HALF)])
    bucket = _t5_bucket(rel * dil)
    b = jnp.zeros((3, A_HEADS, tq, tk), F32)
    for i in range(N_BUCKETS):
        b = jnp.where((bucket == i)[:, None], table[i][None, :, None, None], b)
    return jnp.where((jnp.abs(rel) <= DIL_HALF)[:, None], b, NEG)


def _dil_kernel(q_ref, k_ref, v_ref, gate_ref, bias_ref, o_ref, q32_ref, k32_ref, v32_ref, acc_ref, lse_ref,
                *, S, tq, tk, unroll):
    g = pl.program_id(2)
    scale = A_HD ** -0.5

    def attend(q, k, v, bias):
        s = lax.dot_general(q, k, _NT, preferred_element_type=F32) * scale + bias
        m = jnp.max(s, axis=1, keepdims=True)
        p = jnp.exp(s - m)
        den = jnp.sum(p, axis=1, keepdims=True)
        o = jnp.dot(p.astype(BF16), v, preferred_element_type=F32) / den
        return o, jnp.broadcast_to(m + jnp.log(den), (tq, LANES))

    def window(n, nb, L):
        q0 = pl.multiple_of(n * tq, tq)
        ks = pl.multiple_of(jnp.clip(q0 - DIL_HALF, 0, L - tk), DIL_HALF)
        placement = jnp.where(n == 0, 0, jnp.where(n == nb - 1, 2, 1))
        return q0, ks, placement

    @pl.when(g == 0)
    def _():
        k32_ref[...] = k_ref[...].astype(F32)
        v32_ref[...] = v_ref[...].astype(F32)
        nb = S // tq

        def body(n, carry):
            q0, ks, placement = window(n, nb, S)
            o, lse = attend(q_ref[pl.ds(q0, tq), :], k_ref[pl.ds(ks, tk), :], v_ref[pl.ds(ks, tk), :],
                            bias_ref[placement])
            acc_ref[pl.ds(q0, tq), :] = o
            lse_ref[pl.ds(q0, tq), :] = lse
            return carry

        lax.fori_loop(0, nb, body, 0, unroll=min(unroll, nb))

    def strided_group(dil):
        L = S // dil
        nb = L // tq
        nc = L // DIL_HALF
        wch = tk // DIL_HALF
        qch = tq // DIL_HALF
        q32_ref[...] = q_ref[...].astype(F32)

        def stream(r, carry):
            def chunks(ref):
                return [ref[pl.ds(r + c * DIL_HALF * dil, DIL_HALF, stride=dil), :].astype(BF16)
                        for c in range(nc)]

            kc, vc = chunks(k32_ref), chunks(v32_ref)
            for n in range(nb):
                c0 = min(max(n * qch - 1, 0), nc - wch)
                placement = 0 if n == 0 else (2 if n == nb - 1 else 1)
                q_rows = pl.ds(r + n * tq * dil, tq, stride=dil)
                o, lse = attend(q32_ref[q_rows, :].astype(BF16), jnp.concatenate(kc[c0:c0 + wch], axis=0),
                                jnp.concatenate(vc[c0:c0 + wch], axis=0), bias_ref[placement])
                lse_old = lse_ref[q_rows, :]
                m2 = jnp.maximum(lse_old, lse)
                e_old = jnp.exp(lse_old - m2)
                e_new = jnp.exp(lse - m2)
                den = e_old + e_new
                acc_ref[q_rows, :] = (acc_ref[q_rows, :] * e_old + o * e_new) / den
                lse_ref[q_rows, :] = m2 + jnp.log(den)
            return carry

        lax.fori_loop(0, dil, stream, 0, unroll=max(1, min(unroll // nb, dil)))

    for gi in range(1, N_DIL):
        pl.when(g == gi)(functools.partial(strided_group, DIL_PAIRS[gi][1]))

    @pl.when(g == N_DIL - 1)
    def _():
        o_ref[...] = (acc_ref[...] * _silu(gate_ref[...].astype(F32))).astype(o_ref.dtype)


def _dilated(proj, bias, B, S, tq):
    tk = tq + 2 * DIL_HALF
    for _, dil in DIL_PAIRS:
        nb = S // dil // tq
        assert nb >= 2 and nb * tq * dil == S, (S, dil, tq)
    q_group = lambda g: G_AQ[1] * g + (G_AQ[2] - 2 * G_AQ[1]) * (g // 2)
    col = lambda g0: (lambda b, h, g: (b, 0, g0 + h))
    return pl.pallas_call(
        functools.partial(_dil_kernel, S=S, tq=tq, tk=tk, unroll=DIL_UNROLL),
        grid=(B, A_HEADS, N_DIL),
        in_specs=[
            pl.BlockSpec((None, S, LANES), lambda b, h, g: (b, 0, q_group(g) + h)),
            pl.BlockSpec((None, S, LANES), col(G_AK)),
            pl.BlockSpec((None, S, LANES), col(G_AV)),
            pl.BlockSpec((None, S, LANES), col(G_AG)),
            pl.BlockSpec((None, 3, None, tq, tk), lambda b, h, g: (g, 0, h, 0, 0)),
        ],
        out_specs=pl.BlockSpec((None, S, LANES), lambda b, h, g: (b, 0, h)),
        out_shape=jax.ShapeDtypeStruct((B, S, A_WIDTH), BF16),
        scratch_shapes=[pltpu.VMEM((S, LANES), F32) for _ in range(5)],
        compiler_params=_params(("parallel", "arbitrary", "arbitrary")),
        name="dilated_attn",
    )(proj, proj, proj, proj, bias)


def _memkv_kernel(mem_ref, g_ref, w_ref, o_ref):
    x = mem_ref[...]
    ms = jnp.mean(x * x, axis=-1, keepdims=True)
    hn = (x * lax.rsqrt(ms + EPS) * g_ref[...]).astype(BF16)
    o_ref[...] = jnp.dot(hn, w_ref[...], preferred_element_type=F32).astype(o_ref.dtype)


def _memkv(mem, gain, w_bf16):
    B, M, _ = mem.shape
    return pl.pallas_call(
        _memkv_kernel,
        grid=(B,),
        in_specs=[
            pl.BlockSpec((None, M, D_MODEL), lambda b: (b, 0, 0)),
            pl.BlockSpec((1, D_MODEL), lambda b: (0, 0)),
            pl.BlockSpec((D_MODEL, 2 * C_WIDTH), lambda b: (0, 0)),
        ],
        out_specs=pl.BlockSpec((None, M, 2 * C_WIDTH), lambda b: (b, 0, 0)),
        out_shape=jax.ShapeDtypeStruct((B, M, 2 * C_WIDTH), BF16),
        compiler_params=_params(("parallel",)),
        name="mem_kv_proj",
    )(mem, gain, w_bf16)


def _out_kernel(ya_ref, yb_ref, cq_ref, cg_ref, mkv_ref, w_ref, pg_ref, x_ref, o_ref, y_ref):
    n = pl.program_id(0)
    cur = lax.rem(n, 2)
    prv = 1 - cur

    @pl.when(n == 0)
    def _():
        y_ref[1] = jnp.zeros(y_ref.shape[1:], BF16)

    y = jnp.dot(y_ref[prv], w_ref[...], preferred_element_type=F32)
    ms = jnp.mean(y * y, axis=-1, keepdims=True)
    o_ref[...] = x_ref[...] + y * lax.rsqrt(ms + EPS) * pg_ref[...]

    y_ref[cur, :, 0:A_WIDTH] = ya_ref[...]
    y_ref[cur, :, A_WIDTH:A_WIDTH + B_WIDTH] = yb_ref[...]
    scale = C_HD ** -0.5
    for h in range(C_HEADS):
        cs = slice(h * C_HD, (h + 1) * C_HD)
        s = lax.dot_general(cq_ref[:, cs], mkv_ref[:, cs], _NT, preferred_element_type=F32) * scale
        m = jnp.max(s, axis=1, keepdims=True)
        p = jnp.exp(s - m)
        den = jnp.sum(p, axis=1, keepdims=True)
        mv = mkv_ref[:, C_WIDTH + h * C_HD:C_WIDTH + (h + 1) * C_HD]
        oc = jnp.dot(p.astype(BF16), mv, preferred_element_type=F32) / den
        c0 = A_WIDTH + B_WIDTH + h * C_HD
        y_ref[cur, :, c0:c0 + C_HD] = (oc * _silu(cg_ref[:, cs].astype(F32))).astype(BF16)


def _outproj(ya, yb, proj, mkv, w_out, post_gain, x, tm):
    B, S, _ = x.shape
    M = mkv.shape[1]
    ns = S // tm
    n_tiles = B * ns

    def cur(n):
        t = jnp.minimum(n, n_tiles - 1)
        return t // ns, t % ns

    def prev(n):
        t = jnp.maximum(n - 1, 0)
        return t // ns, t % ns

    def tok(width, blk, which):
        return pl.BlockSpec((None, tm, width), lambda n: (*which(n), blk))

    const = lambda shape: pl.BlockSpec(shape, lambda n: (0,) * len(shape))
    return pl.pallas_call(
        _out_kernel,
        grid=(n_tiles + 1,),
        in_specs=[
            tok(A_WIDTH, 0, cur),
            tok(B_WIDTH, 0, cur),
            tok(C_WIDTH, G_CQ * LANES // C_WIDTH, cur),
            tok(C_WIDTH, G_CG * LANES // C_WIDTH, cur),
            pl.BlockSpec((None, M, 2 * C_WIDTH), lambda n: (cur(n)[0], 0, 0)),
            const((D_MIX, D_MODEL)),
            const((1, D_MODEL)),
            tok(D_MODEL, 0, prev),
        ],
        out_specs=tok(D_MODEL, 0, prev),
        out_shape=jax.ShapeDtypeStruct((B, S, D_MODEL), x.dtype),
        scratch_shapes=[pltpu.VMEM((2, tm, D_MIX), BF16)],
        compiler_params=_params(("arbitrary",)),
        name="outproj",
    )(ya, yb, proj, proj, mkv, w_out, post_gain, x)


def _swap_halves(w):
    half = D_ROPE // 2
    return jnp.concatenate([w[..., half:], w[..., :half]], axis=-1)


def _prep_weights(w_in, w_uq, w_ukv, w_mem_kv, w_out):
    o_kr = N_DIL * A_WIDTH + 3 * A_WIDTH + Q_LORA + KV_LORA
    o_bg = o_kr + D_ROPE
    o_cq = o_bg + B_WIDTH
    kr = w_in[:, o_kr:o_bg]
    aq = [w_in[:, g * A_WIDTH:(g + 1) * A_WIDTH] for g in range(N_DIL)]
    o_ak = N_DIL * A_WIDTH
    w_in_r = jnp.concatenate(
        [aq[0], aq[1], w_in[:, o_ak:o_ak + 2 * A_WIDTH], aq[2], w_in[:, o_ak + 2 * A_WIDTH:o_kr],
         w_in[:, o_cq:], w_in[:, o_bg:o_cq], kr, _swap_halves(kr),
         jnp.zeros((D_MODEL, D_INP - (G_KR + 1) * LANES), w_in.dtype)], axis=1).astype(BF16)
    wq = w_uq.reshape(Q_LORA, B_HEADS, D_NOPE + D_ROPE)
    rope = wq[:, :, D_NOPE:]
    wq_ext = jnp.concatenate([wq[:, :, :D_NOPE], rope, _swap_halves(rope)], axis=-1)
    wq_ext = wq_ext.reshape(Q_LORA, B_HEADS * QK_DIM).astype(BF16)
    return w_in_r, wq_ext, w_ukv.astype(BF16), w_mem_kv.astype(BF16), w_out.astype(BF16)


def _rope_tables(S):
    inv = ROPE_BASE ** (-jnp.arange(0, D_ROPE, 2, dtype=F32) / D_ROPE)
    ang = jnp.arange(S, dtype=F32)[:, None] * inv[None, :]
    cos, sin = jnp.cos(ang), jnp.sin(ang)
    zeros = jnp.zeros((S, LANES - D_ROPE), F32)
    return (jnp.concatenate([cos, cos, zeros], axis=1),
            jnp.concatenate([-sin, sin, zeros], axis=1))


def _dil_biases(rel_bias, tq):
    return jnp.stack([_dil_bias_tiles(rel_bias, g, dil, tq, tq + 2 * DIL_HALF)
                      for g, (_, dil) in enumerate(DIL_PAIRS)])


def _layer(x, mem, pre_gain, q_gain, kv_gain, mem_gain, post_gain, biases, weights):
    w_in_r, wq_ext, wkv, wmem, wout = weights
    B, S, _ = x.shape
    t = _tiles(S)
    proj = _inproj(x.reshape(B * S, D_MODEL), pre_gain, w_in_r, t["tm_in"], t["tn_in"])
    proj = proj.reshape(B, S, D_INP)
    cos_tab, sin_tab = _rope_tables(S)
    q, k, v = _latent(proj, q_gain, kv_gain, wq_ext, wkv, cos_tab, sin_tab, B, S, t["tm_lat"])
    yb = _mla(q, k, v, proj, B, S, t["tq_mla"], t["tk_mla"])
    ya = _dilated(proj, biases, B, S, t["tq_dil"])
    mkv = _memkv(mem, mem_gain, wmem)
    return _outproj(ya, yb, proj, mkv, wout, post_gain, x, t["tm_out"])


def kernel(x_prompt, x_sample, mem_prompt, mem_sample, pre_gain, w_in, q_gain, w_uq, kv_gain, w_ukv,
           mem_gain, w_mem_kv, w_out, post_gain, rel_bias):
    depth = pre_gain.shape[0]
    weights = [_prep_weights(w_in[l], w_uq[l], w_ukv[l], w_mem_kv[l], w_out[l]) for l in range(depth)]
    biases = _dil_biases(rel_bias, _tiles(x_prompt.shape[1])["tq_dil"])

    def trunk(x, mem):
        for l in range(depth):
            x = _layer(x, mem, pre_gain[l][None], q_gain[l][None], kv_gain[l][None], mem_gain[l][None],
                       post_gain[l][None], biases, weights[l])
        return x

    return trunk(x_prompt, mem_prompt), trunk(x_sample, mem_sample)
```

```python
import functools
import math

import jax
import jax.numpy as jnp
import numpy as np
from jax import lax
from jax.experimental import pallas as pl
from jax.experimental.pallas import tpu as pltpu

D_MODEL = 2048
DIL_PAIRS = ((128, 1), (512, 4), (2048, 16))
N_DIL = 3
DIL_HALF = 64
DIL_UNROLL = 8
A_HEADS = 4
A_HD = 128
A_WIDTH = A_HEADS * A_HD
B_HEADS = 8
Q_LORA = 512
KV_LORA = 512
D_NOPE = 128
D_ROPE = 64
D_V = 128
B_WIDTH = B_HEADS * D_V
ROPE_BASE = 10000.0
C_HEADS = 4
C_HD = 128
C_WIDTH = C_HEADS * C_HD
D_MIX = A_WIDTH + B_WIDTH + C_WIDTH
N_BUCKETS = 32
MAX_DISTANCE = 1024
EPS = 1e-6
NEG = -1e30

LANES = 128
VMEM_LIMIT_BYTES = 56 * 1024 * 1024

G_AQ = (0, 4, 16)
G_AK = 8
G_AV = 12
G_AG = 20
G_BCQ = 24
G_BCKV = 28
G_CQ = 32
G_CG = 36
G_BG = 40
G_KR = 48
N_GROUPS = 50
MXU_COLS = 2 * LANES
D_INP = N_GROUPS * LANES
QK_DIM = 2 * LANES

F32 = jnp.float32
BF16 = jnp.bfloat16
_NT = (((1,), (1,)), ((), ()))


def _tiles(S):
    return dict(
        tm_in=min(1024, S),
        tn_in=5 * MXU_COLS,
        tm_lat=min(512, S),
        tq_mla=min(512, S),
        tk_mla=min(512, S),
        tq_dil=128,
        tm_out=min(512, S),
    )


def _params(sem):
    return pltpu.CompilerParams(dimension_semantics=sem, vmem_limit_bytes=VMEM_LIMIT_BYTES)


def _silu(g):
    return g * (1.0 / (1.0 + jnp.exp(-g)))


def _inproj_kernel(x_ref, g_ref, w_ref, o_ref, h_ref, *, rows):
    @pl.when(pl.program_id(1) == 0)
    def _():
        tm = x_ref.shape[0]
        for r0 in range(0, tm, rows):
            x = x_ref[r0:r0 + rows, :]
            ms = jnp.mean(x * x, axis=-1, keepdims=True)
            h_ref[r0:r0 + rows, :] = (x * lax.rsqrt(ms + EPS) * g_ref[...]).astype(BF16)

    o_ref[...] = jnp.dot(h_ref[...], w_ref[...], preferred_element_type=F32).astype(o_ref.dtype)


def _inproj(x2d, gain, w_bf16, tm, tn):
    T = x2d.shape[0]
    return pl.pallas_call(
        functools.partial(_inproj_kernel, rows=min(256, tm)),
        grid=(T // tm, D_INP // tn),
        in_specs=[
            pl.BlockSpec((tm, D_MODEL), lambda i, j: (i, 0)),
            pl.BlockSpec((1, D_MODEL), lambda i, j: (0, 0)),
            pl.BlockSpec((D_MODEL, tn), lambda i, j: (0, j)),
        ],
        out_specs=pl.BlockSpec((tm, tn), lambda i, j: (i, j)),
        out_shape=jax.ShapeDtypeStruct((T, D_INP), BF16),
        scratch_shapes=[pltpu.VMEM((tm, D_MODEL), BF16)],
        compiler_params=_params(("parallel", "arbitrary")),
        name="inproj",
    )(x2d, gain, w_bf16)


def _latent_kernel(cq_ref, ckv_ref, kr_ref, qg_ref, kvg_ref, wq_ref, wkv_ref, cos_ref, sin_ref,
                   q_ref, k_ref, v_ref, *, q_scale):
    def rms(c_ref, gain_ref):
        c = c_ref[...].astype(F32)
        ms = jnp.mean(c * c, axis=-1, keepdims=True)
        return (c * lax.rsqrt(ms + EPS) * gain_ref[...]).astype(BF16)

    cos = cos_ref[...]
    sin = sin_ref[...]

    def rotary(g):
        return g * cos + pltpu.roll(g, D_ROPE, 1) * sin

    qf = jnp.dot(rms(cq_ref, qg_ref), wq_ref[...], preferred_element_type=F32)
    kvf = jnp.dot(rms(ckv_ref, kvg_ref), wkv_ref[...], preferred_element_type=F32)
    k_rot = rotary(kr_ref[...].astype(F32)).astype(BF16)
    for h in range(B_HEADS):
        c0 = h * QK_DIM
        q_ref[h, :, 0:LANES] = (qf[:, c0:c0 + LANES] * q_scale).astype(BF16)
        q_ref[h, :, LANES:QK_DIM] = (rotary(qf[:, c0 + LANES:c0 + QK_DIM]) * q_scale).astype(BF16)
        k_ref[h, :, 0:LANES] = kvf[:, c0:c0 + LANES].astype(BF16)
        k_ref[h, :, LANES:QK_DIM] = k_rot
        v_ref[h, :, :] = kvf[:, c0 + LANES:c0 + QK_DIM].T.astype(BF16)


def _latent(proj, q_gain, kv_gain, wq_ext, wkv, cos_tab, sin_tab, B, S, tm):
    q_scale = (D_NOPE + D_ROPE) ** -0.5 * math.log2(math.e)
    ns = S // tm
    col = lambda g: (lambda b, s: (b, s, g))
    row = lambda b, s: (0, 0)
    return pl.pallas_call(
        functools.partial(_latent_kernel, q_scale=q_scale),
        grid=(B, ns),
        in_specs=[
            pl.BlockSpec((None, tm, Q_LORA), col(G_BCQ * LANES // Q_LORA)),
            pl.BlockSpec((None, tm, KV_LORA), col(G_BCKV * LANES // KV_LORA)),
            pl.BlockSpec((None, tm, LANES), col(G_KR)),
            pl.BlockSpec((1, Q_LORA), row),
            pl.BlockSpec((1, KV_LORA), row),
            pl.BlockSpec((Q_LORA, B_HEADS * QK_DIM), row),
            pl.BlockSpec((KV_LORA, B_HEADS * QK_DIM), row),
            pl.BlockSpec((tm, LANES), lambda b, s: (s, 0)),
            pl.BlockSpec((tm, LANES), lambda b, s: (s, 0)),
        ],
        out_specs=[
            pl.BlockSpec((None, B_HEADS, tm, QK_DIM), lambda b, s: (b, 0, s, 0)),
            pl.BlockSpec((None, B_HEADS, tm, QK_DIM), lambda b, s: (b, 0, s, 0)),
            pl.BlockSpec((None, B_HEADS, D_V, tm), lambda b, s: (b, 0, 0, s)),
        ],
        out_shape=[
            jax.ShapeDtypeStruct((B, B_HEADS, S, QK_DIM), BF16),
            jax.ShapeDtypeStruct((B, B_HEADS, S, QK_DIM), BF16),
            jax.ShapeDtypeStruct((B, B_HEADS, D_V, S), BF16),
        ],
        compiler_params=_params(("parallel", "parallel")),
        name="latent_proj",
    )(proj, proj, proj, q_gain, kv_gain, wq_ext, wkv, cos_tab, sin_tab)


def _mla_kernel(q_ref, k_ref, vt_ref, g_ref, o_ref, s_ref, mrun_ref, mfin_ref, *, nk, tk):
    SUB = 8

    def fold(x, op):
        out = x[0:SUB]
        for t in range(1, tk // SUB):
            out = op(out, x[t * SUB:(t + 1) * SUB])
        return out

    @pl.when(pl.program_id(0) == 0)
    def _():
        s_ref[...] = jnp.zeros(s_ref.shape, F32)
        mfin_ref[...] = jnp.zeros(mfin_ref.shape, F32)

    q = q_ref[...]
    m_prev = mfin_ref[...]
    l_run = None
    acc = None
    for j in range(nk):
        ks = slice(j * tk, (j + 1) * tk)
        p = jnp.exp2(s_ref[ks, :] - m_prev)
        l_blk = fold(p, jnp.add)
        l_run = l_blk if l_run is None else l_run + l_blk
        pv = jnp.dot(vt_ref[:, ks], p.astype(BF16), preferred_element_type=F32)
        acc = pv if acc is None else acc + pv
        s = lax.dot_general(k_ref[ks, :], q, _NT, preferred_element_type=F32)
        s_ref[ks, :] = s
        m_blk = fold(s, jnp.maximum)
        mrun_ref[...] = m_blk if j == 0 else jnp.maximum(mrun_ref[...], m_blk)
    mfin_ref[...] = jnp.max(mrun_ref[...], axis=0, keepdims=True)
    o = (acc / jnp.sum(l_run, axis=0, keepdims=True)).T
    o_ref[...] = (o * _silu(g_ref[...].astype(F32))).astype(o_ref.dtype)


def _mla(q, k, v, proj, B, S, tq, tk):
    nq = S // tq
    n_tiles = B * B_HEADS * nq

    def tile(n):
        return n // (B_HEADS * nq), (n // nq) % B_HEADS, n % nq

    def cur(n):
        return tile(jnp.minimum(n, n_tiles - 1))

    def prev(n):
        return tile(jnp.maximum(n - 1, 0))

    def q_map(n):
        b, h, i = cur(n)
        return b, h, i, 0

    def k_map(n):
        b, h, _ = cur(n)
        return b, h, 0, 0

    def v_map(n):
        b, h, _ = prev(n)
        return b, h, 0, 0

    def g_map(n):
        b, h, i = prev(n)
        return b, i, G_BG + h

    def o_map(n):
        b, h, i = prev(n)
        return b, i, h

    return pl.pallas_call(
        functools.partial(_mla_kernel, nk=S // tk, tk=tk),
        grid=(n_tiles + 1,),
        in_specs=[
            pl.BlockSpec((None, None, tq, QK_DIM), q_map),
            pl.BlockSpec((None, None, S, QK_DIM), k_map),
            pl.BlockSpec((None, None, D_V, S), v_map),
            pl.BlockSpec((None, tq, LANES), g_map),
        ],
        out_specs=pl.BlockSpec((None, tq, D_V), o_map),
        out_shape=jax.ShapeDtypeStruct((B, S, B_WIDTH), BF16),
        scratch_shapes=[pltpu.VMEM((S, tq), F32), pltpu.VMEM((8, tq), F32), pltpu.VMEM((1, tq), F32)],
        compiler_params=_params(("arbitrary",)),
        name="latent_attn",
    )(q, k, v, proj)


def _t5_bucket(rel):
    nb = N_BUCKETS // 2
    max_exact = nb // 2
    bucket = jnp.where(rel > 0, nb, 0)
    n = jnp.abs(rel)
    nf = jnp.maximum(n, 1).astype(F32)
    large = max_exact + (jnp.log(nf / max_exact) / math.log(MAX_DISTANCE / max_exact)
                         * (nb - max_exact)).astype(jnp.int32)
    large = jnp.minimum(large, nb - 1)
    return bucket + jnp.where(n < max_exact, n, large)


def _dil_bias_tiles(rel_bias, g, dil, tq, tk):
    qi = jnp.arange(tq)[:, None]
    kj = jnp.arange(tk)[None, :]
    table = rel_bias.astype(F32)[:, g * A_HEADS:(g + 1) * A_HEADS]
    rel = jnp.stack([kj - qi - off for off in (0, DIL_HALF, 2 * DIL_HALF)])
    bucket = _t5_bucket(rel * dil)
    b = jnp.zeros((3, A_HEADS, tq, tk), F32)
    for i in range(N_BUCKETS):
        b = jnp.where((bucket == i)[:, None], table[i][None, :, None, None], b)
    return jnp.where((jnp.abs(rel) <= DIL_HALF)[:, None], b, NEG)


def _dil_kernel(q_ref, k_ref, v_ref, gate_ref, bias_ref, o_ref, q32_ref, k32_ref, v32_ref, acc_ref, lse_ref,
                *, S, tq, tk, unroll):
    g = pl.program_id(2)
    scale = A_HD ** -0.5

    def attend(q, k, v, bias):
        s = lax.dot_general(q, k, _NT, preferred_element_type=F32) * scale + bias
        m = jnp.max(s, axis=1, keepdims=True)
        p = jnp.exp(s - m)
        den = jnp.sum(p, axis=1, keepdims=True)
        o = jnp.dot(p.astype(BF16), v, preferred_element_type=F32) / den
        return o, jnp.broadcast_to(m + jnp.log(den), (tq, LANES))

    def window(n, nb, L):
        q0 = pl.multiple_of(n * tq, tq)
        ks = pl.multiple_of(jnp.clip(q0 - DIL_HALF, 0, L - tk), DIL_HALF)
        placement = jnp.where(n == 0, 0, jnp.where(n == nb - 1, 2, 1))
        return q0, ks, placement

    @pl.when(g == 0)
    def _():
        k32_ref[...] = k_ref[...].astype(F32)
        v32_ref[...] = v_ref[...].astype(F32)
        nb = S // tq

        def body(n, carry):
            q0, ks, placement = window(n, nb, S)
            o, lse = attend(q_ref[pl.ds(q0, tq), :], k_ref[pl.ds(ks, tk), :], v_ref[pl.ds(ks, tk), :],
                            bias_ref[placement])
            acc_ref[pl.ds(q0, tq), :] = o
            lse_ref[pl.ds(q0, tq), :] = lse
            return carry

        lax.fori_loop(0, nb, body, 0, unroll=min(unroll, nb))

    def strided_group(dil):
        L = S // dil
        nb = L // tq
        nc = L // DIL_HALF
        wch = tk // DIL_HALF
        qch = tq // DIL_HALF
        q32_ref[...] = q_ref[...].astype(F32)

        def stream(r, carry):
            def chunks(ref):
                return [ref[pl.ds(r + c * DIL_HALF * dil, DIL_HALF, stride=dil), :].astype(BF16)
                        for c in range(nc)]

            kc, vc = chunks(k32_ref), chunks(v32_ref)
            for n in range(nb):
                c0 = min(max(n * qch - 1, 0), nc - wch)
                placement = 0 if n == 0 else (2 if n == nb - 1 else 1)
                q_rows = pl.ds(r + n * tq * dil, tq, stride=dil)
                o, lse = attend(q32_ref[q_rows, :].astype(BF16), jnp.concatenate(kc[c0:c0 + wch], axis=0),
                                jnp.concatenate(vc[c0:c0 + wch], axis=0), bias_ref[placement])
                lse_old = lse_ref[q_rows, :]
                m2 = jnp.maximum(lse_old, lse)
                e_old = jnp.exp(lse_old - m2)
                e_new = jnp.exp(lse - m2)
                den = e_old + e_new
                acc_ref[q_rows, :] = (acc_ref[q_rows, :] * e_old + o * e_new) / den
                lse_ref[q_rows, :] = m2 + jnp.log(den)
            return carry

        lax.fori_loop(0, dil, stream, 0, unroll=max(1, min(unroll // nb, dil)))

    for gi in range(1, N_DIL):
        pl.when(g == gi)(functools.partial(strided_group, DIL_PAIRS[gi][1]))

    @pl.when(g == N_DIL - 1)
    def _():
        o_ref[...] = (acc_ref[...] * _silu(gate_ref[...].astype(F32))).astype(o_ref.dtype)


def _dilated(proj, bias, B, S, tq):
    tk = tq + 2 * DIL_HALF
    for _, dil in DIL_PAIRS:
        nb = S // dil // tq
        assert nb >= 2 and nb * tq * dil == S, (S, dil, tq)
    q_group = lambda g: G_AQ[1] * g + (G_AQ[2] - 2 * G_AQ[1]) * (g // 2)
    col = lambda g0: (lambda b, h, g: (b, 0, g0 + h))
    return pl.pallas_call(
        functools.partial(_dil_kernel, S=S, tq=tq, tk=tk, unroll=DIL_UNROLL),
        grid=(B, A_HEADS, N_DIL),
        in_specs=[
            pl.BlockSpec((None, S, LANES), lambda b, h, g: (b, 0, q_group(g) + h)),
            pl.BlockSpec((None, S, LANES), col(G_AK)),
            pl.BlockSpec((None, S, LANES), col(G_AV)),
            pl.BlockSpec((None, S, LANES), col(G_AG)),
            pl.BlockSpec((None, 3, None, tq, tk), lambda b, h, g: (g, 0, h, 0, 0)),
        ],
        out_specs=pl.BlockSpec((None, S, LANES), lambda b, h, g: (b, 0, h)),
        out_shape=jax.ShapeDtypeStruct((B, S, A_WIDTH), BF16),
        scratch_shapes=[pltpu.VMEM((S, LANES), F32) for _ in range(5)],
        compiler_params=_params(("parallel", "arbitrary", "arbitrary")),
        name="dilated_attn",
    )(proj, proj, proj, proj, bias)


def _memkv_kernel(mem_ref, g_ref, w_ref, o_ref):
    x = mem_ref[...]
    ms = jnp.mean(x * x, axis=-1, keepdims=True)
    hn = (x * lax.rsqrt(ms + EPS) * g_ref[...]).astype(BF16)
    o_ref[...] = jnp.dot(hn, w_ref[...], preferred_element_type=F32).astype(o_ref.dtype)


def _memkv(mem, gain, w_bf16):
    B, M, _ = mem.shape
    return pl.pallas_call(
        _memkv_kernel,
        grid=(B,),
        in_specs=[
            pl.BlockSpec((None, M, D_MODEL), lambda b: (b, 0, 0)),
            pl.BlockSpec((1, D_MODEL), lambda b: (0, 0)),
            pl.BlockSpec((D_MODEL, 2 * C_WIDTH), lambda b: (0, 0)),
        ],
        out_specs=pl.BlockSpec((None, M, 2 * C_WIDTH), lambda b: (b, 0, 0)),
        out_shape=jax.ShapeDtypeStruct((B, M, 2 * C_WIDTH), BF16),
        compiler_params=_params(("parallel",)),
        name="mem_kv_proj",
    )(mem, gain, w_bf16)


def _out_kernel(ya_ref, yb_ref, cq_ref, cg_ref, mkv_ref, w_ref, pg_ref, x_ref, o_ref, y_ref):
    n = pl.program_id(0)
    cur = lax.rem(n, 2)
    prv = 1 - cur

    @pl.when(n == 0)
    def _():
        y_ref[1] = jnp.zeros(y_ref.shape[1:], BF16)

    y = jnp.dot(y_ref[prv], w_ref[...], preferred_element_type=F32)
    ms = jnp.mean(y * y, axis=-1, keepdims=True)
    o_ref[...] = x_ref[...] + y * lax.rsqrt(ms + EPS) * pg_ref[...]

    y_ref[cur, :, 0:A_WIDTH] = ya_ref[...]
    y_ref[cur, :, A_WIDTH:A_WIDTH + B_WIDTH] = yb_ref[...]
    scale = C_HD ** -0.5
    for h in range(C_HEADS):
        cs = slice(h * C_HD, (h + 1) * C_HD)
        s = lax.dot_general(cq_ref[:, cs], mkv_ref[:, cs], _NT, preferred_element_type=F32) * scale
        m = jnp.max(s, axis=1, keepdims=True)
        p = jnp.exp(s - m)
        den = jnp.sum(p, axis=1, keepdims=True)
        mv = mkv_ref[:, C_WIDTH + h * C_HD:C_WIDTH + (h + 1) * C_HD]
        oc = jnp.dot(p.astype(BF16), mv, preferred_element_type=F32) / den
        c0 = A_WIDTH + B_WIDTH + h * C_HD
        y_ref[cur, :, c0:c0 + C_HD] = (oc * _silu(cg_ref[:, cs].astype(F32))).astype(BF16)


def _outproj(ya, yb, proj, mkv, w_out, post_gain, x, tm):
    B, S, _ = x.shape
    M = mkv.shape[1]
    ns = S // tm
    n_tiles = B * ns

    def cur(n):
        t = jnp.minimum(n, n_tiles - 1)
        return t // ns, t % ns

    def prev(n):
        t = jnp.maximum(n - 1, 0)
        return t // ns, t % ns

    def tok(width, blk, which):
        return pl.BlockSpec((None, tm, width), lambda n: (*which(n), blk))

    const = lambda shape: pl.BlockSpec(shape, lambda n: (0,) * len(shape))
    return pl.pallas_call(
        _out_kernel,
        grid=(n_tiles + 1,),
        in_specs=[
            tok(A_WIDTH, 0, cur),
            tok(B_WIDTH, 0, cur),
            tok(C_WIDTH, G_CQ * LANES // C_WIDTH, cur),
            tok(C_WIDTH, G_CG * LANES // C_WIDTH, cur),
            pl.BlockSpec((None, M, 2 * C_WIDTH), lambda n: (cur(n)[0], 0, 0)),
            const((D_MIX, D_MODEL)),
            const((1, D_MODEL)),
            tok(D_MODEL, 0, prev),
        ],
        out_specs=tok(D_MODEL, 0, prev),
        out_shape=jax.ShapeDtypeStruct((B, S, D_MODEL), x.dtype),
        scratch_shapes=[pltpu.VMEM((2, tm, D_MIX), BF16)],
        compiler_params=_params(("arbitrary",)),
        name="outproj",
    )(ya, yb, proj, proj, mkv, w_out, post_gain, x)


def _swap_halves(w):
    half = D_ROPE // 2
    return jnp.concatenate([w[..., half:], w[..., :half]], axis=-1)


def _prep_weights(w_in, w_uq, w_ukv, w_mem_kv, w_out):
    o_kr = N_DIL * A_WIDTH + 3 * A_WIDTH + Q_LORA + KV_LORA
    o_bg = o_kr + D_ROPE
    o_cq = o_bg + B_WIDTH
    kr = w_in[:, o_kr:o_bg]
    aq = [w_in[:, g * A_WIDTH:(g + 1) * A_WIDTH] for g in range(N_DIL)]
    o_ak = N_DIL * A_WIDTH
    w_in_r = jnp.concatenate(
        [aq[0], aq[1], w_in[:, o_ak:o_ak + 2 * A_WIDTH], aq[2], w_in[:, o_ak + 2 * A_WIDTH:o_kr],
         w_in[:, o_cq:], w_in[:, o_bg:o_cq], kr, _swap_halves(kr),
         jnp.zeros((D_MODEL, D_INP - (G_KR + 1) * LANES), w_in.dtype)], axis=1).astype(BF16)
    wq = w_uq.reshape(Q_LORA, B_HEADS, D_NOPE + D_ROPE)
    rope = wq[:, :, D_NOPE:]
    wq_ext = jnp.concatenate([wq[:, :, :D_NOPE], rope, _swap_halves(rope)], axis=-1)
    wq_ext = wq_ext.reshape(Q_LORA, B_HEADS * QK_DIM).astype(BF16)
    return w_in_r, wq_ext, w_ukv.astype(BF16), w_mem_kv.astype(BF16), w_out.astype(BF16)


def _rope_tables(S):
    inv = ROPE_BASE ** (-jnp.arange(0, D_ROPE, 2, dtype=F32) / D_ROPE)
    ang = jnp.arange(S, dtype=F32)[:, None] * inv[None, :]
    cos, sin = jnp.cos(ang), jnp.sin(ang)
    zeros = jnp.zeros((S, LANES - D_ROPE), F32)
    return (jnp.concatenate([cos, cos, zeros], axis=1),
            jnp.concatenate([-sin, sin, zeros], axis=1))


def _dil_biases(rel_bias, tq):
    return jnp.stack([_dil_bias_tiles(rel_bias, g, dil, tq, tq + 2 * DIL_HALF)
                      for g, (_, dil) in enumerate(DIL_PAIRS)])


def _layer(x, mem, pre_gain, q_gain, kv_gain, mem_gain, post_gain, biases, weights):
    w_in_r, wq_ext, wkv, wmem, wout = weights
    B, S, _ = x.shape
    t = _tiles(S)
    proj = _inproj(x.reshape(B * S, D_MODEL), pre_gain, w_in_r, t["tm_in"], t["tn_in"])
    proj = proj.reshape(B, S, D_INP)
    cos_tab, sin_tab = _rope_tables(S)
    q, k, v = _latent(proj, q_gain, kv_gain, wq_ext, wkv, cos_tab, sin_tab, B, S, t["tm_lat"])
    yb = _mla(q, k, v, proj, B, S, t["tq_mla"], t["tk_mla"])
    ya = _dilated(proj, biases, B, S, t["tq_dil"])
    mkv = _memkv(mem, mem_gain, wmem)
    return _outproj(ya, yb, proj, mkv, wout, post_gain, x, t["tm_out"])


def kernel(x_prompt, x_sample, mem_prompt, mem_sample, pre_gain, w_in, q_gain, w_uq, kv_gain, w_ukv,
           mem_gain, w_mem_kv, w_out, post_gain, rel_bias):
    depth = pre_gain.shape[0]
    weights = [_prep_weights(w_in[l], w_uq[l], w_ukv[l], w_mem_kv[l], w_out[l]) for l in range(depth)]
    biases = _dil_biases(rel_bias, _tiles(x_prompt.shape[1])["tq_dil"])

    def trunk(x, mem):
        for l in range(depth):
            x = _layer(x, mem, pre_gain[l][None], q_gain[l][None], kv_gain[l][None], mem_gain[l][None],
                       post_gain[l][None], biases, weights[l])
        return x

    return trunk(x_prompt, mem_prompt), trunk(x_sample, mem_sample)
```

```python
import functools
import math

import jax
import jax.numpy as jnp
import numpy as np
from jax import lax
from jax.experimental import pallas as pl
from jax.experimental.pallas import tpu as pltpu

D_MODEL = 2048
DIL_PAIRS = ((128, 1), (512, 4), (2048, 16))
N_DIL = 3
DIL_HALF = 64
DIL_UNROLL = 8
A_HEADS = 4
A_HD = 128
A_WIDTH = A_HEADS * A_HD
B_HEADS = 8
Q_LORA = 512
KV_LORA = 512
D_NOPE = 128
D_ROPE = 64
D_V = 128
B_WIDTH = B_HEADS * D_V
ROPE_BASE = 10000.0
C_HEADS = 4
C_HD = 128
C_WIDTH = C_HEADS * C_HD
D_MIX = A_WIDTH + B_WIDTH + C_WIDTH
N_BUCKETS = 32
MAX_DISTANCE = 1024
EPS = 1e-6
NEG = -1e30

LANES = 128
VMEM_LIMIT_BYTES = 56 * 1024 * 1024

G_AQ = (0, 4, 16)
G_AK = 8
G_AV = 12
G_AG = 20
G_BCQ = 24
G_BCKV = 28
G_CQ = 32
G_CG = 36
G_BG = 40
G_KR = 48
N_GROUPS = 50
MXU_COLS = 2 * LANES
D_INP = N_GROUPS * LANES
QK_DIM = 2 * LANES

F32 = jnp.float32
BF16 = jnp.bfloat16
_NT = (((1,), (1,)), ((), ()))


def _tiles(S):
    return dict(
        tm_in=min(1024, S),
        tn_in=5 * MXU_COLS,
        tm_lat=min(512, S),
        tq_mla=min(512, S),
        tk_mla=min(512, S),
        tq_dil=128,
        tm_out=min(512, S),
    )


def _params(sem):
    return pltpu.CompilerParams(dimension_semantics=sem, vmem_limit_bytes=VMEM_LIMIT_BYTES)


def _silu(g):
    return g * (1.0 / (1.0 + jnp.exp(-g)))


def _inproj_kernel(x_ref, g_ref, w_ref, o_ref, h_ref, *, rows):
    @pl.when(pl.program_id(1) == 0)
    def _():
        tm = x_ref.shape[0]
        for r0 in range(0, tm, rows):
            x = x_ref[r0:r0 + rows, :]
            ms = jnp.mean(x * x, axis=-1, keepdims=True)
            h_ref[r0:r0 + rows, :] = (x * lax.rsqrt(ms + EPS) * g_ref[...]).astype(BF16)

    o_ref[...] = jnp.dot(h_ref[...], w_ref[...], preferred_element_type=F32).astype(o_ref.dtype)


def _inproj(x2d, gain, w_bf16, tm, tn):
    T = x2d.shape[0]
    return pl.pallas_call(
        functools.partial(_inproj_kernel, rows=min(256, tm)),
        grid=(T // tm, D_INP // tn),
        in_specs=[
            pl.BlockSpec((tm, D_MODEL), lambda i, j: (i, 0)),
            pl.BlockSpec((1, D_MODEL), lambda i, j: (0, 0)),
            pl.BlockSpec((D_MODEL, tn), lambda i, j: (0, j)),
        ],
        out_specs=pl.BlockSpec((tm, tn), lambda i, j: (i, j)),
        out_shape=jax.ShapeDtypeStruct((T, D_INP), BF16),
        scratch_shapes=[pltpu.VMEM((tm, D_MODEL), BF16)],
        compiler_params=_params(("parallel", "arbitrary")),
        name="inproj",
    )(x2d, gain, w_bf16)


def _latent_kernel(cq_ref, ckv_ref, kr_ref, qg_ref, kvg_ref, wq_ref, wkv_ref, cos_ref, sin_ref,
                   q_ref, k_ref, v_ref, *, q_scale):
    def rms(c_ref, gain_ref):
        c = c_ref[...].astype(F32)
        ms = jnp.mean(c * c, axis=-1, keepdims=True)
        return (c * lax.rsqrt(ms + EPS) * gain_ref[...]).astype(BF16)

    cos = cos_ref[...]
    sin = sin_ref[...]

    def rotary(g):
        return g * cos + pltpu.roll(g, D_ROPE, 1) * sin

    qf = jnp.dot(rms(cq_ref, qg_ref), wq_ref[...], preferred_element_type=F32)
    kvf = jnp.dot(rms(ckv_ref, kvg_ref), wkv_ref[...], preferred_element_type=F32)
    k_rot = rotary(kr_ref[...].astype(F32)).astype(BF16)
    for h in range(B_HEADS):
        c0 = h * QK_DIM
        q_ref[h, 0:LANES, :] = (qf[:, c0:c0 + LANES] * q_scale).T.astype(BF16)
        q_ref[h, LANES:QK_DIM, :] = (rotary(qf[:, c0 + LANES:c0 + QK_DIM]) * q_scale).T.astype(BF16)
        k_ref[h, :, 0:LANES] = kvf[:, c0:c0 + LANES].astype(BF16)
        k_ref[h, :, LANES:QK_DIM] = k_rot
        v_ref[h, :, :] = kvf[:, c0 + LANES:c0 + QK_DIM].T.astype(BF16)


def _latent(proj, q_gain, kv_gain, wq_ext, wkv, cos_tab, sin_tab, B, S, tm):
    q_scale = (D_NOPE + D_ROPE) ** -0.5 * math.log2(math.e)
    ns = S // tm
    col = lambda g: (lambda b, s: (b, s, g))
    row = lambda b, s: (0, 0)
    return pl.pallas_call(
        functools.partial(_latent_kernel, q_scale=q_scale),
        grid=(B, ns),
        in_specs=[
            pl.BlockSpec((None, tm, Q_LORA), col(G_BCQ * LANES // Q_LORA)),
            pl.BlockSpec((None, tm, KV_LORA), col(G_BCKV * LANES // KV_LORA)),
            pl.BlockSpec((None, tm, LANES), col(G_KR)),
            pl.BlockSpec((1, Q_LORA), row),
            pl.BlockSpec((1, KV_LORA), row),
            pl.BlockSpec((Q_LORA, B_HEADS * QK_DIM), row),
            pl.BlockSpec((KV_LORA, B_HEADS * QK_DIM), row),
            pl.BlockSpec((tm, LANES), lambda b, s: (s, 0)),
            pl.BlockSpec((tm, LANES), lambda b, s: (s, 0)),
        ],
        out_specs=[
            pl.BlockSpec((None, B_HEADS, QK_DIM, tm), lambda b, s: (b, 0, 0, s)),
            pl.BlockSpec((None, B_HEADS, tm, QK_DIM), lambda b, s: (b, 0, s, 0)),
            pl.BlockSpec((None, B_HEADS, D_V, tm), lambda b, s: (b, 0, 0, s)),
        ],
        out_shape=[
            jax.ShapeDtypeStruct((B, B_HEADS, QK_DIM, S), BF16),
            jax.ShapeDtypeStruct((B, B_HEADS, S, QK_DIM), BF16),
            jax.ShapeDtypeStruct((B, B_HEADS, D_V, S), BF16),
        ],
        compiler_params=_params(("parallel", "parallel")),
        name="latent_proj",
    )(proj, proj, proj, q_gain, kv_gain, wq_ext, wkv, cos_tab, sin_tab)


def _mla_kernel(q_ref, k_ref, vt_ref, g_ref, o_ref, s_ref, mrun_ref, mfin_ref, *, nk, tk):
    SUB = 8

    def fold(x, op):
        out = x[0:SUB]
        for t in range(1, tk // SUB):
            out = op(out, x[t * SUB:(t + 1) * SUB])
        return out

    @pl.when(pl.program_id(0) == 0)
    def _():
        s_ref[...] = jnp.zeros(s_ref.shape, F32)
        mfin_ref[...] = jnp.zeros(mfin_ref.shape, F32)

    qt = q_ref[...]
    m_prev = mfin_ref[...]
    l_run = None
    acc = None
    for j in range(nk):
        ks = slice(j * tk, (j + 1) * tk)
        p = jnp.exp2(s_ref[ks, :] - m_prev)
        l_blk = fold(p, jnp.add)
        l_run = l_blk if l_run is None else l_run + l_blk
        pv = jnp.dot(vt_ref[:, ks], p.astype(BF16), preferred_element_type=F32)
        acc = pv if acc is None else acc + pv
        s = jnp.dot(k_ref[ks, :], qt, preferred_element_type=F32)
        s_ref[ks, :] = s
        m_blk = fold(s, jnp.maximum)
        mrun_ref[...] = m_blk if j == 0 else jnp.maximum(mrun_ref[...], m_blk)
    mfin_ref[...] = jnp.max(mrun_ref[...], axis=0, keepdims=True)
    o = (acc / jnp.sum(l_run, axis=0, keepdims=True)).T
    o_ref[...] = (o * _silu(g_ref[...].astype(F32))).astype(o_ref.dtype)


def _mla(q, k, v, proj, B, S, tq, tk):
    nq = S // tq
    n_tiles = B * B_HEADS * nq

    def tile(n):
        return n // (B_HEADS * nq), (n // nq) % B_HEADS, n % nq

    def cur(n):
        return tile(jnp.minimum(n, n_tiles - 1))

    def prev(n):
        return tile(jnp.maximum(n - 1, 0))

    def q_map(n):
        b, h, i = cur(n)
        return b, h, 0, i

    def k_map(n):
        b, h, _ = cur(n)
        return b, h, 0, 0

    def v_map(n):
        b, h, _ = prev(n)
        return b, h, 0, 0

    def g_map(n):
        b, h, i = prev(n)
        return b, i, G_BG + h

    def o_map(n):
        b, h, i = prev(n)
        return b, i, h

    return pl.pallas_call(
        functools.partial(_mla_kernel, nk=S // tk, tk=tk),
        grid=(n_tiles + 1,),
        in_specs=[
            pl.BlockSpec((None, None, QK_DIM, tq), q_map),
            pl.BlockSpec((None, None, S, QK_DIM), k_map),
            pl.BlockSpec((None, None, D_V, S), v_map),
            pl.BlockSpec((None, tq, LANES), g_map),
        ],
        out_specs=pl.BlockSpec((None, tq, D_V), o_map),
        out_shape=jax.ShapeDtypeStruct((B, S, B_WIDTH), BF16),
        scratch_shapes=[pltpu.VMEM((S, tq), F32), pltpu.VMEM((8, tq), F32), pltpu.VMEM((1, tq), F32)],
        compiler_params=_params(("arbitrary",)),
        name="latent_attn",
    )(q, k, v, proj)


def _t5_bucket(rel):
    nb = N_BUCKETS // 2
    max_exact = nb // 2
    bucket = jnp.where(rel > 0, nb, 0)
    n = jnp.abs(rel)
    nf = jnp.maximum(n, 1).astype(F32)
    large = max_exact + (jnp.log(nf / max_exact) / math.log(MAX_DISTANCE / max_exact)
                         * (nb - max_exact)).astype(jnp.int32)
    large = jnp.minimum(large, nb - 1)
    return bucket + jnp.where(n < max_exact, n, large)


def _dil_bias_tiles(rel_bias, g, dil, tq, tk):
    qi = jnp.arange(tq)[:, None]
    kj = jnp.arange(tk)[None, :]
    table = rel_bias.astype(F32)[:, g * A_HEADS:(g + 1) * A_HEADS]
    rel = jnp.stack([kj - qi - off for off in (0, DIL_HALF, 2 * DIL_HALF)])
    bucket = _t5_bucket(rel * dil)
    b = jnp.zeros((3, A_HEADS, tq, tk), F32)
    for i in range(N_BUCKETS):
        b = jnp.where((bucket == i)[:, None], table[i][None, :, None, None], b)
    return jnp.where((jnp.abs(rel) <= DIL_HALF)[:, None], b, NEG)


def _dil_kernel(q_ref, k_ref, v_ref, gate_ref, bias_ref, o_ref, q32_ref, k32_ref, v32_ref, acc_ref, lse_ref,
                *, S, tq, tk, unroll):
    g = pl.program_id(2)
    scale = A_HD ** -0.5

    def attend(q, k, v, bias):
        s = lax.dot_general(q, k, _NT, preferred_element_type=F32) * scale + bias
        m = jnp.max(s, axis=1, keepdims=True)
        p = jnp.exp(s - m)
        den = jnp.sum(p, axis=1, keepdims=True)
        o = jnp.dot(p.astype(BF16), v, preferred_element_type=F32) / den
        return o, jnp.broadcast_to(m + jnp.log(den), (tq, LANES))

    def window(n, nb, L):
        q0 = pl.multiple_of(n * tq, tq)
        ks = pl.multiple_of(jnp.clip(q0 - DIL_HALF, 0, L - tk), DIL_HALF)
        placement = jnp.where(n == 0, 0, jnp.where(n == nb - 1, 2, 1))
        return q0, ks, placement

    @pl.when(g == 0)
    def _():
        k32_ref[...] = k_ref[...].astype(F32)
        v32_ref[...] = v_ref[...].astype(F32)
        nb = S // tq

        def body(n, carry):
            q0, ks, placement = window(n, nb, S)
            o, lse = attend(q_ref[pl.ds(q0, tq), :], k_ref[pl.ds(ks, tk), :], v_ref[pl.ds(ks, tk), :],
                            bias_ref[placement])
            acc_ref[pl.ds(q0, tq), :] = o
            lse_ref[pl.ds(q0, tq), :] = lse
            return carry

        lax.fori_loop(0, nb, body, 0, unroll=min(unroll, nb))

    def strided_group(dil):
        L = S // dil
        nb = L // tq
        nc = L // DIL_HALF
        wch = tk // DIL_HALF
        qch = tq // DIL_HALF
        q32_ref[...] = q_ref[...].astype(F32)

        def stream(r, carry):
            def chunks(ref):
                return [ref[pl.ds(r + c * DIL_HALF * dil, DIL_HALF, stride=dil), :].astype(BF16)
                        for c in range(nc)]

            kc, vc = chunks(k32_ref), chunks(v32_ref)
            for n in range(nb):
                c0 = min(max(n * qch - 1, 0), nc - wch)
                placement = 0 if n == 0 else (2 if n == nb - 1 else 1)
                q_rows = pl.ds(r + n * tq * dil, tq, stride=dil)
                o, lse = attend(q32_ref[q_rows, :].astype(BF16), jnp.concatenate(kc[c0:c0 + wch], axis=0),
                                jnp.concatenate(vc[c0:c0 + wch], axis=0), bias_ref[placement])
                lse_old = lse_ref[q_rows, :]
                m2 = jnp.maximum(lse_old, lse)
                e_old = jnp.exp(lse_old - m2)
                e_new = jnp.exp(lse - m2)
                den = e_old + e_new
                acc_ref[q_rows, :] = (acc_ref[q_rows, :] * e_old + o * e_new) / den
                lse_ref[q_rows, :] = m2 + jnp.log(den)
            return carry

        lax.fori_loop(0, dil, stream, 0, unroll=max(1, min(unroll // nb, dil)))

    for gi in range(1, N_DIL):
        pl.when(g == gi)(functools.partial(strided_group, DIL_PAIRS[gi][1]))

    @pl.when(g == N_DIL - 1)
    def _():
        o_ref[...] = (acc_ref[...] * _silu(gate_ref[...].astype(F32))).astype(o_ref.dtype)


def _dilated(proj, bias, B, S, tq):
    tk = tq + 2 * DIL_HALF
    for _, dil in DIL_PAIRS:
        nb = S // dil // tq
        assert nb >= 2 and nb * tq * dil == S, (S, dil, tq)
    q_group = lambda g: G_AQ[1] * g + (G_AQ[2] - 2 * G_AQ[1]) * (g // 2)
    col = lambda g0: (lambda b, h, g: (b, 0, g0 + h))
    return pl.pallas_call(
        functools.partial(_dil_kernel, S=S, tq=tq, tk=tk, unroll=DIL_UNROLL),
        grid=(B, A_HEADS, N_DIL),
        in_specs=[
            pl.BlockSpec((None, S, LANES), lambda b, h, g: (b, 0, q_group(g) + h)),
            pl.BlockSpec((None, S, LANES), col(G_AK)),
            pl.BlockSpec((None, S, LANES), col(G_AV)),
            pl.BlockSpec((None, S, LANES), col(G_AG)),
            pl.BlockSpec((None, 3, None, tq, tk), lambda b, h, g: (g, 0, h, 0, 0)),
        ],
        out_specs=pl.BlockSpec((None, S, LANES), lambda b, h, g: (b, 0, h)),
        out_shape=jax.ShapeDtypeStruct((B, S, A_WIDTH), BF16),
        scratch_shapes=[pltpu.VMEM((S, LANES), F32) for _ in range(5)],
        compiler_params=_params(("parallel", "arbitrary", "arbitrary")),
        name="dilated_attn",
    )(proj, proj, proj, proj, bias)


def _memkv_kernel(mem_ref, g_ref, w_ref, o_ref):
    x = mem_ref[...]
    ms = jnp.mean(x * x, axis=-1, keepdims=True)
    hn = (x * lax.rsqrt(ms + EPS) * g_ref[...]).astype(BF16)
    o_ref[...] = jnp.dot(hn, w_ref[...], preferred_element_type=F32).astype(o_ref.dtype)


def _memkv(mem, gain, w_bf16):
    B, M, _ = mem.shape
    return pl.pallas_call(
        _memkv_kernel,
        grid=(B,),
        in_specs=[
            pl.BlockSpec((None, M, D_MODEL), lambda b: (b, 0, 0)),
            pl.BlockSpec((1, D_MODEL), lambda b: (0, 0)),
            pl.BlockSpec((D_MODEL, 2 * C_WIDTH), lambda b: (0, 0)),
        ],
        out_specs=pl.BlockSpec((None, M, 2 * C_WIDTH), lambda b: (b, 0, 0)),
        out_shape=jax.ShapeDtypeStruct((B, M, 2 * C_WIDTH), BF16),
        compiler_params=_params(("parallel",)),
        name="mem_kv_proj",
    )(mem, gain, w_bf16)


def _out_kernel(ya_ref, yb_ref, cq_ref, cg_ref, mkv_ref, w_ref, pg_ref, x_ref, o_ref, y_ref):
    n = pl.program_id(0)
    cur = lax.rem(n, 2)
    prv = 1 - cur

    @pl.when(n == 0)
    def _():
        y_ref[1] = jnp.zeros(y_ref.shape[1:], BF16)

    y = jnp.dot(y_ref[prv], w_ref[...], preferred_element_type=F32)
    ms = jnp.mean(y * y, axis=-1, keepdims=True)
    o_ref[...] = x_ref[...] + y * lax.rsqrt(ms + EPS) * pg_ref[...]

    y_ref[cur, :, 0:A_WIDTH] = ya_ref[...]
    y_ref[cur, :, A_WIDTH:A_WIDTH + B_WIDTH] = yb_ref[...]
    scale = C_HD ** -0.5
    for h in range(C_HEADS):
        cs = slice(h * C_HD, (h + 1) * C_HD)
        s = lax.dot_general(cq_ref[:, cs], mkv_ref[:, cs], _NT, preferred_element_type=F32) * scale
        m = jnp.max(s, axis=1, keepdims=True)
        p = jnp.exp(s - m)
        den = jnp.sum(p, axis=1, keepdims=True)
        mv = mkv_ref[:, C_WIDTH + h * C_HD:C_WIDTH + (h + 1) * C_HD]
        oc = jnp.dot(p.astype(BF16), mv, preferred_element_type=F32) / den
        c0 = A_WIDTH + B_WIDTH + h * C_HD
        y_ref[cur, :, c0:c0 + C_HD] = (oc * _silu(cg_ref[:, cs].astype(F32))).astype(BF16)


def _outproj(ya, yb, proj, mkv, w_out, post_gain, x, tm):
    B, S, _ = x.shape
    M = mkv.shape[1]
    ns = S // tm
    n_tiles = B * ns

    def cur(n):
        t = jnp.minimum(n, n_tiles - 1)
        return t // ns, t % ns

    def prev(n):
        t = jnp.maximum(n - 1, 0)
        return t // ns, t % ns

    def tok(width, blk, which):
        return pl.BlockSpec((None, tm, width), lambda n: (*which(n), blk))

    const = lambda shape: pl.BlockSpec(shape, lambda n: (0,) * len(shape))
    return pl.pallas_call(
        _out_kernel,
        grid=(n_tiles + 1,),
        in_specs=[
            tok(A_WIDTH, 0, cur),
            tok(B_WIDTH, 0, cur),
            tok(C_WIDTH, G_CQ * LANES // C_WIDTH, cur),
            tok(C_WIDTH, G_CG * LANES // C_WIDTH, cur),
            pl.BlockSpec((None, M, 2 * C_WIDTH), lambda n: (cur(n)[0], 0, 0)),
            const((D_MIX, D_MODEL)),
            const((1, D_MODEL)),
            tok(D_MODEL, 0, prev),
        ],
        out_specs=tok(D_MODEL, 0, prev),
        out_shape=jax.ShapeDtypeStruct((B, S, D_MODEL), x.dtype),
        scratch_shapes=[pltpu.VMEM((2, tm, D_MIX), BF16)],
        compiler_params=_params(("arbitrary",)),
        name="outproj",
    )(ya, yb, proj, proj, mkv, w_out, post_gain, x)


def _swap_halves(w):
    half = D_ROPE // 2
    return jnp.concatenate([w[..., half:], w[..., :half]], axis=-1)


def _prep_weights(w_in, w_uq, w_ukv, w_mem_kv, w_out):
    o_kr = N_DIL * A_WIDTH + 3 * A_WIDTH + Q_LORA + KV_LORA
    o_bg = o_kr + D_ROPE
    o_cq = o_bg + B_WIDTH
    w_in = w_in.astype(BF16)
    kr = w_in[:, o_kr:o_bg]
    aq = [w_in[:, g * A_WIDTH:(g + 1) * A_WIDTH] for g in range(N_DIL)]
    o_ak = N_DIL * A_WIDTH
    w_in_r = jnp.concatenate(
        [aq[0], aq[1], w_in[:, o_ak:o_ak + 2 * A_WIDTH], aq[2], w_in[:, o_ak + 2 * A_WIDTH:o_kr],
         w_in[:, o_cq:], w_in[:, o_bg:o_cq], kr, _swap_halves(kr),
         jnp.zeros((D_MODEL, D_INP - (G_KR + 1) * LANES), BF16)], axis=1)
    wq = w_uq.reshape(Q_LORA, B_HEADS, D_NOPE + D_ROPE)
    rope = wq[:, :, D_NOPE:]
    wq_ext = jnp.concatenate([wq[:, :, :D_NOPE], rope, _swap_halves(rope)], axis=-1)
    wq_ext = wq_ext.reshape(Q_LORA, B_HEADS * QK_DIM).astype(BF16)
    return w_in_r, wq_ext, w_ukv.astype(BF16), w_mem_kv.astype(BF16), w_out.astype(BF16)


def _rope_tables(S):
    inv = ROPE_BASE ** (-jnp.arange(0, D_ROPE, 2, dtype=F32) / D_ROPE)
    ang = jnp.arange(S, dtype=F32)[:, None] * inv[None, :]
    cos, sin = jnp.cos(ang), jnp.sin(ang)
    zeros = jnp.zeros((S, LANES - D_ROPE), F32)
    return (jnp.concatenate([cos, cos, zeros], axis=1),
            jnp.concatenate([-sin, sin, zeros], axis=1))


def _dil_biases(rel_bias, tq):
    return jnp.stack([_dil_bias_tiles(rel_bias, g, dil, tq, tq + 2 * DIL_HALF)
                      for g, (_, dil) in enumerate(DIL_PAIRS)])


def _layer(x, mem, pre_gain, q_gain, kv_gain, mem_gain, post_gain, biases, weights):
    w_in_r, wq_ext, wkv, wmem, wout = weights
    B, S, _ = x.shape
    t = _tiles(S)
    proj = _inproj(x.reshape(B * S, D_MODEL), pre_gain, w_in_r, t["tm_in"], t["tn_in"])
    proj = proj.reshape(B, S, D_INP)
    cos_tab, sin_tab = _rope_tables(S)
    q, k, v = _latent(proj, q_gain, kv_gain, wq_ext, wkv, cos_tab, sin_tab, B, S, t["tm_lat"])
    yb = _mla(q, k, v, proj, B, S, t["tq_mla"], t["tk_mla"])
    ya = _dilated(proj, biases, B, S, t["tq_dil"])
    mkv = _memkv(mem, mem_gain, wmem)
    return _outproj(ya, yb, proj, mkv, wout, post_gain, x, t["tm_out"])


def kernel(x_prompt, x_sample, mem_prompt, mem_sample, pre_gain, w_in, q_gain, w_uq, kv_gain, w_ukv,
           mem_gain, w_mem_kv, w_out, post_gain, rel_bias):
    depth = pre_gain.shape[0]
    weights = [_prep_weights(w_in[l], w_uq[l], w_ukv[l], w_mem_kv[l], w_out[l]) for l in range(depth)]
    biases = _dil_biases(rel_bias, _tiles(x_prompt.shape[1])["tq_dil"])

    def trunk(x, mem):
        for l in range(depth):
            x = _layer(x, mem, pre_gain[l][None], q_gain[l][None], kv_gain[l][None], mem_gain[l][None],
                       post_gain[l][None], biases, weights[l])
        return x

    return trunk(x_prompt, mem_prompt), trunk(x_sample, mem_sample)
```

```python
import functools
import math

import jax
import jax.numpy as jnp
import numpy as np
from jax import lax
from jax.experimental import pallas as pl
from jax.experimental.pallas import tpu as pltpu

D_MODEL = 2048
DIL_PAIRS = ((128, 1), (512, 4), (2048, 16))
N_DIL = 3
DIL_HALF = 64
DIL_UNROLL = 16
A_HEADS = 4
A_HD = 128
A_WIDTH = A_HEADS * A_HD
B_HEADS = 8
Q_LORA = 512
KV_LORA = 512
D_NOPE = 128
D_ROPE = 64
D_V = 128
B_WIDTH = B_HEADS * D_V
ROPE_BASE = 10000.0
C_HEADS = 4
C_HD = 128
C_WIDTH = C_HEADS * C_HD
D_MIX = A_WIDTH + B_WIDTH + C_WIDTH
N_BUCKETS = 32
MAX_DISTANCE = 1024
EPS = 1e-6
NEG = -1e30

LANES = 128
VMEM_LIMIT_BYTES = 56 * 1024 * 1024

G_AQ = (0, 4, 8)
G_AK = 12
G_AV = 16
G_AG = 20
G_BCQ = 24
G_BCKV = 28
G_BG = 32
G_CQ = 40
G_CG = 44
G_KR = 48
N_GROUPS = 50
MXU_COLS = 2 * LANES
D_INP = N_GROUPS * LANES
QK_DIM = 2 * LANES

F32 = jnp.float32
BF16 = jnp.bfloat16
_NT = (((1,), (1,)), ((), ()))


def _tiles(S):
    return dict(
        tm_in=min(1024, S),
        tn_in=5 * MXU_COLS,
        tm_lat=min(512, S),
        tq_mla=min(512, S),
        tk_mla=min(512, S),
        tq_dil=128,
        tm_out=min(512, S),
    )


def _params(sem):
    return pltpu.CompilerParams(dimension_semantics=sem, vmem_limit_bytes=VMEM_LIMIT_BYTES)


def _silu(g):
    return g * (1.0 / (1.0 + jnp.exp(-g)))


def _inproj_kernel(x_ref, g_ref, w_ref, o_ref, h_ref, *, rows):
    @pl.when(pl.program_id(1) == 0)
    def _():
        tm = x_ref.shape[0]
        for r0 in range(0, tm, rows):
            x = x_ref[r0:r0 + rows, :]
            ms = jnp.mean(x * x, axis=-1, keepdims=True)
            h_ref[r0:r0 + rows, :] = (x * lax.rsqrt(ms + EPS) * g_ref[...]).astype(BF16)

    o_ref[...] = jnp.dot(h_ref[...], w_ref[...], preferred_element_type=F32).astype(o_ref.dtype)


def _inproj(x2d, gain, w_bf16, tm, tn):
    T = x2d.shape[0]
    return pl.pallas_call(
        functools.partial(_inproj_kernel, rows=min(256, tm)),
        grid=(T // tm, D_INP // tn),
        in_specs=[
            pl.BlockSpec((tm, D_MODEL), lambda i, j: (i, 0)),
            pl.BlockSpec((1, D_MODEL), lambda i, j: (0, 0)),
            pl.BlockSpec((D_MODEL, tn), lambda i, j: (0, j)),
        ],
        out_specs=pl.BlockSpec((tm, tn), lambda i, j: (i, j)),
        out_shape=jax.ShapeDtypeStruct((T, D_INP), BF16),
        scratch_shapes=[pltpu.VMEM((tm, D_MODEL), BF16)],
        compiler_params=_params(("parallel", "arbitrary")),
        name="inproj",
    )(x2d, gain, w_bf16)


def _latent_kernel(cq_ref, ckv_ref, kr_ref, qg_ref, kvg_ref, wq_ref, wkv_ref, cos_ref, sin_ref,
                   q_ref, k_ref, v_ref, *, q_scale):
    def rms(c_ref, gain_ref):
        c = c_ref[...].astype(F32)
        ms = jnp.mean(c * c, axis=-1, keepdims=True)
        return (c * lax.rsqrt(ms + EPS) * gain_ref[...]).astype(BF16)

    cos = cos_ref[...]
    sin = sin_ref[...]

    def rotary(g):
        return g * cos + pltpu.roll(g, D_ROPE, 1) * sin

    qf = jnp.dot(rms(cq_ref, qg_ref), wq_ref[...], preferred_element_type=F32)
    kvf = jnp.dot(rms(ckv_ref, kvg_ref), wkv_ref[...], preferred_element_type=F32)
    k_rot = rotary(kr_ref[...].astype(F32)).astype(BF16)
    for h in range(B_HEADS):
        c0 = h * QK_DIM
        q_ref[h, 0:LANES, :] = (qf[:, c0:c0 + LANES] * q_scale).T.astype(BF16)
        q_ref[h, LANES:QK_DIM, :] = (rotary(qf[:, c0 + LANES:c0 + QK_DIM]) * q_scale).T.astype(BF16)
        k_ref[h, :, 0:LANES] = kvf[:, c0:c0 + LANES].astype(BF16)
        k_ref[h, :, LANES:QK_DIM] = k_rot
        v_ref[h, :, :] = kvf[:, c0 + LANES:c0 + QK_DIM].T.astype(BF16)


def _latent(proj, q_gain, kv_gain, wq_ext, wkv, cos_tab, sin_tab, B, S, tm):
    q_scale = (D_NOPE + D_ROPE) ** -0.5 * math.log2(math.e)
    ns = S // tm
    col = lambda g: (lambda b, s: (b, s, g))
    row = lambda b, s: (0, 0)
    return pl.pallas_call(
        functools.partial(_latent_kernel, q_scale=q_scale),
        grid=(B, ns),
        in_specs=[
            pl.BlockSpec((None, tm, Q_LORA), col(G_BCQ * LANES // Q_LORA)),
            pl.BlockSpec((None, tm, KV_LORA), col(G_BCKV * LANES // KV_LORA)),
            pl.BlockSpec((None, tm, LANES), col(G_KR)),
            pl.BlockSpec((1, Q_LORA), row),
            pl.BlockSpec((1, KV_LORA), row),
            pl.BlockSpec((Q_LORA, B_HEADS * QK_DIM), row),
            pl.BlockSpec((KV_LORA, B_HEADS * QK_DIM), row),
            pl.BlockSpec((tm, LANES), lambda b, s: (s, 0)),
            pl.BlockSpec((tm, LANES), lambda b, s: (s, 0)),
        ],
        out_specs=[
            pl.BlockSpec((None, B_HEADS, QK_DIM, tm), lambda b, s: (b, 0, 0, s)),
            pl.BlockSpec((None, B_HEADS, tm, QK_DIM), lambda b, s: (b, 0, s, 0)),
            pl.BlockSpec((None, B_HEADS, D_V, tm), lambda b, s: (b, 0, 0, s)),
        ],
        out_shape=[
            jax.ShapeDtypeStruct((B, B_HEADS, QK_DIM, S), BF16),
            jax.ShapeDtypeStruct((B, B_HEADS, S, QK_DIM), BF16),
            jax.ShapeDtypeStruct((B, B_HEADS, D_V, S), BF16),
        ],
        compiler_params=_params(("parallel", "parallel")),
        name="latent_proj",
    )(proj, proj, proj, q_gain, kv_gain, wq_ext, wkv, cos_tab, sin_tab)


def _mla_kernel(q_ref, k_ref, vt_ref, g_ref, o_ref, s_ref, mrun_ref, mfin_ref, *, nk, tk):
    SUB = 8

    def fold(x, op):
        out = x[0:SUB]
        for t in range(1, x.shape[0] // SUB):
            out = op(out, x[t * SUB:(t + 1) * SUB])
        return out

    @pl.when(pl.program_id(0) == 0)
    def _():
        s_ref[...] = jnp.zeros(s_ref.shape, F32)
        mfin_ref[...] = jnp.zeros(mfin_ref.shape, F32)

    qt = q_ref[...]
    m_prev = mfin_ref[...]
    l_run = None
    acc = None
    for j in range(nk):
        ks = slice(j * tk, (j + 1) * tk)
        p = jnp.exp2(s_ref[ks, :] - m_prev)
        l_blk = fold(p, jnp.add)
        l_run = l_blk if l_run is None else l_run + l_blk
        pv = jnp.dot(vt_ref[:, ks], p.astype(BF16), preferred_element_type=F32)
        acc = pv if acc is None else acc + pv
        s = jnp.dot(k_ref[ks, :], qt, preferred_element_type=F32)
        s_ref[ks, :] = s
        m_blk = fold(s, jnp.maximum)
        mrun_ref[...] = m_blk if j == 0 else jnp.maximum(mrun_ref[...], m_blk)
    mfin_ref[...] = jnp.max(mrun_ref[...], axis=0, keepdims=True)
    o = (acc / jnp.sum(l_run, axis=0, keepdims=True)).T
    o_ref[...] = (o * _silu(g_ref[...].astype(F32))).astype(o_ref.dtype)


def _mla(q, k, v, proj, B, S, tq, tk):
    nq = S // tq
    n_tiles = B * B_HEADS * nq

    def tile(n):
        return n // (B_HEADS * nq), (n // nq) % B_HEADS, n % nq

    def cur(n):
        return tile(jnp.minimum(n, n_tiles - 1))

    def prev(n):
        return tile(jnp.maximum(n - 1, 0))

    def q_map(n):
        b, h, i = cur(n)
        return b, h, 0, i

    def k_map(n):
        b, h, _ = cur(n)
        return b, h, 0, 0

    def v_map(n):
        b, h, _ = prev(n)
        return b, h, 0, 0

    def g_map(n):
        b, h, i = prev(n)
        return b, i, G_BG + h

    def o_map(n):
        b, h, i = prev(n)
        return b, i, h

    return pl.pallas_call(
        functools.partial(_mla_kernel, nk=S // tk, tk=tk),
        grid=(n_tiles + 1,),
        in_specs=[
            pl.BlockSpec((None, None, QK_DIM, tq), q_map),
            pl.BlockSpec((None, None, S, QK_DIM), k_map),
            pl.BlockSpec((None, None, D_V, S), v_map),
            pl.BlockSpec((None, tq, LANES), g_map),
        ],
        out_specs=pl.BlockSpec((None, tq, D_V), o_map),
        out_shape=jax.ShapeDtypeStruct((B, S, B_WIDTH), BF16),
        scratch_shapes=[pltpu.VMEM((S, tq), F32), pltpu.VMEM((8, tq), F32), pltpu.VMEM((1, tq), F32)],
        compiler_params=_params(("arbitrary",)),
        name="latent_attn",
    )(q, k, v, proj)


def _t5_bucket(rel):
    nb = N_BUCKETS // 2
    max_exact = nb // 2
    bucket = jnp.where(rel > 0, nb, 0)
    n = jnp.abs(rel)
    nf = jnp.maximum(n, 1).astype(F32)
    large = max_exact + (jnp.log(nf / max_exact) / math.log(MAX_DISTANCE / max_exact)
                         * (nb - max_exact)).astype(jnp.int32)
    large = jnp.minimum(large, nb - 1)
    return bucket + jnp.where(n < max_exact, n, large)


def _dil_bias_tiles(rel_bias, g, dil, tq, tk):
    qi = jnp.arange(tq)[:, None]
    kj = jnp.arange(tk)[None, :]
    table = rel_bias.astype(F32)[:, g * A_HEADS:(g + 1) * A_HEADS]
    rel = jnp.stack([kj - qi - off for off in (0, DIL_HALF, 2 * DIL_HALF)])
    bucket = _t5_bucket(rel * dil)
    b = jnp.zeros((3, A_HEADS, tq, tk), F32)
    for i in range(N_BUCKETS):
        b = jnp.where((bucket == i)[:, None], table[i][None, :, None, None], b)
    return jnp.where((jnp.abs(rel) <= DIL_HALF)[:, None], b, NEG)


def _dil_kernel(q_ref, k_ref, v_ref, gate_ref, bias_ref, o_ref, q32_ref, k32_ref, v32_ref, acc_ref, lse_ref,
                *, S, tq, tk, unroll):
    g = pl.program_id(2)
    scale = A_HD ** -0.5

    def attend(q, k, v, bias):
        s = lax.dot_general(q, k, _NT, preferred_element_type=F32) * scale + bias
        m = jnp.max(s, axis=1, keepdims=True)
        p = jnp.exp(s - m)
        den = jnp.sum(p, axis=1, keepdims=True)
        o = jnp.dot(p.astype(BF16), v, preferred_element_type=F32) / den
        return o, jnp.broadcast_to(m + jnp.log(den), (tq, LANES))

    def window(n, nb, L):
        q0 = pl.multiple_of(n * tq, tq)
        ks = pl.multiple_of(jnp.clip(q0 - DIL_HALF, 0, L - tk), DIL_HALF)
        placement = jnp.where(n == 0, 0, jnp.where(n == nb - 1, 2, 1))
        return q0, ks, placement

    @pl.when(g == 0)
    def _():
        k32_ref[...] = k_ref[...].astype(F32)
        v32_ref[...] = v_ref[...].astype(F32)
        nb = S // tq

        def body(n, carry):
            q0, ks, placement = window(n, nb, S)
            o, lse = attend(q_ref[pl.ds(q0, tq), :], k_ref[pl.ds(ks, tk), :], v_ref[pl.ds(ks, tk), :],
                            bias_ref[placement])
            acc_ref[pl.ds(q0, tq), :] = o
            lse_ref[pl.ds(q0, tq), :] = lse
            return carry

        lax.fori_loop(0, nb, body, 0, unroll=min(unroll, nb))

    def strided_group(dil):
        L = S // dil
        nb = L // tq
        nc = L // DIL_HALF
        wch = tk // DIL_HALF
        qch = tq // DIL_HALF
        q32_ref[...] = q_ref[...].astype(F32)

        def stream(r, carry):
            def chunks(ref):
                return [ref[pl.ds(r + c * DIL_HALF * dil, DIL_HALF, stride=dil), :].astype(BF16)
                        for c in range(nc)]

            kc, vc = chunks(k32_ref), chunks(v32_ref)
            for n in range(nb):
                c0 = min(max(n * qch - 1, 0), nc - wch)
                placement = 0 if n == 0 else (2 if n == nb - 1 else 1)
                q_rows = pl.ds(r + n * tq * dil, tq, stride=dil)
                o, lse = attend(q32_ref[q_rows, :].astype(BF16), jnp.concatenate(kc[c0:c0 + wch], axis=0),
                                jnp.concatenate(vc[c0:c0 + wch], axis=0), bias_ref[placement])
                lse_old = lse_ref[q_rows, :]
                m2 = jnp.maximum(lse_old, lse)
                e_old = jnp.exp(lse_old - m2)
                e_new = jnp.exp(lse - m2)
                den = e_old + e_new
                acc_ref[q_rows, :] = (acc_ref[q_rows, :] * e_old + o * e_new) / den
                lse_ref[q_rows, :] = m2 + jnp.log(den)
            return carry

        lax.fori_loop(0, dil, stream, 0, unroll=max(1, min(unroll // nb, dil)))

    for gi in range(1, N_DIL):
        pl.when(g == gi)(functools.partial(strided_group, DIL_PAIRS[gi][1]))

    @pl.when(g == N_DIL - 1)
    def _():
        o_ref[...] = (acc_ref[...] * _silu(gate_ref[...].astype(F32))).astype(o_ref.dtype)


def _dilated(proj, bias, B, S, tq):
    tk = tq + 2 * DIL_HALF
    for _, dil in DIL_PAIRS:
        nb = S // dil // tq
        assert nb >= 2 and nb * tq * dil == S, (S, dil, tq)
    q_group = lambda g: G_AQ[1] * g + (G_AQ[2] - 2 * G_AQ[1]) * (g // 2)
    col = lambda g0: (lambda b, h, g: (b, 0, g0 + h))
    return pl.pallas_call(
        functools.partial(_dil_kernel, S=S, tq=tq, tk=tk, unroll=DIL_UNROLL),
        grid=(B, A_HEADS, N_DIL),
        in_specs=[
            pl.BlockSpec((None, S, LANES), lambda b, h, g: (b, 0, q_group(g) + h)),
            pl.BlockSpec((None, S, LANES), col(G_AK)),
            pl.BlockSpec((None, S, LANES), col(G_AV)),
            pl.BlockSpec((None, S, LANES), col(G_AG)),
            pl.BlockSpec((None, 3, None, tq, tk), lambda b, h, g: (g, 0, h, 0, 0)),
        ],
        out_specs=pl.BlockSpec((None, S, LANES), lambda b, h, g: (b, 0, h)),
        out_shape=jax.ShapeDtypeStruct((B, S, A_WIDTH), BF16),
        scratch_shapes=[pltpu.VMEM((S, LANES), F32) for _ in range(5)],
        compiler_params=_params(("parallel", "arbitrary", "arbitrary")),
        name="dilated_attn",
    )(proj, proj, proj, proj, bias)


def _memkv_kernel(mem_ref, g_ref, w_ref, o_ref):
    x = mem_ref[...]
    ms = jnp.mean(x * x, axis=-1, keepdims=True)
    hn = (x * lax.rsqrt(ms + EPS) * g_ref[...]).astype(BF16)
    o_ref[...] = jnp.dot(hn, w_ref[...], preferred_element_type=F32).astype(o_ref.dtype)


def _memkv(mem, gain, w_bf16):
    B, M, _ = mem.shape
    return pl.pallas_call(
        _memkv_kernel,
        grid=(B,),
        in_specs=[
            pl.BlockSpec((None, M, D_MODEL), lambda b: (b, 0, 0)),
            pl.BlockSpec((1, D_MODEL), lambda b: (0, 0)),
            pl.BlockSpec((D_MODEL, 2 * C_WIDTH), lambda b: (0, 0)),
        ],
        out_specs=pl.BlockSpec((None, M, 2 * C_WIDTH), lambda b: (b, 0, 0)),
        out_shape=jax.ShapeDtypeStruct((B, M, 2 * C_WIDTH), BF16),
        compiler_params=_params(("parallel",)),
        name="mem_kv_proj",
    )(mem, gain, w_bf16)


def _out_kernel(ya_ref, yb_ref, cq_ref, cg_ref, mkv_ref, w_ref, pg_ref, x_ref, o_ref, y_ref):
    n = pl.program_id(0)
    cur = lax.rem(n, 2)
    prv = 1 - cur

    @pl.when(n == 0)
    def _():
        y_ref[1] = jnp.zeros(y_ref.shape[1:], BF16)

    y = jnp.dot(y_ref[prv], w_ref[...], preferred_element_type=F32)
    ms = jnp.mean(y * y, axis=-1, keepdims=True)
    o_ref[...] = x_ref[...] + y * lax.rsqrt(ms + EPS) * pg_ref[...]

    y_ref[cur, :, 0:A_WIDTH] = ya_ref[...]
    y_ref[cur, :, A_WIDTH:A_WIDTH + B_WIDTH] = yb_ref[...]
    scale = C_HD ** -0.5
    for h in range(C_HEADS):
        cs = slice(h * C_HD, (h + 1) * C_HD)
        s = lax.dot_general(cq_ref[:, cs], mkv_ref[:, cs], _NT, preferred_element_type=F32) * scale
        m = jnp.max(s, axis=1, keepdims=True)
        p = jnp.exp(s - m)
        den = jnp.sum(p, axis=1, keepdims=True)
        mv = mkv_ref[:, C_WIDTH + h * C_HD:C_WIDTH + (h + 1) * C_HD]
        oc = jnp.dot(p.astype(BF16), mv, preferred_element_type=F32) / den
        c0 = A_WIDTH + B_WIDTH + h * C_HD
        y_ref[cur, :, c0:c0 + C_HD] = (oc * _silu(cg_ref[:, cs].astype(F32))).astype(BF16)


def _outproj(ya, yb, proj, mkv, w_out, post_gain, x, tm):
    B, S, _ = x.shape
    M = mkv.shape[1]
    ns = S // tm
    n_tiles = B * ns

    def cur(n):
        t = jnp.minimum(n, n_tiles - 1)
        return t // ns, t % ns

    def prev(n):
        t = jnp.maximum(n - 1, 0)
        return t // ns, t % ns

    def tok(width, blk, which):
        return pl.BlockSpec((None, tm, width), lambda n: (*which(n), blk))

    const = lambda shape: pl.BlockSpec(shape, lambda n: (0,) * len(shape))
    return pl.pallas_call(
        _out_kernel,
        grid=(n_tiles + 1,),
        in_specs=[
            tok(A_WIDTH, 0, cur),
            tok(B_WIDTH, 0, cur),
            tok(C_WIDTH, G_CQ * LANES // C_WIDTH, cur),
            tok(C_WIDTH, G_CG * LANES // C_WIDTH, cur),
            pl.BlockSpec((None, M, 2 * C_WIDTH), lambda n: (cur(n)[0], 0, 0)),
            const((D_MIX, D_MODEL)),
            const((1, D_MODEL)),
            tok(D_MODEL, 0, prev),
        ],
        out_specs=tok(D_MODEL, 0, prev),
        out_shape=jax.ShapeDtypeStruct((B, S, D_MODEL), x.dtype),
        scratch_shapes=[pltpu.VMEM((2, tm, D_MIX), BF16)],
        compiler_params=_params(("arbitrary",)),
        name="outproj",
    )(ya, yb, proj, proj, mkv, w_out, post_gain, x)


def _swap_halves(w):
    half = D_ROPE // 2
    return jnp.concatenate([w[..., half:], w[..., :half]], axis=-1)


def _prep_weights(w_in, w_uq, w_ukv, w_mem_kv, w_out):
    o_kr = N_DIL * A_WIDTH + 3 * A_WIDTH + Q_LORA + KV_LORA
    o_bg = o_kr + D_ROPE
    w_in = w_in.astype(BF16)
    kr = w_in[:, o_kr:o_bg]
    w_in_r = jnp.concatenate(
        [w_in[:, :o_kr], w_in[:, o_bg:], kr, _swap_halves(kr),
         jnp.zeros((D_MODEL, D_INP - (G_KR + 1) * LANES), BF16)], axis=1)
    wq = w_uq.reshape(Q_LORA, B_HEADS, D_NOPE + D_ROPE)
    rope = wq[:, :, D_NOPE:]
    wq_ext = jnp.concatenate([wq[:, :, :D_NOPE], rope, _swap_halves(rope)], axis=-1)
    wq_ext = wq_ext.reshape(Q_LORA, B_HEADS * QK_DIM).astype(BF16)
    return w_in_r, wq_ext, w_ukv.astype(BF16), w_mem_kv.astype(BF16), w_out.astype(BF16)


def _rope_tables(S):
    inv = ROPE_BASE ** (-jnp.arange(0, D_ROPE, 2, dtype=F32) / D_ROPE)
    ang = jnp.arange(S, dtype=F32)[:, None] * inv[None, :]
    cos, sin = jnp.cos(ang), jnp.sin(ang)
    zeros = jnp.zeros((S, LANES - D_ROPE), F32)
    return (jnp.concatenate([cos, cos, zeros], axis=1),
            jnp.concatenate([-sin, sin, zeros], axis=1))


def _dil_biases(rel_bias, tq):
    return jnp.stack([_dil_bias_tiles(rel_bias, g, dil, tq, tq + 2 * DIL_HALF)
                      for g, (_, dil) in enumerate(DIL_PAIRS)])


def _layer(x, mem, pre_gain, q_gain, kv_gain, mem_gain, post_gain, biases, weights):
    w_in_r, wq_ext, wkv, wmem, wout = weights
    B, S, _ = x.shape
    t = _tiles(S)
    proj = _inproj(x.reshape(B * S, D_MODEL), pre_gain, w_in_r, t["tm_in"], t["tn_in"])
    proj = proj.reshape(B, S, D_INP)
    cos_tab, sin_tab = _rope_tables(S)
    q, k, v = _latent(proj, q_gain, kv_gain, wq_ext, wkv, cos_tab, sin_tab, B, S, t["tm_lat"])
    yb = _mla(q, k, v, proj, B, S, t["tq_mla"], t["tk_mla"])
    ya = _dilated(proj, biases, B, S, t["tq_dil"])
    mkv = _memkv(mem, mem_gain, wmem)
    return _outproj(ya, yb, proj, mkv, wout, post_gain, x, t["tm_out"])


def kernel(x_prompt, x_sample, mem_prompt, mem_sample, pre_gain, w_in, q_gain, w_uq, kv_gain, w_ukv,
           mem_gain, w_mem_kv, w_out, post_gain, rel_bias):
    depth = pre_gain.shape[0]
    weights = [_prep_weights(w_in[l], w_uq[l], w_ukv[l], w_mem_kv[l], w_out[l]) for l in range(depth)]
    biases = _dil_biases(rel_bias, _tiles(x_prompt.shape[1])["tq_dil"])

    def trunk(x, mem):
        for l in range(depth):
            x = _layer(x, mem, pre_gain[l][None], q_gain[l][None], kv_gain[l][None], mem_gain[l][None],
                       post_gain[l][None], biases, weights[l])
        return x

    return trunk(x_prompt, mem_prompt), trunk(x_sample, mem_sample)
```

```python
import functools
import math

import jax
import jax.numpy as jnp
from jax import lax
from jax.experimental import pallas as pl
from jax.experimental.pallas import tpu as pltpu

D_MODEL = 2048
DIL_PAIRS = ((128, 1), (512, 4), (2048, 16))
N_DIL = 3
DIL_HALF = 64
A_HEADS = 4
A_HD = 128
A_WIDTH = A_HEADS * A_HD
B_HEADS = 8
Q_LORA = 512
KV_LORA = 512
D_NOPE = 128
D_ROPE = 64
D_V = 128
B_WIDTH = B_HEADS * D_V
ROPE_BASE = 10000.0
C_HEADS = 4
C_HD = 128
C_WIDTH = C_HEADS * C_HD
D_MIX = A_WIDTH + B_WIDTH + C_WIDTH
N_BUCKETS = 32
MAX_DISTANCE = 1024
EPS = 1e-6
NEG = -1e30

LANES = 128
SUBLANES = 8
VMEM_LIMIT_BYTES = 56 * 1024 * 1024

G_AQ = (0, 4, 8)
G_AK = 12
G_AV = 16
G_AG = 20
G_BCQ = 24
G_BCKV = 28
G_BG = 32
G_CQ = 40
G_CG = 44
G_KR = 48
N_GROUPS = 50
MXU_COLS = 2 * LANES
D_INP = N_GROUPS * LANES
QK_DIM = 2 * LANES

F32 = jnp.float32
BF16 = jnp.bfloat16
_NT = (((1,), (1,)), ((), ()))


def _tiles(S):
    return dict(
        tm_in=min(1024, S),
        rows_in=min(256, S),
        tn_in=5 * MXU_COLS,
        tm_lat=min(1024, S),
        tq_mla=min(512, S),
        tk_mla=min(512, S),
        tq_dil=128,
        unroll_dil=16,
        tm_out=min(512, S),
    )


def _params(sem):
    return pltpu.CompilerParams(dimension_semantics=sem, vmem_limit_bytes=VMEM_LIMIT_BYTES)


def _silu(g):
    return g * (1.0 / (1.0 + jnp.exp(-g)))


def _inproj_kernel(x_ref, g_ref, w_ref, o_ref, h_ref, *, rows):
    @pl.when(pl.program_id(1) == 0)
    def _():
        tm = x_ref.shape[0]
        for r0 in range(0, tm, rows):
            x = x_ref[r0:r0 + rows, :]
            ms = jnp.mean(x * x, axis=-1, keepdims=True)
            h_ref[r0:r0 + rows, :] = (x * lax.rsqrt(ms + EPS) * g_ref[...]).astype(BF16)

    o_ref[...] = jnp.dot(h_ref[...], w_ref[...], preferred_element_type=F32).astype(o_ref.dtype)


def _inproj(x2d, gain, w_bf16, tm, tn, rows):
    T = x2d.shape[0]
    return pl.pallas_call(
        functools.partial(_inproj_kernel, rows=rows),
        grid=(T // tm, D_INP // tn),
        in_specs=[
            pl.BlockSpec((tm, D_MODEL), lambda i, j: (i, 0)),
            pl.BlockSpec((1, D_MODEL), lambda i, j: (0, 0)),
            pl.BlockSpec((D_MODEL, tn), lambda i, j: (0, j)),
        ],
        out_specs=pl.BlockSpec((tm, tn), lambda i, j: (i, j)),
        out_shape=jax.ShapeDtypeStruct((T, D_INP), BF16),
        scratch_shapes=[pltpu.VMEM((tm, D_MODEL), BF16)],
        compiler_params=_params(("parallel", "arbitrary")),
        name="inproj",
    )(x2d, gain, w_bf16)


def _latent_kernel(cq_ref, ckv_ref, kr_ref, qg_ref, kvg_ref, wq_ref, wkv_ref, cos_ref, sin_ref,
                   q_ref, k_ref, v_ref, *, q_scale):
    def rms(c_ref, gain_ref):
        c = c_ref[...].astype(F32)
        ms = jnp.mean(c * c, axis=-1, keepdims=True)
        return (c * lax.rsqrt(ms + EPS) * gain_ref[...]).astype(BF16)

    cos = cos_ref[...]
    sin = sin_ref[...]

    def rotary(g):
        return g * cos + pltpu.roll(g, D_ROPE, 1) * sin

    qf = jnp.dot(rms(cq_ref, qg_ref), wq_ref[...], preferred_element_type=F32)
    kvf = jnp.dot(rms(ckv_ref, kvg_ref), wkv_ref[...], preferred_element_type=F32)
    k_rot = rotary(kr_ref[...].astype(F32)).astype(BF16)
    for h in range(B_HEADS):
        c0 = h * QK_DIM
        q_ref[h, 0:LANES, :] = (qf[:, c0:c0 + LANES] * q_scale).T.astype(BF16)
        q_ref[h, LANES:QK_DIM, :] = (rotary(qf[:, c0 + LANES:c0 + QK_DIM]) * q_scale).T.astype(BF16)
        k_ref[h, :, 0:LANES] = kvf[:, c0:c0 + LANES].astype(BF16)
        k_ref[h, :, LANES:QK_DIM] = k_rot
        v_ref[h, :, :] = kvf[:, c0 + LANES:c0 + QK_DIM].T.astype(BF16)


def _latent(proj, q_gain, kv_gain, wq_ext, wkv, cos_tab, sin_tab, B, S, tm):
    q_scale = (D_NOPE + D_ROPE) ** -0.5 * math.log2(math.e)
    ns = S // tm
    col = lambda g: (lambda b, s: (b, s, g))
    row = lambda b, s: (0, 0)
    return pl.pallas_call(
        functools.partial(_latent_kernel, q_scale=q_scale),
        grid=(B, ns),
        in_specs=[
            pl.BlockSpec((None, tm, Q_LORA), col(G_BCQ * LANES // Q_LORA)),
            pl.BlockSpec((None, tm, KV_LORA), col(G_BCKV * LANES // KV_LORA)),
            pl.BlockSpec((None, tm, LANES), col(G_KR)),
            pl.BlockSpec((1, Q_LORA), row),
            pl.BlockSpec((1, KV_LORA), row),
            pl.BlockSpec((Q_LORA, B_HEADS * QK_DIM), row),
            pl.BlockSpec((KV_LORA, B_HEADS * QK_DIM), row),
            pl.BlockSpec((tm, LANES), lambda b, s: (s, 0)),
            pl.BlockSpec((tm, LANES), lambda b, s: (s, 0)),
        ],
        out_specs=[
            pl.BlockSpec((None, B_HEADS, QK_DIM, tm), lambda b, s: (b, 0, 0, s)),
            pl.BlockSpec((None, B_HEADS, tm, QK_DIM), lambda b, s: (b, 0, s, 0)),
            pl.BlockSpec((None, B_HEADS, D_V, tm), lambda b, s: (b, 0, 0, s)),
        ],
        out_shape=[
            jax.ShapeDtypeStruct((B, B_HEADS, QK_DIM, S), BF16),
            jax.ShapeDtypeStruct((B, B_HEADS, S, QK_DIM), BF16),
            jax.ShapeDtypeStruct((B, B_HEADS, D_V, S), BF16),
        ],
        compiler_params=_params(("parallel", "parallel")),
        name="latent_proj",
    )(proj, proj, proj, q_gain, kv_gain, wq_ext, wkv, cos_tab, sin_tab)


def _mla_kernel(q_ref, k_ref, vt_ref, g_ref, o_ref, s_ref, mrun_ref, mfin_ref, *, nk, tk):
    def fold(x, op):
        out = x[0:SUBLANES]
        for t in range(1, x.shape[0] // SUBLANES):
            out = op(out, x[t * SUBLANES:(t + 1) * SUBLANES])
        return out

    @pl.when(pl.program_id(0) == 0)
    def _():
        s_ref[...] = jnp.zeros(s_ref.shape, F32)
        mfin_ref[...] = jnp.zeros(mfin_ref.shape, F32)

    qt = q_ref[...]
    m_prev = mfin_ref[...]
    l_run = None
    acc = None
    for j in range(nk):
        ks = slice(j * tk, (j + 1) * tk)
        p = jnp.exp2(s_ref[ks, :] - m_prev)
        l_blk = fold(p, jnp.add)
        l_run = l_blk if l_run is None else l_run + l_blk
        pv = jnp.dot(vt_ref[:, ks], p.astype(BF16), preferred_element_type=F32)
        acc = pv if acc is None else acc + pv
        s = jnp.dot(k_ref[ks, :], qt, preferred_element_type=F32)
        s_ref[ks, :] = s
        m_blk = fold(s, jnp.maximum)
        mrun_ref[...] = m_blk if j == 0 else jnp.maximum(mrun_ref[...], m_blk)
    mfin_ref[...] = jnp.max(mrun_ref[...], axis=0, keepdims=True)
    o = (acc / jnp.sum(l_run, axis=0, keepdims=True)).T
    o_ref[...] = (o * _silu(g_ref[...].astype(F32))).astype(o_ref.dtype)


def _mla(q, k, v, proj, B, S, tq, tk):
    nq = S // tq
    n_tiles = B * B_HEADS * nq

    def tile(n):
        return n // (B_HEADS * nq), (n // nq) % B_HEADS, n % nq

    def cur(n):
        return tile(jnp.minimum(n, n_tiles - 1))

    def prev(n):
        return tile(jnp.maximum(n - 1, 0))

    def q_map(n):
        b, h, i = cur(n)
        return b, h, 0, i

    def k_map(n):
        b, h, _ = cur(n)
        return b, h, 0, 0

    def v_map(n):
        b, h, _ = prev(n)
        return b, h, 0, 0

    def g_map(n):
        b, h, i = prev(n)
        return b, i, G_BG + h

    def o_map(n):
        b, h, i = prev(n)
        return b, i, h

    return pl.pallas_call(
        functools.partial(_mla_kernel, nk=S // tk, tk=tk),
        grid=(n_tiles + 1,),
        in_specs=[
            pl.BlockSpec((None, None, QK_DIM, tq), q_map),
            pl.BlockSpec((None, None, S, QK_DIM), k_map),
            pl.BlockSpec((None, None, D_V, S), v_map),
            pl.BlockSpec((None, tq, LANES), g_map),
        ],
        out_specs=pl.BlockSpec((None, tq, D_V), o_map),
        out_shape=jax.ShapeDtypeStruct((B, S, B_WIDTH), BF16),
        scratch_shapes=[pltpu.VMEM((S, tq), F32), pltpu.VMEM((SUBLANES, tq), F32), pltpu.VMEM((1, tq), F32)],
        compiler_params=_params(("arbitrary",)),
        name="latent_attn",
    )(q, k, v, proj)


def _t5_bucket(rel):
    nb = N_BUCKETS // 2
    max_exact = nb // 2
    bucket = jnp.where(rel > 0, nb, 0)
    n = jnp.abs(rel)
    nf = jnp.maximum(n, 1).astype(F32)
    large = max_exact + (jnp.log(nf / max_exact) / math.log(MAX_DISTANCE / max_exact)
                         * (nb - max_exact)).astype(jnp.int32)
    large = jnp.minimum(large, nb - 1)
    return bucket + jnp.where(n < max_exact, n, large)


def _dil_bias_tiles(rel_bias, g, dil, tq, tk):
    qi = jnp.arange(tq)[:, None]
    kj = jnp.arange(tk)[None, :]
    table = rel_bias.astype(F32)[:, g * A_HEADS:(g + 1) * A_HEADS]
    rel = jnp.stack([kj - qi - off for off in (0, DIL_HALF, 2 * DIL_HALF)])
    bucket = _t5_bucket(rel * dil)
    b = jnp.zeros((3, A_HEADS, tq, tk), F32)
    for i in range(N_BUCKETS):
        b = jnp.where((bucket == i)[:, None], table[i][None, :, None, None], b)
    return jnp.where((jnp.abs(rel) <= DIL_HALF)[:, None], b, NEG)


def _dil_kernel(q_ref, k_ref, v_ref, gate_ref, bias_ref, o_ref, q32_ref, k32_ref, v32_ref, acc_ref, lse_ref,
                *, S, tq, tk, unroll):
    g = pl.program_id(2)
    scale = A_HD ** -0.5

    def attend(q, k, v, bias):
        s = lax.dot_general(q, k, _NT, preferred_element_type=F32) * scale + bias
        m = jnp.max(s, axis=1, keepdims=True)
        p = jnp.exp(s - m)
        den = jnp.sum(p, axis=1, keepdims=True)
        o = jnp.dot(p.astype(BF16), v, preferred_element_type=F32) / den
        return o, jnp.broadcast_to(m + jnp.log(den), (tq, LANES))

    def window(n, nb, L):
        q0 = pl.multiple_of(n * tq, tq)
        ks = pl.multiple_of(jnp.clip(q0 - DIL_HALF, 0, L - tk), DIL_HALF)
        placement = jnp.where(n == 0, 0, jnp.where(n == nb - 1, 2, 1))
        return q0, ks, placement

    @pl.when(g == 0)
    def _():
        k32_ref[...] = k_ref[...].astype(F32)
        v32_ref[...] = v_ref[...].astype(F32)
        nb = S // tq

        def body(n, carry):
            q0, ks, placement = window(n, nb, S)
            o, lse = attend(q_ref[pl.ds(q0, tq), :], k_ref[pl.ds(ks, tk), :], v_ref[pl.ds(ks, tk), :],
                            bias_ref[placement])
            acc_ref[pl.ds(q0, tq), :] = o
            lse_ref[pl.ds(q0, tq), :] = lse
            return carry

        lax.fori_loop(0, nb, body, 0, unroll=min(unroll, nb))

    def strided_group(dil):
        L = S // dil
        nb = L // tq
        nc = L // DIL_HALF
        wch = tk // DIL_HALF
        qch = tq // DIL_HALF
        q32_ref[...] = q_ref[...].astype(F32)

        def stream(r, carry):
            def chunks(ref):
                return [ref[pl.ds(r + c * DIL_HALF * dil, DIL_HALF, stride=dil), :].astype(BF16)
                        for c in range(nc)]

            kc, vc = chunks(k32_ref), chunks(v32_ref)
            for n in range(nb):
                c0 = min(max(n * qch - 1, 0), nc - wch)
                placement = 0 if n == 0 else (2 if n == nb - 1 else 1)
                q_rows = pl.ds(r + n * tq * dil, tq, stride=dil)
                o, lse = attend(q32_ref[q_rows, :].astype(BF16), jnp.concatenate(kc[c0:c0 + wch], axis=0),
                                jnp.concatenate(vc[c0:c0 + wch], axis=0), bias_ref[placement])
                lse_old = lse_ref[q_rows, :]
                m2 = jnp.maximum(lse_old, lse)
                e_old = jnp.exp(lse_old - m2)
                e_new = jnp.exp(lse - m2)
                den = e_old + e_new
                acc_ref[q_rows, :] = (acc_ref[q_rows, :] * e_old + o * e_new) / den
                lse_ref[q_rows, :] = m2 + jnp.log(den)
            return carry

        lax.fori_loop(0, dil, stream, 0, unroll=max(1, min(unroll // nb, dil)))

    for gi in range(1, N_DIL):
        pl.when(g == gi)(functools.partial(strided_group, DIL_PAIRS[gi][1]))

    @pl.when(g == N_DIL - 1)
    def _():
        o_ref[...] = (acc_ref[...] * _silu(gate_ref[...].astype(F32))).astype(o_ref.dtype)


def _dilated(proj, bias, B, S, tq, unroll):
    tk = tq + 2 * DIL_HALF
    for _, dil in DIL_PAIRS:
        nb = S // dil // tq
        assert nb >= 2 and nb * tq * dil == S, (S, dil, tq)
    q_group = lambda g: G_AQ[1] * g + (G_AQ[2] - 2 * G_AQ[1]) * (g // 2)
    col = lambda g0: (lambda b, h, g: (b, 0, g0 + h))
    return pl.pallas_call(
        functools.partial(_dil_kernel, S=S, tq=tq, tk=tk, unroll=unroll),
        grid=(B, A_HEADS, N_DIL),
        in_specs=[
            pl.BlockSpec((None, S, LANES), lambda b, h, g: (b, 0, q_group(g) + h)),
            pl.BlockSpec((None, S, LANES), col(G_AK)),
            pl.BlockSpec((None, S, LANES), col(G_AV)),
            pl.BlockSpec((None, S, LANES), col(G_AG)),
            pl.BlockSpec((None, 3, None, tq, tk), lambda b, h, g: (g, 0, h, 0, 0)),
        ],
        out_specs=pl.BlockSpec((None, S, LANES), lambda b, h, g: (b, 0, h)),
        out_shape=jax.ShapeDtypeStruct((B, S, A_WIDTH), BF16),
        scratch_shapes=[pltpu.VMEM((S, LANES), F32) for _ in range(5)],
        compiler_params=_params(("parallel", "arbitrary", "arbitrary")),
        name="dilated_attn",
    )(proj, proj, proj, proj, bias)


def _memkv_kernel(mem_ref, g_ref, w_ref, o_ref):
    x = mem_ref[...]
    ms = jnp.mean(x * x, axis=-1, keepdims=True)
    hn = (x * lax.rsqrt(ms + EPS) * g_ref[...]).astype(BF16)
    o_ref[...] = jnp.dot(hn, w_ref[...], preferred_element_type=F32).astype(o_ref.dtype)


def _memkv(mem, gain, w_bf16):
    B, M, _ = mem.shape
    return pl.pallas_call(
        _memkv_kernel,
        grid=(B,),
        in_specs=[
            pl.BlockSpec((None, M, D_MODEL), lambda b: (b, 0, 0)),
            pl.BlockSpec((1, D_MODEL), lambda b: (0, 0)),
            pl.BlockSpec((D_MODEL, 2 * C_WIDTH), lambda b: (0, 0)),
        ],
        out_specs=pl.BlockSpec((None, M, 2 * C_WIDTH), lambda b: (b, 0, 0)),
        out_shape=jax.ShapeDtypeStruct((B, M, 2 * C_WIDTH), BF16),
        compiler_params=_params(("parallel",)),
        name="mem_kv_proj",
    )(mem, gain, w_bf16)


def _out_kernel(ya_ref, yb_ref, cq_ref, cg_ref, mkv_ref, w_ref, pg_ref, x_ref, o_ref, y_ref):
    n = pl.program_id(0)
    cur = lax.rem(n, 2)
    prv = 1 - cur

    @pl.when(n == 0)
    def _():
        y_ref[1] = jnp.zeros(y_ref.shape[1:], BF16)

    y = jnp.dot(y_ref[prv], w_ref[...], preferred_element_type=F32)
    ms = jnp.mean(y * y, axis=-1, keepdims=True)
    o_ref[...] = x_ref[...] + y * lax.rsqrt(ms + EPS) * pg_ref[...]

    y_ref[cur, :, 0:A_WIDTH] = ya_ref[...]
    y_ref[cur, :, A_WIDTH:A_WIDTH + B_WIDTH] = yb_ref[...]
    scale = C_HD ** -0.5
    for h in range(C_HEADS):
        cs = slice(h * C_HD, (h + 1) * C_HD)
        s = lax.dot_general(cq_ref[:, cs], mkv_ref[:, cs], _NT, preferred_element_type=F32) * scale
        m = jnp.max(s, axis=1, keepdims=True)
        p = jnp.exp(s - m)
        den = jnp.sum(p, axis=1, keepdims=True)
        mv = mkv_ref[:, C_WIDTH + h * C_HD:C_WIDTH + (h + 1) * C_HD]
        oc = jnp.dot(p.astype(BF16), mv, preferred_element_type=F32) / den
        c0 = A_WIDTH + B_WIDTH + h * C_HD
        y_ref[cur, :, c0:c0 + C_HD] = (oc * _silu(cg_ref[:, cs].astype(F32))).astype(BF16)


def _outproj(ya, yb, proj, mkv, w_out, post_gain, x, tm):
    B, S, _ = x.shape
    M = mkv.shape[1]
    ns = S // tm
    n_tiles = B * ns

    def cur(n):
        t = jnp.minimum(n, n_tiles - 1)
        return t // ns, t % ns

    def prev(n):
        t = jnp.maximum(n - 1, 0)
        return t // ns, t % ns

    def tok(width, blk, which):
        return pl.BlockSpec((None, tm, width), lambda n: (*which(n), blk))

    const = lambda shape: pl.BlockSpec(shape, lambda n: (0,) * len(shape))
    return pl.pallas_call(
        _out_kernel,
        grid=(n_tiles + 1,),
        in_specs=[
            tok(A_WIDTH, 0, cur),
            tok(B_WIDTH, 0, cur),
            tok(C_WIDTH, G_CQ * LANES // C_WIDTH, cur),
            tok(C_WIDTH, G_CG * LANES // C_WIDTH, cur),
            pl.BlockSpec((None, M, 2 * C_WIDTH), lambda n: (cur(n)[0], 0, 0)),
            const((D_MIX, D_MODEL)),
            const((1, D_MODEL)),
            tok(D_MODEL, 0, prev),
        ],
        out_specs=tok(D_MODEL, 0, prev),
        out_shape=jax.ShapeDtypeStruct((B, S, D_MODEL), x.dtype),
        scratch_shapes=[pltpu.VMEM((2, tm, D_MIX), BF16)],
        compiler_params=_params(("arbitrary",)),
        name="outproj",
    )(ya, yb, proj, proj, mkv, w_out, post_gain, x)


def _swap_halves(w):
    half = D_ROPE // 2
    return jnp.concatenate([w[..., half:], w[..., :half]], axis=-1)


def _prep_weights(w_in, w_uq, w_ukv, w_mem_kv, w_out):
    o_kr = N_DIL * A_WIDTH + 3 * A_WIDTH + Q_LORA + KV_LORA
    o_bg = o_kr + D_ROPE
    w_in = w_in.astype(BF16)
    kr = w_in[:, o_kr:o_bg]
    w_in_r = jnp.concatenate(
        [w_in[:, :o_kr], w_in[:, o_bg:], kr, _swap_halves(kr),
         jnp.zeros((D_MODEL, D_INP - (G_KR + 1) * LANES), BF16)], axis=1)
    wq = w_uq.reshape(Q_LORA, B_HEADS, D_NOPE + D_ROPE)
    rope = wq[:, :, D_NOPE:]
    wq_ext = jnp.concatenate([wq[:, :, :D_NOPE], rope, _swap_halves(rope)], axis=-1)
    wq_ext = wq_ext.reshape(Q_LORA, B_HEADS * QK_DIM).astype(BF16)
    return w_in_r, wq_ext, w_ukv.astype(BF16), w_mem_kv.astype(BF16), w_out.astype(BF16)


def _rope_tables(S):
    inv = ROPE_BASE ** (-jnp.arange(0, D_ROPE, 2, dtype=F32) / D_ROPE)
    ang = jnp.arange(S, dtype=F32)[:, None] * inv[None, :]
    cos, sin = jnp.cos(ang), jnp.sin(ang)
    zeros = jnp.zeros((S, LANES - D_ROPE), F32)
    return (jnp.concatenate([cos, cos, zeros], axis=1),
            jnp.concatenate([-sin, sin, zeros], axis=1))


def _dil_biases(rel_bias, tq):
    return jnp.stack([_dil_bias_tiles(rel_bias, g, dil, tq, tq + 2 * DIL_HALF)
                      for g, (_, dil) in enumerate(DIL_PAIRS)])


def _layer(x, mem, pre_gain, q_gain, kv_gain, mem_gain, post_gain, biases, weights):
    w_in_r, wq_ext, wkv, wmem, wout = weights
    B, S, _ = x.shape
    t = _tiles(S)
    proj = _inproj(x.reshape(B * S, D_MODEL), pre_gain, w_in_r, t["tm_in"], t["tn_in"], t["rows_in"])
    proj = proj.reshape(B, S, D_INP)
    cos_tab, sin_tab = _rope_tables(S)
    q, k, v = _latent(proj, q_gain, kv_gain, wq_ext, wkv, cos_tab, sin_tab, B, S, t["tm_lat"])
    yb = _mla(q, k, v, proj, B, S, t["tq_mla"], t["tk_mla"])
    ya = _dilated(proj, biases, B, S, t["tq_dil"], t["unroll_dil"])
    mkv = _memkv(mem, mem_gain, wmem)
    return _outproj(ya, yb, proj, mkv, wout, post_gain, x, t["tm_out"])


def kernel(x_prompt, x_sample, mem_prompt, mem_sample, pre_gain, w_in, q_gain, w_uq, kv_gain, w_ukv,
           mem_gain, w_mem_kv, w_out, post_gain, rel_bias):
    depth = pre_gain.shape[0]
    weights = [_prep_weights(w_in[l], w_uq[l], w_ukv[l], w_mem_kv[l], w_out[l]) for l in range(depth)]
    biases = _dil_biases(rel_bias, _tiles(x_prompt.shape[1])["tq_dil"])

    def trunk(x, mem):
        for l in range(depth):
            x = _layer(x, mem, pre_gain[l][None], q_gain[l][None], kv_gain[l][None], mem_gain[l][None],
                       post_gain[l][None], biases, weights[l])
        return x

    return trunk(x_prompt, mem_prompt), trunk(x_sample, mem_sample)
```

```python
import functools
import math

import jax
import jax.numpy as jnp
from jax import lax
from jax.experimental import pallas as pl
from jax.experimental.pallas import tpu as pltpu

D_MODEL = 2048
DIL_PAIRS = ((128, 1), (512, 4), (2048, 16))
N_DIL = 3
DIL_HALF = 64
A_HEADS = 4
A_HD = 128
A_WIDTH = A_HEADS * A_HD
B_HEADS = 8
Q_LORA = 512
KV_LORA = 512
D_NOPE = 128
D_ROPE = 64
D_V = 128
B_WIDTH = B_HEADS * D_V
ROPE_BASE = 10000.0
C_HEADS = 4
C_HD = 128
C_WIDTH = C_HEADS * C_HD
D_MIX = A_WIDTH + B_WIDTH + C_WIDTH
N_BUCKETS = 32
MAX_DISTANCE = 1024
EPS = 1e-6
NEG = -1e30

LANES = 128
SUBLANES = 8
VMEM_LIMIT_BYTES = 56 * 1024 * 1024

G_AQ = (0, 4, 8)
G_AK = 12
G_AV = 16
G_AG = 20
G_BCQ = 24
G_BCKV = 28
G_BG = 32
G_CQ = 40
G_CG = 44
G_KR = 48
N_GROUPS = 50
MXU_COLS = 2 * LANES
D_INP = N_GROUPS * LANES
QK_DIM = 2 * LANES

F32 = jnp.float32
BF16 = jnp.bfloat16
_NT = (((1,), (1,)), ((), ()))


def _tiles(S):
    return dict(
        tm_in=min(1024, S),
        rows_in=min(256, S),
        tn_in=5 * MXU_COLS,
        tm_lat=min(1024, S),
        tq_mla=min(512, S),
        tk_mla=min(512, S),
        tq_dil=128,
        unroll_dil=64,
        tm_out=min(512, S),
    )


def _params(sem):
    return pltpu.CompilerParams(dimension_semantics=sem, vmem_limit_bytes=VMEM_LIMIT_BYTES)


def _silu(g):
    return g * (1.0 / (1.0 + jnp.exp(-g)))


def _inproj_kernel(x_ref, g_ref, w_ref, o_ref, h_ref, *, rows):
    @pl.when(pl.program_id(1) == 0)
    def _():
        tm = x_ref.shape[0]
        for r0 in range(0, tm, rows):
            x = x_ref[r0:r0 + rows, :]
            ms = jnp.mean(x * x, axis=-1, keepdims=True)
            h_ref[r0:r0 + rows, :] = (x * lax.rsqrt(ms + EPS) * g_ref[...]).astype(BF16)

    o_ref[...] = jnp.dot(h_ref[...], w_ref[...], preferred_element_type=F32).astype(o_ref.dtype)


def _inproj(x2d, gain, w_bf16, tm, tn, rows):
    T = x2d.shape[0]
    return pl.pallas_call(
        functools.partial(_inproj_kernel, rows=rows),
        grid=(T // tm, D_INP // tn),
        in_specs=[
            pl.BlockSpec((tm, D_MODEL), lambda i, j: (i, 0)),
            pl.BlockSpec((1, D_MODEL), lambda i, j: (0, 0)),
            pl.BlockSpec((D_MODEL, tn), lambda i, j: (0, j)),
        ],
        out_specs=pl.BlockSpec((tm, tn), lambda i, j: (i, j)),
        out_shape=jax.ShapeDtypeStruct((T, D_INP), BF16),
        scratch_shapes=[pltpu.VMEM((tm, D_MODEL), BF16)],
        compiler_params=_params(("parallel", "arbitrary")),
        name="inproj",
    )(x2d, gain, w_bf16)


def _latent_kernel(cq_ref, ckv_ref, kr_ref, qg_ref, kvg_ref, wq_ref, wkv_ref, cos_ref, sin_ref,
                   q_ref, k_ref, v_ref, *, q_scale):
    def rms(c_ref, gain_ref):
        c = c_ref[...].astype(F32)
        ms = jnp.mean(c * c, axis=-1, keepdims=True)
        return (c * lax.rsqrt(ms + EPS) * gain_ref[...]).astype(BF16)

    cos = cos_ref[...]
    sin = sin_ref[...]

    def rotary(g):
        return g * cos + pltpu.roll(g, D_ROPE, 1) * sin

    qf = jnp.dot(rms(cq_ref, qg_ref), wq_ref[...], preferred_element_type=F32)
    kvf = jnp.dot(rms(ckv_ref, kvg_ref), wkv_ref[...], preferred_element_type=F32)
    k_rot = rotary(kr_ref[...].astype(F32)).astype(BF16)
    for h in range(B_HEADS):
        c0 = h * QK_DIM
        q_ref[h, 0:LANES, :] = (qf[:, c0:c0 + LANES] * q_scale).T.astype(BF16)
        q_ref[h, LANES:QK_DIM, :] = (rotary(qf[:, c0 + LANES:c0 + QK_DIM]) * q_scale).T.astype(BF16)
        k_ref[h, :, 0:LANES] = kvf[:, c0:c0 + LANES].astype(BF16)
        k_ref[h, :, LANES:QK_DIM] = k_rot
        v_ref[h, :, :] = kvf[:, c0 + LANES:c0 + QK_DIM].T.astype(BF16)


def _latent(proj, q_gain, kv_gain, wq_ext, wkv, cos_tab, sin_tab, B, S, tm):
    q_scale = (D_NOPE + D_ROPE) ** -0.5 * math.log2(math.e)
    ns = S // tm
    col = lambda g: (lambda b, s: (b, s, g))
    row = lambda b, s: (0, 0)
    return pl.pallas_call(
        functools.partial(_latent_kernel, q_scale=q_scale),
        grid=(B, ns),
        in_specs=[
            pl.BlockSpec((None, tm, Q_LORA), col(G_BCQ * LANES // Q_LORA)),
            pl.BlockSpec((None, tm, KV_LORA), col(G_BCKV * LANES // KV_LORA)),
            pl.BlockSpec((None, tm, LANES), col(G_KR)),
            pl.BlockSpec((1, Q_LORA), row),
            pl.BlockSpec((1, KV_LORA), row),
            pl.BlockSpec((Q_LORA, B_HEADS * QK_DIM), row),
            pl.BlockSpec((KV_LORA, B_HEADS * QK_DIM), row),
            pl.BlockSpec((tm, LANES), lambda b, s: (s, 0)),
            pl.BlockSpec((tm, LANES), lambda b, s: (s, 0)),
        ],
        out_specs=[
            pl.BlockSpec((None, B_HEADS, QK_DIM, tm), lambda b, s: (b, 0, 0, s)),
            pl.BlockSpec((None, B_HEADS, tm, QK_DIM), lambda b, s: (b, 0, s, 0)),
            pl.BlockSpec((None, B_HEADS, D_V, tm), lambda b, s: (b, 0, 0, s)),
        ],
        out_shape=[
            jax.ShapeDtypeStruct((B, B_HEADS, QK_DIM, S), BF16),
            jax.ShapeDtypeStruct((B, B_HEADS, S, QK_DIM), BF16),
            jax.ShapeDtypeStruct((B, B_HEADS, D_V, S), BF16),
        ],
        compiler_params=_params(("parallel", "parallel")),
        name="latent_proj",
    )(proj, proj, proj, q_gain, kv_gain, wq_ext, wkv, cos_tab, sin_tab)


def _mla_kernel(q_ref, k_ref, vt_ref, g_ref, o_ref, s_ref, mrun_ref, mfin_ref, *, nk, tk):
    def fold(x, op):
        out = x[0:SUBLANES]
        for t in range(1, x.shape[0] // SUBLANES):
            out = op(out, x[t * SUBLANES:(t + 1) * SUBLANES])
        return out

    @pl.when(pl.program_id(0) == 0)
    def _():
        s_ref[...] = jnp.zeros(s_ref.shape, F32)
        mfin_ref[...] = jnp.zeros(mfin_ref.shape, F32)

    qt = q_ref[...]
    m_prev = mfin_ref[...]
    l_run = None
    acc = None
    for j in range(nk):
        ks = slice(j * tk, (j + 1) * tk)
        p = jnp.exp2(s_ref[ks, :] - m_prev)
        l_blk = fold(p, jnp.add)
        l_run = l_blk if l_run is None else l_run + l_blk
        pv = jnp.dot(vt_ref[:, ks], p.astype(BF16), preferred_element_type=F32)
        acc = pv if acc is None else acc + pv
        s = jnp.dot(k_ref[ks, :], qt, preferred_element_type=F32)
        s_ref[ks, :] = s
        m_blk = fold(s, jnp.maximum)
        mrun_ref[...] = m_blk if j == 0 else jnp.maximum(mrun_ref[...], m_blk)
    mfin_ref[...] = jnp.max(mrun_ref[...], axis=0, keepdims=True)
    o = (acc / jnp.sum(l_run, axis=0, keepdims=True)).T
    o_ref[...] = (o * _silu(g_ref[...].astype(F32))).astype(o_ref.dtype)


def _mla(q, k, v, proj, B, S, tq, tk):
    nq = S // tq
    n_tiles = B * B_HEADS * nq

    def tile(n):
        return n // (B_HEADS * nq), (n // nq) % B_HEADS, n % nq

    def cur(n):
        return tile(jnp.minimum(n, n_tiles - 1))

    def prev(n):
        return tile(jnp.maximum(n - 1, 0))

    def q_map(n):
        b, h, i = cur(n)
        return b, h, 0, i

    def k_map(n):
        b, h, _ = cur(n)
        return b, h, 0, 0

    def v_map(n):
        b, h, _ = prev(n)
        return b, h, 0, 0

    def g_map(n):
        b, h, i = prev(n)
        return b, i, G_BG + h

    def o_map(n):
        b, h, i = prev(n)
        return b, i, h

    return pl.pallas_call(
        functools.partial(_mla_kernel, nk=S // tk, tk=tk),
        grid=(n_tiles + 1,),
        in_specs=[
            pl.BlockSpec((None, None, QK_DIM, tq), q_map),
            pl.BlockSpec((None, None, S, QK_DIM), k_map),
            pl.BlockSpec((None, None, D_V, S), v_map),
            pl.BlockSpec((None, tq, LANES), g_map),
        ],
        out_specs=pl.BlockSpec((None, tq, D_V), o_map),
        out_shape=jax.ShapeDtypeStruct((B, S, B_WIDTH), BF16),
        scratch_shapes=[pltpu.VMEM((S, tq), F32), pltpu.VMEM((SUBLANES, tq), F32), pltpu.VMEM((1, tq), F32)],
        compiler_params=_params(("arbitrary",)),
        name="latent_attn",
    )(q, k, v, proj)


def _t5_bucket(rel):
    nb = N_BUCKETS // 2
    max_exact = nb // 2
    bucket = jnp.where(rel > 0, nb, 0)
    n = jnp.abs(rel)
    nf = jnp.maximum(n, 1).astype(F32)
    large = max_exact + (jnp.log(nf / max_exact) / math.log(MAX_DISTANCE / max_exact)
                         * (nb - max_exact)).astype(jnp.int32)
    large = jnp.minimum(large, nb - 1)
    return bucket + jnp.where(n < max_exact, n, large)


def _dil_bias_tiles(rel_bias, g, dil, tq, tk):
    qi = jnp.arange(tq)[:, None]
    kj = jnp.arange(tk)[None, :]
    table = rel_bias.astype(F32)[:, g * A_HEADS:(g + 1) * A_HEADS]
    rel = jnp.stack([kj - qi - off for off in (0, DIL_HALF, 2 * DIL_HALF)])
    bucket = _t5_bucket(rel * dil)
    b = jnp.zeros((3, A_HEADS, tq, tk), F32)
    for i in range(N_BUCKETS):
        b = jnp.where((bucket == i)[:, None], table[i][None, :, None, None], b)
    return jnp.where((jnp.abs(rel) <= DIL_HALF)[:, None], b, NEG)


def _dil_kernel(q_ref, k_ref, v_ref, gate_ref, bias_ref, o_ref, q32_ref, k32_ref, v32_ref, acc_ref, lse_ref,
                *, S, tq, tk, unroll):
    g = pl.program_id(2)
    scale = A_HD ** -0.5

    def attend(q, k, v, bias):
        s = lax.dot_general(q, k, _NT, preferred_element_type=F32) * scale + bias
        m = jnp.max(s, axis=1, keepdims=True)
        p = jnp.exp(s - m)
        den = jnp.sum(p, axis=1, keepdims=True)
        o = jnp.dot(p.astype(BF16), v, preferred_element_type=F32) / den
        return o, jnp.broadcast_to(m + jnp.log(den), (tq, LANES))

    def window(n, nb, L):
        q0 = pl.multiple_of(n * tq, tq)
        ks = pl.multiple_of(jnp.clip(q0 - DIL_HALF, 0, L - tk), DIL_HALF)
        placement = jnp.where(n == 0, 0, jnp.where(n == nb - 1, 2, 1))
        return q0, ks, placement

    @pl.when(g == 0)
    def _():
        k32_ref[...] = k_ref[...].astype(F32)
        v32_ref[...] = v_ref[...].astype(F32)
        nb = S // tq

        def body(n, carry):
            q0, ks, placement = window(n, nb, S)
            o, lse = attend(q_ref[pl.ds(q0, tq), :], k_ref[pl.ds(ks, tk), :], v_ref[pl.ds(ks, tk), :],
                            bias_ref[placement])
            acc_ref[pl.ds(q0, tq), :] = o
            lse_ref[pl.ds(q0, tq), :] = lse
            return carry

        lax.fori_loop(0, nb, body, 0, unroll=min(unroll, nb))

    def strided_group(dil):
        L = S // dil
        nb = L // tq
        nc = L // DIL_HALF
        wch = tk // DIL_HALF
        qch = tq // DIL_HALF
        q32_ref[...] = q_ref[...].astype(F32)

        def stream(r, carry):
            def chunks(ref):
                return [ref[pl.ds(r + c * DIL_HALF * dil, DIL_HALF, stride=dil), :].astype(BF16)
                        for c in range(nc)]

            kc, vc = chunks(k32_ref), chunks(v32_ref)
            for n in range(nb):
                c0 = min(max(n * qch - 1, 0), nc - wch)
                placement = 0 if n == 0 else (2 if n == nb - 1 else 1)
                q_rows = pl.ds(r + n * tq * dil, tq, stride=dil)
                o, lse = attend(q32_ref[q_rows, :].astype(BF16), jnp.concatenate(kc[c0:c0 + wch], axis=0),
                                jnp.concatenate(vc[c0:c0 + wch], axis=0), bias_ref[placement])
                lse_old = lse_ref[q_rows, :]
                m2 = jnp.maximum(lse_old, lse)
                e_old = jnp.exp(lse_old - m2)
                e_new = jnp.exp(lse - m2)
                den = e_old + e_new
                acc_ref[q_rows, :] = (acc_ref[q_rows, :] * e_old + o * e_new) / den
                lse_ref[q_rows, :] = m2 + jnp.log(den)
            return carry

        lax.fori_loop(0, dil, stream, 0, unroll=max(1, min(unroll // nb, dil)))

    for gi in range(1, N_DIL):
        pl.when(g == gi)(functools.partial(strided_group, DIL_PAIRS[gi][1]))

    @pl.when(g == N_DIL - 1)
    def _():
        o_ref[...] = (acc_ref[...] * _silu(gate_ref[...].astype(F32))).astype(o_ref.dtype)


def _dilated(proj, bias, B, S, tq, unroll):
    tk = tq + 2 * DIL_HALF
    for _, dil in DIL_PAIRS:
        nb = S // dil // tq
        assert nb >= 2 and nb * tq * dil == S, (S, dil, tq)
    q_group = lambda g: G_AQ[1] * g + (G_AQ[2] - 2 * G_AQ[1]) * (g // 2)
    col = lambda g0: (lambda b, h, g: (b, 0, g0 + h))
    return pl.pallas_call(
        functools.partial(_dil_kernel, S=S, tq=tq, tk=tk, unroll=unroll),
        grid=(B, A_HEADS, N_DIL),
        in_specs=[
            pl.BlockSpec((None, S, LANES), lambda b, h, g: (b, 0, q_group(g) + h)),
            pl.BlockSpec((None, S, LANES), col(G_AK)),
            pl.BlockSpec((None, S, LANES), col(G_AV)),
            pl.BlockSpec((None, S, LANES), col(G_AG)),
            pl.BlockSpec((None, 3, None, tq, tk), lambda b, h, g: (g, 0, h, 0, 0)),
        ],
        out_specs=pl.BlockSpec((None, S, LANES), lambda b, h, g: (b, 0, h)),
        out_shape=jax.ShapeDtypeStruct((B, S, A_WIDTH), BF16),
        scratch_shapes=[pltpu.VMEM((S, LANES), F32) for _ in range(5)],
        compiler_params=_params(("parallel", "arbitrary", "arbitrary")),
        name="dilated_attn",
    )(proj, proj, proj, proj, bias)


def _memkv_kernel(mem_ref, g_ref, w_ref, o_ref):
    x = mem_ref[...]
    ms = jnp.mean(x * x, axis=-1, keepdims=True)
    hn = (x * lax.rsqrt(ms + EPS) * g_ref[...]).astype(BF16)
    o_ref[...] = jnp.dot(hn, w_ref[...], preferred_element_type=F32).astype(o_ref.dtype)


def _memkv(mem, gain, w_bf16):
    B, M, _ = mem.shape
    return pl.pallas_call(
        _memkv_kernel,
        grid=(B,),
        in_specs=[
            pl.BlockSpec((None, M, D_MODEL), lambda b: (b, 0, 0)),
            pl.BlockSpec((1, D_MODEL), lambda b: (0, 0)),
            pl.BlockSpec((D_MODEL, 2 * C_WIDTH), lambda b: (0, 0)),
        ],
        out_specs=pl.BlockSpec((None, M, 2 * C_WIDTH), lambda b: (b, 0, 0)),
        out_shape=jax.ShapeDtypeStruct((B, M, 2 * C_WIDTH), BF16),
        compiler_params=_params(("parallel",)),
        name="mem_kv_proj",
    )(mem, gain, w_bf16)


def _out_kernel(ya_ref, yb_ref, cq_ref, cg_ref, mkv_ref, w_ref, pg_ref, x_ref, o_ref, y_ref):
    n = pl.program_id(0)
    cur = lax.rem(n, 2)
    prv = 1 - cur

    @pl.when(n == 0)
    def _():
        y_ref[1] = jnp.zeros(y_ref.shape[1:], BF16)

    y = jnp.dot(y_ref[prv], w_ref[...], preferred_element_type=F32)
    ms = jnp.mean(y * y, axis=-1, keepdims=True)
    o_ref[...] = x_ref[...] + y * lax.rsqrt(ms + EPS) * pg_ref[...]

    y_ref[cur, :, 0:A_WIDTH] = ya_ref[...]
    y_ref[cur, :, A_WIDTH:A_WIDTH + B_WIDTH] = yb_ref[...]
    scale = C_HD ** -0.5
    for h in range(C_HEADS):
        cs = slice(h * C_HD, (h + 1) * C_HD)
        s = lax.dot_general(cq_ref[:, cs], mkv_ref[:, cs], _NT, preferred_element_type=F32) * scale
        m = jnp.max(s, axis=1, keepdims=True)
        p = jnp.exp(s - m)
        den = jnp.sum(p, axis=1, keepdims=True)
        mv = mkv_ref[:, C_WIDTH + h * C_HD:C_WIDTH + (h + 1) * C_HD]
        oc = jnp.dot(p.astype(BF16), mv, preferred_element_type=F32) / den
        c0 = A_WIDTH + B_WIDTH + h * C_HD
        y_ref[cur, :, c0:c0 + C_HD] = (oc * _silu(cg_ref[:, cs].astype(F32))).astype(BF16)


def _outproj(ya, yb, proj, mkv, w_out, post_gain, x, tm):
    B, S, _ = x.shape
    M = mkv.shape[1]
    ns = S // tm
    n_tiles = B * ns

    def cur(n):
        t = jnp.minimum(n, n_tiles - 1)
        return t // ns, t % ns

    def prev(n):
        t = jnp.maximum(n - 1, 0)
        return t // ns, t % ns

    def tok(width, blk, which):
        return pl.BlockSpec((None, tm, width), lambda n: (*which(n), blk))

    const = lambda shape: pl.BlockSpec(shape, lambda n: (0,) * len(shape))
    return pl.pallas_call(
        _out_kernel,
        grid=(n_tiles + 1,),
        in_specs=[
            tok(A_WIDTH, 0, cur),
            tok(B_WIDTH, 0, cur),
            tok(C_WIDTH, G_CQ * LANES // C_WIDTH, cur),
            tok(C_WIDTH, G_CG * LANES // C_WIDTH, cur),
            pl.BlockSpec((None, M, 2 * C_WIDTH), lambda n: (cur(n)[0], 0, 0)),
            const((D_MIX, D_MODEL)),
            const((1, D_MODEL)),
            tok(D_MODEL, 0, prev),
        ],
        out_specs=tok(D_MODEL, 0, prev),
        out_shape=jax.ShapeDtypeStruct((B, S, D_MODEL), x.dtype),
        scratch_shapes=[pltpu.VMEM((2, tm, D_MIX), BF16)],
        compiler_params=_params(("arbitrary",)),
        name="outproj",
    )(ya, yb, proj, proj, mkv, w_out, post_gain, x)


def _swap_halves(w):
    half = D_ROPE // 2
    return jnp.concatenate([w[..., half:], w[..., :half]], axis=-1)


def _prep_weights(w_in, w_uq, w_ukv, w_mem_kv, w_out):
    o_kr = N_DIL * A_WIDTH + 3 * A_WIDTH + Q_LORA + KV_LORA
    o_bg = o_kr + D_ROPE
    w_in = w_in.astype(BF16)
    kr = w_in[:, o_kr:o_bg]
    w_in_r = jnp.concatenate(
        [w_in[:, :o_kr], w_in[:, o_bg:], kr, _swap_halves(kr),
         jnp.zeros((D_MODEL, D_INP - (G_KR + 1) * LANES), BF16)], axis=1)
    wq = w_uq.reshape(Q_LORA, B_HEADS, D_NOPE + D_ROPE)
    rope = wq[:, :, D_NOPE:]
    wq_ext = jnp.concatenate([wq[:, :, :D_NOPE], rope, _swap_halves(rope)], axis=-1)
    wq_ext = wq_ext.reshape(Q_LORA, B_HEADS * QK_DIM).astype(BF16)
    return w_in_r, wq_ext, w_ukv.astype(BF16), w_mem_kv.astype(BF16), w_out.astype(BF16)


def _rope_tables(S):
    inv = ROPE_BASE ** (-jnp.arange(0, D_ROPE, 2, dtype=F32) / D_ROPE)
    ang = jnp.arange(S, dtype=F32)[:, None] * inv[None, :]
    cos, sin = jnp.cos(ang), jnp.sin(ang)
    zeros = jnp.zeros((S, LANES - D_ROPE), F32)
    return (jnp.concatenate([cos, cos, zeros], axis=1),
            jnp.concatenate([-sin, sin, zeros], axis=1))


def _dil_biases(rel_bias, tq):
    return jnp.stack([_dil_bias_tiles(rel_bias, g, dil, tq, tq + 2 * DIL_HALF)
                      for g, (_, dil) in enumerate(DIL_PAIRS)])


def _layer(x, mem, pre_gain, q_gain, kv_gain, mem_gain, post_gain, biases, weights):
    w_in_r, wq_ext, wkv, wmem, wout = weights
    B, S, _ = x.shape
    t = _tiles(S)
    proj = _inproj(x.reshape(B * S, D_MODEL), pre_gain, w_in_r, t["tm_in"], t["tn_in"], t["rows_in"])
    proj = proj.reshape(B, S, D_INP)
    cos_tab, sin_tab = _rope_tables(S)
    q, k, v = _latent(proj, q_gain, kv_gain, wq_ext, wkv, cos_tab, sin_tab, B, S, t["tm_lat"])
    yb = _mla(q, k, v, proj, B, S, t["tq_mla"], t["tk_mla"])
    ya = _dilated(proj, biases, B, S, t["tq_dil"], t["unroll_dil"])
    mkv = _memkv(mem, mem_gain, wmem)
    return _outproj(ya, yb, proj, mkv, wout, post_gain, x, t["tm_out"])


def kernel(x_prompt, x_sample, mem_prompt, mem_sample, pre_gain, w_in, q_gain, w_uq, kv_gain, w_ukv,
           mem_gain, w_mem_kv, w_out, post_gain, rel_bias):
    depth = pre_gain.shape[0]
    weights = [_prep_weights(w_in[l], w_uq[l], w_ukv[l], w_mem_kv[l], w_out[l]) for l in range(depth)]
    biases = _dil_biases(rel_bias, _tiles(x_prompt.shape[1])["tq_dil"])

    def trunk(x, mem):
        for l in range(depth):
            x = _layer(x, mem, pre_gain[l][None], q_gain[l][None], kv_gain[l][None], mem_gain[l][None],
                       post_gain[l][None], biases, weights[l])
        return x

    return trunk(x_prompt, mem_prompt), trunk(x_sample, mem_sample)
```

```python
import functools
import math

import jax
import jax.numpy as jnp
from jax import lax
from jax.experimental import pallas as pl
from jax.experimental.pallas import tpu as pltpu

D_MODEL = 2048
DIL_PAIRS = ((128, 1), (512, 4), (2048, 16))
N_DIL = 3
DIL_HALF = 64
DIL_BASE = 4
A_HEADS = 4
A_HD = 128
A_WIDTH = A_HEADS * A_HD
B_HEADS = 8
Q_LORA = 512
KV_LORA = 512
D_NOPE = 128
D_ROPE = 64
D_V = 128
B_WIDTH = B_HEADS * D_V
ROPE_BASE = 10000.0
C_HEADS = 4
C_HD = 128
C_WIDTH = C_HEADS * C_HD
D_MIX = A_WIDTH + B_WIDTH + C_WIDTH
N_BUCKETS = 32
MAX_DISTANCE = 1024
EPS = 1e-6
NEG = -1e30

LANES = 128
SUBLANES = 8
VMEM_LIMIT_BYTES = 56 * 1024 * 1024

G_AQ = (0, 4, 8)
G_AK = 12
G_AV = 16
G_AG = 20
G_BCQ = 24
G_BCKV = 28
G_BG = 32
G_CQ = 40
G_CG = 44
G_KR = 48
N_GROUPS = 50
MXU_COLS = 2 * LANES
D_INP = N_GROUPS * LANES
QK_DIM = 2 * LANES

F32 = jnp.float32
BF16 = jnp.bfloat16
_NT = (((1,), (1,)), ((), ()))


def _tiles(S):
    return dict(
        tm_in=min(1024, S),
        rows_in=min(256, S),
        tn_in=5 * MXU_COLS,
        tm_lat=min(1024, S),
        tq_mla=min(512, S),
        tk_mla=min(512, S),
        tq_dil=128,
        unroll_dil=64,
        tm_out=min(512, S),
    )


def _params(sem):
    return pltpu.CompilerParams(dimension_semantics=sem, vmem_limit_bytes=VMEM_LIMIT_BYTES)


def _silu(g):
    return g * (1.0 / (1.0 + jnp.exp(-g)))


def _inproj_kernel(x_ref, g_ref, w_ref, o_ref, h_ref, *, rows):
    @pl.when(pl.program_id(1) == 0)
    def _():
        tm = x_ref.shape[0]
        for r0 in range(0, tm, rows):
            x = x_ref[r0:r0 + rows, :]
            ms = jnp.mean(x * x, axis=-1, keepdims=True)
            h_ref[r0:r0 + rows, :] = (x * lax.rsqrt(ms + EPS) * g_ref[...]).astype(BF16)

    o_ref[...] = jnp.dot(h_ref[...], w_ref[...], preferred_element_type=F32).astype(o_ref.dtype)


def _inproj(x2d, gain, w_bf16, tm, tn, rows):
    T = x2d.shape[0]
    return pl.pallas_call(
        functools.partial(_inproj_kernel, rows=rows),
        grid=(T // tm, D_INP // tn),
        in_specs=[
            pl.BlockSpec((tm, D_MODEL), lambda i, j: (i, 0)),
            pl.BlockSpec((1, D_MODEL), lambda i, j: (0, 0)),
            pl.BlockSpec((D_MODEL, tn), lambda i, j: (0, j)),
        ],
        out_specs=pl.BlockSpec((tm, tn), lambda i, j: (i, j)),
        out_shape=jax.ShapeDtypeStruct((T, D_INP), BF16),
        scratch_shapes=[pltpu.VMEM((tm, D_MODEL), BF16)],
        compiler_params=_params(("parallel", "arbitrary")),
        name="inproj",
    )(x2d, gain, w_bf16)


def _latent_kernel(cq_ref, ckv_ref, kr_ref, qg_ref, kvg_ref, wq_ref, wkv_ref, cos_ref, sin_ref,
                   q_ref, k_ref, v_ref, *, q_scale):
    def rms(c_ref, gain_ref):
        c = c_ref[...].astype(F32)
        ms = jnp.mean(c * c, axis=-1, keepdims=True)
        return (c * lax.rsqrt(ms + EPS) * gain_ref[...]).astype(BF16)

    cos = cos_ref[...]
    sin = sin_ref[...]

    def rotary(g):
        return g * cos + pltpu.roll(g, D_ROPE, 1) * sin

    qf = jnp.dot(rms(cq_ref, qg_ref), wq_ref[...], preferred_element_type=F32)
    kvf = jnp.dot(rms(ckv_ref, kvg_ref), wkv_ref[...], preferred_element_type=F32)
    k_rot = rotary(kr_ref[...].astype(F32)).astype(BF16)
    for h in range(B_HEADS):
        c0 = h * QK_DIM
        q_ref[h, 0:LANES, :] = (qf[:, c0:c0 + LANES] * q_scale).T.astype(BF16)
        q_ref[h, LANES:QK_DIM, :] = (rotary(qf[:, c0 + LANES:c0 + QK_DIM]) * q_scale).T.astype(BF16)
        k_ref[h, :, 0:LANES] = kvf[:, c0:c0 + LANES].astype(BF16)
        k_ref[h, :, LANES:QK_DIM] = k_rot
        v_ref[h, :, :] = kvf[:, c0 + LANES:c0 + QK_DIM].T.astype(BF16)


def _latent(proj, q_gain, kv_gain, wq_ext, wkv, cos_tab, sin_tab, B, S, tm):
    q_scale = (D_NOPE + D_ROPE) ** -0.5 * math.log2(math.e)
    ns = S // tm
    col = lambda g: (lambda b, s: (b, s, g))
    row = lambda b, s: (0, 0)
    return pl.pallas_call(
        functools.partial(_latent_kernel, q_scale=q_scale),
        grid=(B, ns),
        in_specs=[
            pl.BlockSpec((None, tm, Q_LORA), col(G_BCQ * LANES // Q_LORA)),
            pl.BlockSpec((None, tm, KV_LORA), col(G_BCKV * LANES // KV_LORA)),
            pl.BlockSpec((None, tm, LANES), col(G_KR)),
            pl.BlockSpec((1, Q_LORA), row),
            pl.BlockSpec((1, KV_LORA), row),
            pl.BlockSpec((Q_LORA, B_HEADS * QK_DIM), row),
            pl.BlockSpec((KV_LORA, B_HEADS * QK_DIM), row),
            pl.BlockSpec((tm, LANES), lambda b, s: (s, 0)),
            pl.BlockSpec((tm, LANES), lambda b, s: (s, 0)),
        ],
        out_specs=[
            pl.BlockSpec((None, B_HEADS, QK_DIM, tm), lambda b, s: (b, 0, 0, s)),
            pl.BlockSpec((None, B_HEADS, tm, QK_DIM), lambda b, s: (b, 0, s, 0)),
            pl.BlockSpec((None, B_HEADS, D_V, tm), lambda b, s: (b, 0, 0, s)),
        ],
        out_shape=[
            jax.ShapeDtypeStruct((B, B_HEADS, QK_DIM, S), BF16),
            jax.ShapeDtypeStruct((B, B_HEADS, S, QK_DIM), BF16),
            jax.ShapeDtypeStruct((B, B_HEADS, D_V, S), BF16),
        ],
        compiler_params=_params(("parallel", "parallel")),
        name="latent_proj",
    )(proj, proj, proj, q_gain, kv_gain, wq_ext, wkv, cos_tab, sin_tab)


def _mla_kernel(q_ref, k_ref, vt_ref, g_ref, o_ref, s_ref, mrun_ref, mfin_ref, *, nk, tk):
    def fold(x, op):
        out = x[0:SUBLANES]
        for t in range(1, x.shape[0] // SUBLANES):
            out = op(out, x[t * SUBLANES:(t + 1) * SUBLANES])
        return out

    @pl.when(pl.program_id(0) == 0)
    def _():
        s_ref[...] = jnp.zeros(s_ref.shape, F32)
        mfin_ref[...] = jnp.zeros(mfin_ref.shape, F32)

    qt = q_ref[...]
    m_prev = mfin_ref[...]
    l_run = None
    acc = None
    for j in range(nk):
        ks = slice(j * tk, (j + 1) * tk)
        p = jnp.exp2(s_ref[ks, :] - m_prev)
        l_blk = fold(p, jnp.add)
        l_run = l_blk if l_run is None else l_run + l_blk
        pv = jnp.dot(vt_ref[:, ks], p.astype(BF16), preferred_element_type=F32)
        acc = pv if acc is None else acc + pv
        s = jnp.dot(k_ref[ks, :], qt, preferred_element_type=F32)
        s_ref[ks, :] = s
        m_blk = fold(s, jnp.maximum)
        mrun_ref[...] = m_blk if j == 0 else jnp.maximum(mrun_ref[...], m_blk)
    mfin_ref[...] = jnp.max(mrun_ref[...], axis=0, keepdims=True)
    o = (acc / jnp.sum(l_run, axis=0, keepdims=True)).T
    o_ref[...] = (o * _silu(g_ref[...].astype(F32))).astype(o_ref.dtype)


def _mla(q, k, v, proj, B, S, tq, tk):
    nq = S // tq
    n_tiles = B * B_HEADS * nq

    def tile(n):
        return n // (B_HEADS * nq), (n // nq) % B_HEADS, n % nq

    def cur(n):
        return tile(jnp.minimum(n, n_tiles - 1))

    def prev(n):
        return tile(jnp.maximum(n - 1, 0))

    def q_map(n):
        b, h, i = cur(n)
        return b, h, 0, i

    def k_map(n):
        b, h, _ = cur(n)
        return b, h, 0, 0

    def v_map(n):
        b, h, _ = prev(n)
        return b, h, 0, 0

    def g_map(n):
        b, h, i = prev(n)
        return b, i, G_BG + h

    def o_map(n):
        b, h, i = prev(n)
        return b, i, h

    return pl.pallas_call(
        functools.partial(_mla_kernel, nk=S // tk, tk=tk),
        grid=(n_tiles + 1,),
        in_specs=[
            pl.BlockSpec((None, None, QK_DIM, tq), q_map),
            pl.BlockSpec((None, None, S, QK_DIM), k_map),
            pl.BlockSpec((None, None, D_V, S), v_map),
            pl.BlockSpec((None, tq, LANES), g_map),
        ],
        out_specs=pl.BlockSpec((None, tq, D_V), o_map),
        out_shape=jax.ShapeDtypeStruct((B, S, B_WIDTH), BF16),
        scratch_shapes=[pltpu.VMEM((S, tq), F32), pltpu.VMEM((SUBLANES, tq), F32), pltpu.VMEM((1, tq), F32)],
        compiler_params=_params(("arbitrary",)),
        name="latent_attn",
    )(q, k, v, proj)


def _t5_bucket(rel):
    nb = N_BUCKETS // 2
    max_exact = nb // 2
    bucket = jnp.where(rel > 0, nb, 0)
    n = jnp.abs(rel)
    nf = jnp.maximum(n, 1).astype(F32)
    large = max_exact + (jnp.log(nf / max_exact) / math.log(MAX_DISTANCE / max_exact)
                         * (nb - max_exact)).astype(jnp.int32)
    large = jnp.minimum(large, nb - 1)
    return bucket + jnp.where(n < max_exact, n, large)


def _dil_bias_tiles(rel_bias, g, dil, tq, tk):
    qi = jnp.arange(tq)[:, None]
    kj = jnp.arange(tk)[None, :]
    table = rel_bias.astype(F32)[:, g * A_HEADS:(g + 1) * A_HEADS]
    rel = jnp.stack([kj - qi - off for off in (0, DIL_HALF, 2 * DIL_HALF)])
    bucket = _t5_bucket(rel * dil)
    b = jnp.zeros((3, A_HEADS, tq, tk), F32)
    for i in range(N_BUCKETS):
        b = jnp.where((bucket == i)[:, None], table[i][None, :, None, None], b)
    return jnp.where((jnp.abs(rel) <= DIL_HALF)[:, None], b, NEG)


def _dil_kernel(q_ref, k_ref, v_ref, gate_ref, bias_ref, o_ref, t1_ref, t2_ref, k4_ref, v4_ref, acc_ref, lse_ref,
                *, S, tq, tk, unroll):
    g = pl.program_id(2)
    scale = A_HD ** -0.5
    LB = S // DIL_BASE

    def attend(q, k, v, bias):
        s = lax.dot_general(q, k, _NT, preferred_element_type=F32) * scale + bias
        m = jnp.max(s, axis=1, keepdims=True)
        p = jnp.exp(s - m)
        den = jnp.sum(p, axis=1, keepdims=True)
        o = jnp.dot(p.astype(BF16), v, preferred_element_type=F32) / den
        return o, jnp.broadcast_to(m + jnp.log(den), (tq, LANES))

    def to_base_major(dst_ref, src_ref):
        for b in range(DIL_BASE):
            dst_ref[b * LB:(b + 1) * LB, :] = src_ref[pl.ds(b, LB, stride=DIL_BASE), :]

    @pl.when(g == 0)
    def _():
        t1_ref[...] = k_ref[...].astype(F32)
        to_base_major(k4_ref, t1_ref)
        t1_ref[...] = v_ref[...].astype(F32)
        to_base_major(v4_ref, t1_ref)
        nb = S // tq
        tb = tq // DIL_BASE

        def body(n, carry):
            q0 = pl.multiple_of(n * tq, tq)
            ks = pl.multiple_of(jnp.clip(q0 - DIL_HALF, 0, S - tk), DIL_HALF)
            placement = jnp.where(n == 0, 0, jnp.where(n == nb - 1, 2, 1))
            o, lse = attend(q_ref[pl.ds(q0, tq), :], k_ref[pl.ds(ks, tk), :], v_ref[pl.ds(ks, tk), :],
                            bias_ref[placement])
            t1_ref[pl.ds(q0, tq), :] = o
            t2_ref[pl.ds(q0, tq), :] = lse
            for b in range(DIL_BASE):
                rows = pl.ds(pl.multiple_of(b * LB + n * tb, SUBLANES), tb)
                acc_ref[rows, :] = t1_ref[pl.ds(q0 + b, tb, stride=DIL_BASE), :]
                lse_ref[rows, :] = t2_ref[pl.ds(q0 + b, tb, stride=DIL_BASE), :]
            return carry

        lax.fori_loop(0, nb, body, 0, unroll=min(unroll, nb))

    def strided_group(dil):
        e = dil // DIL_BASE
        L = S // dil
        nb = L // tq
        nc = L // DIL_HALF
        wch = tk // DIL_HALF
        qch = tq // DIL_HALF
        t1_ref[...] = q_ref[...].astype(F32)
        to_base_major(t2_ref, t1_ref)

        def rows_of(start, n_rows):
            if e > 1:
                return pl.ds(start, n_rows, stride=e)
            return pl.ds(pl.multiple_of(start, SUBLANES), n_rows)

        def stream(r, carry):
            base = lax.rem(r, DIL_BASE) * LB + lax.div(r, DIL_BASE)

            def chunks(ref):
                return [ref[rows_of(base + c * DIL_HALF * e, DIL_HALF), :].astype(BF16) for c in range(nc)]

            kc, vc = chunks(k4_ref), chunks(v4_ref)
            for n in range(nb):
                c0 = min(max(n * qch - 1, 0), nc - wch)
                placement = 0 if n == 0 else (2 if n == nb - 1 else 1)
                q_rows = rows_of(base + n * tq * e, tq)
                o, lse = attend(t2_ref[q_rows, :].astype(BF16), jnp.concatenate(kc[c0:c0 + wch], axis=0),
                                jnp.concatenate(vc[c0:c0 + wch], axis=0), bias_ref[placement])
                lse_old = lse_ref[q_rows, :]
                m2 = jnp.maximum(lse_old, lse)
                e_old = jnp.exp(lse_old - m2)
                e_new = jnp.exp(lse - m2)
                den = e_old + e_new
                acc_ref[q_rows, :] = (acc_ref[q_rows, :] * e_old + o * e_new) / den
                lse_ref[q_rows, :] = m2 + jnp.log(den)
            return carry

        lax.fori_loop(0, dil, stream, 0, unroll=max(1, min(unroll // nb, dil)))

    for gi in range(1, N_DIL):
        pl.when(g == gi)(functools.partial(strided_group, DIL_PAIRS[gi][1]))

    @pl.when(g == N_DIL - 1)
    def _():
        for b in range(DIL_BASE):
            t1_ref[pl.ds(b, LB, stride=DIL_BASE), :] = acc_ref[b * LB:(b + 1) * LB, :]
        o_ref[...] = (t1_ref[...] * _silu(gate_ref[...].astype(F32))).astype(o_ref.dtype)


def _dilated(proj, bias, B, S, tq, unroll):
    tk = tq + 2 * DIL_HALF
    for _, dil in DIL_PAIRS:
        nb = S // dil // tq
        assert nb >= 2 and nb * tq * dil == S, (S, dil, tq)
        assert dil == 1 or dil % DIL_BASE == 0, dil
    q_group = lambda g: G_AQ[1] * g + (G_AQ[2] - 2 * G_AQ[1]) * (g // 2)
    col = lambda g0: (lambda b, h, g: (b, 0, g0 + h))
    return pl.pallas_call(
        functools.partial(_dil_kernel, S=S, tq=tq, tk=tk, unroll=unroll),
        grid=(B, A_HEADS, N_DIL),
        in_specs=[
            pl.BlockSpec((None, S, LANES), lambda b, h, g: (b, 0, q_group(g) + h)),
            pl.BlockSpec((None, S, LANES), col(G_AK)),
            pl.BlockSpec((None, S, LANES), col(G_AV)),
            pl.BlockSpec((None, S, LANES), col(G_AG)),
            pl.BlockSpec((None, 3, None, tq, tk), lambda b, h, g: (g, 0, h, 0, 0)),
        ],
        out_specs=pl.BlockSpec((None, S, LANES), lambda b, h, g: (b, 0, h)),
        out_shape=jax.ShapeDtypeStruct((B, S, A_WIDTH), BF16),
        scratch_shapes=[pltpu.VMEM((S, LANES), F32) for _ in range(6)],
        compiler_params=_params(("parallel", "arbitrary", "arbitrary")),
        name="dilated_attn",
    )(proj, proj, proj, proj, bias)


def _memkv_kernel(mem_ref, g_ref, w_ref, o_ref):
    x = mem_ref[...]
    ms = jnp.mean(x * x, axis=-1, keepdims=True)
    hn = (x * lax.rsqrt(ms + EPS) * g_ref[...]).astype(BF16)
    o_ref[...] = jnp.dot(hn, w_ref[...], preferred_element_type=F32).astype(o_ref.dtype)


def _memkv(mem, gain, w_bf16):
    B, M, _ = mem.shape
    return pl.pallas_call(
        _memkv_kernel,
        grid=(B,),
        in_specs=[
            pl.BlockSpec((None, M, D_MODEL), lambda b: (b, 0, 0)),
            pl.BlockSpec((1, D_MODEL), lambda b: (0, 0)),
            pl.BlockSpec((D_MODEL, 2 * C_WIDTH), lambda b: (0, 0)),
        ],
        out_specs=pl.BlockSpec((None, M, 2 * C_WIDTH), lambda b: (b, 0, 0)),
        out_shape=jax.ShapeDtypeStruct((B, M, 2 * C_WIDTH), BF16),
        compiler_params=_params(("parallel",)),
        name="mem_kv_proj",
    )(mem, gain, w_bf16)


def _out_kernel(ya_ref, yb_ref, cq_ref, cg_ref, mkv_ref, w_ref, pg_ref, x_ref, o_ref, y_ref):
    n = pl.program_id(0)
    cur = lax.rem(n, 2)
    prv = 1 - cur

    @pl.when(n == 0)
    def _():
        y_ref[1] = jnp.zeros(y_ref.shape[1:], BF16)

    y = jnp.dot(y_ref[prv], w_ref[...], preferred_element_type=F32)
    ms = jnp.mean(y * y, axis=-1, keepdims=True)
    o_ref[...] = x_ref[...] + y * lax.rsqrt(ms + EPS) * pg_ref[...]

    y_ref[cur, :, 0:A_WIDTH] = ya_ref[...]
    y_ref[cur, :, A_WIDTH:A_WIDTH + B_WIDTH] = yb_ref[...]
    scale = C_HD ** -0.5
    for h in range(C_HEADS):
        cs = slice(h * C_HD, (h + 1) * C_HD)
        s = lax.dot_general(cq_ref[:, cs], mkv_ref[:, cs], _NT, preferred_element_type=F32) * scale
        m = jnp.max(s, axis=1, keepdims=True)
        p = jnp.exp(s - m)
        den = jnp.sum(p, axis=1, keepdims=True)
        mv = mkv_ref[:, C_WIDTH + h * C_HD:C_WIDTH + (h + 1) * C_HD]
        oc = jnp.dot(p.astype(BF16), mv, preferred_element_type=F32) / den
        c0 = A_WIDTH + B_WIDTH + h * C_HD
        y_ref[cur, :, c0:c0 + C_HD] = (oc * _silu(cg_ref[:, cs].astype(F32))).astype(BF16)


def _outproj(ya, yb, proj, mkv, w_out, post_gain, x, tm):
    B, S, _ = x.shape
    M = mkv.shape[1]
    ns = S // tm
    n_tiles = B * ns

    def cur(n):
        t = jnp.minimum(n, n_tiles - 1)
        return t // ns, t % ns

    def prev(n):
        t = jnp.maximum(n - 1, 0)
        return t // ns, t % ns

    def tok(width, blk, which):
        return pl.BlockSpec((None, tm, width), lambda n: (*which(n), blk))

    const = lambda shape: pl.BlockSpec(shape, lambda n: (0,) * len(shape))
    return pl.pallas_call(
        _out_kernel,
        grid=(n_tiles + 1,),
        in_specs=[
            tok(A_WIDTH, 0, cur),
            tok(B_WIDTH, 0, cur),
            tok(C_WIDTH, G_CQ * LANES // C_WIDTH, cur),
            tok(C_WIDTH, G_CG * LANES // C_WIDTH, cur),
            pl.BlockSpec((None, M, 2 * C_WIDTH), lambda n: (cur(n)[0], 0, 0)),
            const((D_MIX, D_MODEL)),
            const((1, D_MODEL)),
            tok(D_MODEL, 0, prev),
        ],
        out_specs=tok(D_MODEL, 0, prev),
        out_shape=jax.ShapeDtypeStruct((B, S, D_MODEL), x.dtype),
        scratch_shapes=[pltpu.VMEM((2, tm, D_MIX), BF16)],
        compiler_params=_params(("arbitrary",)),
        name="outproj",
    )(ya, yb, proj, proj, mkv, w_out, post_gain, x)


def _swap_halves(w):
    half = D_ROPE // 2
    return jnp.concatenate([w[..., half:], w[..., :half]], axis=-1)


def _prep_weights(w_in, w_uq, w_ukv, w_mem_kv, w_out):
    o_kr = N_DIL * A_WIDTH + 3 * A_WIDTH + Q_LORA + KV_LORA
    o_bg = o_kr + D_ROPE
    w_in = w_in.astype(BF16)
    kr = w_in[:, o_kr:o_bg]
    w_in_r = jnp.concatenate(
        [w_in[:, :o_kr], w_in[:, o_bg:], kr, _swap_halves(kr),
         jnp.zeros((D_MODEL, D_INP - (G_KR + 1) * LANES), BF16)], axis=1)
    wq = w_uq.reshape(Q_LORA, B_HEADS, D_NOPE + D_ROPE)
    rope = wq[:, :, D_NOPE:]
    wq_ext = jnp.concatenate([wq[:, :, :D_NOPE], rope, _swap_halves(rope)], axis=-1)
    wq_ext = wq_ext.reshape(Q_LORA, B_HEADS * QK_DIM).astype(BF16)
    return w_in_r, wq_ext, w_ukv.astype(BF16), w_mem_kv.astype(BF16), w_out.astype(BF16)


def _rope_tables(S):
    inv = ROPE_BASE ** (-jnp.arange(0, D_ROPE, 2, dtype=F32) / D_ROPE)
    ang = jnp.arange(S, dtype=F32)[:, None] * inv[None, :]
    cos, sin = jnp.cos(ang), jnp.sin(ang)
    zeros = jnp.zeros((S, LANES - D_ROPE), F32)
    return (jnp.concatenate([cos, cos, zeros], axis=1),
            jnp.concatenate([-sin, sin, zeros], axis=1))


def _dil_biases(rel_bias, tq):
    return jnp.stack([_dil_bias_tiles(rel_bias, g, dil, tq, tq + 2 * DIL_HALF)
                      for g, (_, dil) in enumerate(DIL_PAIRS)])


def _layer(x, mem, pre_gain, q_gain, kv_gain, mem_gain, post_gain, biases, weights):
    w_in_r, wq_ext, wkv, wmem, wout = weights
    B, S, _ = x.shape
    t = _tiles(S)
    proj = _inproj(x.reshape(B * S, D_MODEL), pre_gain, w_in_r, t["tm_in"], t["tn_in"], t["rows_in"])
    proj = proj.reshape(B, S, D_INP)
    cos_tab, sin_tab = _rope_tables(S)
    q, k, v = _latent(proj, q_gain, kv_gain, wq_ext, wkv, cos_tab, sin_tab, B, S, t["tm_lat"])
    yb = _mla(q, k, v, proj, B, S, t["tq_mla"], t["tk_mla"])
    ya = _dilated(proj, biases, B, S, t["tq_dil"], t["unroll_dil"])
    mkv = _memkv(mem, mem_gain, wmem)
    return _outproj(ya, yb, proj, mkv, wout, post_gain, x, t["tm_out"])


def kernel(x_prompt, x_sample, mem_prompt, mem_sample, pre_gain, w_in, q_gain, w_uq, kv_gain, w_ukv,
           mem_gain, w_mem_kv, w_out, post_gain, rel_bias):
    depth = pre_gain.shape[0]
    weights = [_prep_weights(w_in[l], w_uq[l], w_ukv[l], w_mem_kv[l], w_out[l]) for l in range(depth)]
    biases = _dil_biases(rel_bias, _tiles(x_prompt.shape[1])["tq_dil"])

    def trunk(x, mem):
        for l in range(depth):
            x = _layer(x, mem, pre_gain[l][None], q_gain[l][None], kv_gain[l][None], mem_gain[l][None],
                       post_gain[l][None], biases, weights[l])
        return x

    return trunk(x_prompt, mem_prompt), trunk(x_sample, mem_sample)
```

```python
import functools
import math

import jax
import jax.numpy as jnp
from jax import lax
from jax.experimental import pallas as pl
from jax.experimental.pallas import tpu as pltpu

D_MODEL = 2048
DIL_PAIRS = ((128, 1), (512, 4), (2048, 16))
N_DIL = 3
DIL_HALF = 64
DIL_BASE = 4
A_HEADS = 4
A_HD = 128
A_WIDTH = A_HEADS * A_HD
B_HEADS = 8
Q_LORA = 512
KV_LORA = 512
D_NOPE = 128
D_ROPE = 64
D_V = 128
B_WIDTH = B_HEADS * D_V
ROPE_BASE = 10000.0
C_HEADS = 4
C_HD = 128
C_WIDTH = C_HEADS * C_HD
D_MIX = A_WIDTH + B_WIDTH + C_WIDTH
N_BUCKETS = 32
MAX_DISTANCE = 1024
EPS = 1e-6
NEG = -1e30

LANES = 128
SUBLANES = 8
VMEM_LIMIT_BYTES = 56 * 1024 * 1024

G_AQ = (0, 4, 8)
G_AK = 12
G_AV = 16
G_AG = 20
G_BCQ = 24
G_BCKV = 28
G_BG = 32
G_CQ = 40
G_CG = 44
G_KR = 48
N_GROUPS = 50
MXU_COLS = 2 * LANES
D_INP = N_GROUPS * LANES
QK_DIM = 2 * LANES

F32 = jnp.float32
BF16 = jnp.bfloat16
_NT = (((1,), (1,)), ((), ()))


def _tiles(S):
    return dict(
        tm_in=min(1024, S),
        rows_in=min(256, S),
        tn_in=5 * MXU_COLS,
        tm_lat=min(1024, S),
        tq_mla=min(512, S),
        tk_mla=min(512, S),
        tq_dil=128,
        unroll_dil=64,
        tm_out=min(512, S),
    )


def _params(sem):
    return pltpu.CompilerParams(dimension_semantics=sem, vmem_limit_bytes=VMEM_LIMIT_BYTES)


def _silu(g):
    return g * (1.0 / (1.0 + jnp.exp(-g)))


def _inproj_kernel(x_ref, g_ref, w_ref, o_ref, h_ref, *, rows):
    @pl.when(pl.program_id(1) == 0)
    def _():
        tm = x_ref.shape[0]
        for r0 in range(0, tm, rows):
            x = x_ref[r0:r0 + rows, :]
            ms = jnp.mean(x * x, axis=-1, keepdims=True)
            h_ref[r0:r0 + rows, :] = (x * lax.rsqrt(ms + EPS) * g_ref[...]).astype(BF16)

    o_ref[...] = jnp.dot(h_ref[...], w_ref[...], preferred_element_type=F32).astype(o_ref.dtype)


def _inproj(x2d, gain, w_bf16, tm, tn, rows):
    T = x2d.shape[0]
    return pl.pallas_call(
        functools.partial(_inproj_kernel, rows=rows),
        grid=(T // tm, D_INP // tn),
        in_specs=[
            pl.BlockSpec((tm, D_MODEL), lambda i, j: (i, 0)),
            pl.BlockSpec((1, D_MODEL), lambda i, j: (0, 0)),
            pl.BlockSpec((D_MODEL, tn), lambda i, j: (0, j)),
        ],
        out_specs=pl.BlockSpec((tm, tn), lambda i, j: (i, j)),
        out_shape=jax.ShapeDtypeStruct((T, D_INP), BF16),
        scratch_shapes=[pltpu.VMEM((tm, D_MODEL), BF16)],
        compiler_params=_params(("parallel", "arbitrary")),
        name="inproj",
    )(x2d, gain, w_bf16)


def _latent_kernel(cq_ref, ckv_ref, kr_ref, qg_ref, kvg_ref, wq_ref, wkv_ref, cos_ref, sin_ref,
                   q_ref, k_ref, v_ref, *, q_scale):
    def rms(c_ref, gain_ref):
        c = c_ref[...].astype(F32)
        ms = jnp.mean(c * c, axis=-1, keepdims=True)
        return (c * lax.rsqrt(ms + EPS) * gain_ref[...]).astype(BF16)

    cos = cos_ref[...]
    sin = sin_ref[...]

    def rotary(g):
        return g * cos + pltpu.roll(g, D_ROPE, 1) * sin

    qf = jnp.dot(rms(cq_ref, qg_ref), wq_ref[...], preferred_element_type=F32)
    kvf = jnp.dot(rms(ckv_ref, kvg_ref), wkv_ref[...], preferred_element_type=F32)
    k_rot = rotary(kr_ref[...].astype(F32)).astype(BF16)
    for h in range(B_HEADS):
        c0 = h * QK_DIM
        q_ref[h, 0:LANES, :] = (qf[:, c0:c0 + LANES] * q_scale).T.astype(BF16)
        q_ref[h, LANES:QK_DIM, :] = (rotary(qf[:, c0 + LANES:c0 + QK_DIM]) * q_scale).T.astype(BF16)
        k_ref[h, :, 0:LANES] = kvf[:, c0:c0 + LANES].astype(BF16)
        k_ref[h, :, LANES:QK_DIM] = k_rot
        v_ref[h, :, :] = kvf[:, c0 + LANES:c0 + QK_DIM].T.astype(BF16)


def _latent(proj, q_gain, kv_gain, wq_ext, wkv, cos_tab, sin_tab, B, S, tm):
    q_scale = (D_NOPE + D_ROPE) ** -0.5 * math.log2(math.e)
    ns = S // tm
    col = lambda g: (lambda b, s: (b, s, g))
    row = lambda b, s: (0, 0)
    return pl.pallas_call(
        functools.partial(_latent_kernel, q_scale=q_scale),
        grid=(B, ns),
        in_specs=[
            pl.BlockSpec((None, tm, Q_LORA), col(G_BCQ * LANES // Q_LORA)),
            pl.BlockSpec((None, tm, KV_LORA), col(G_BCKV * LANES // KV_LORA)),
            pl.BlockSpec((None, tm, LANES), col(G_KR)),
            pl.BlockSpec((1, Q_LORA), row),
            pl.BlockSpec((1, KV_LORA), row),
            pl.BlockSpec((Q_LORA, B_HEADS * QK_DIM), row),
            pl.BlockSpec((KV_LORA, B_HEADS * QK_DIM), row),
            pl.BlockSpec((tm, LANES), lambda b, s: (s, 0)),
            pl.BlockSpec((tm, LANES), lambda b, s: (s, 0)),
        ],
        out_specs=[
            pl.BlockSpec((None, B_HEADS, QK_DIM, tm), lambda b, s: (b, 0, 0, s)),
            pl.BlockSpec((None, B_HEADS, tm, QK_DIM), lambda b, s: (b, 0, s, 0)),
            pl.BlockSpec((None, B_HEADS, D_V, tm), lambda b, s: (b, 0, 0, s)),
        ],
        out_shape=[
            jax.ShapeDtypeStruct((B, B_HEADS, QK_DIM, S), BF16),
            jax.ShapeDtypeStruct((B, B_HEADS, S, QK_DIM), BF16),
            jax.ShapeDtypeStruct((B, B_HEADS, D_V, S), BF16),
        ],
        compiler_params=_params(("parallel", "parallel")),
        name="latent_proj",
    )(proj, proj, proj, q_gain, kv_gain, wq_ext, wkv, cos_tab, sin_tab)


def _mla_kernel(q_ref, k_ref, vt_ref, g_ref, o_ref, s_ref, mrun_ref, mfin_ref, *, nk, tk):
    def fold(x, op):
        out = x[0:SUBLANES]
        for t in range(1, x.shape[0] // SUBLANES):
            out = op(out, x[t * SUBLANES:(t + 1) * SUBLANES])
        return out

    @pl.when(pl.program_id(0) == 0)
    def _():
        s_ref[...] = jnp.zeros(s_ref.shape, F32)
        mfin_ref[...] = jnp.zeros(mfin_ref.shape, F32)

    qt = q_ref[...]
    m_prev = mfin_ref[...]
    l_run = None
    acc = None
    for j in range(nk):
        ks = slice(j * tk, (j + 1) * tk)
        p = jnp.exp2(s_ref[ks, :] - m_prev)
        l_blk = fold(p, jnp.add)
        l_run = l_blk if l_run is None else l_run + l_blk
        pv = jnp.dot(vt_ref[:, ks], p.astype(BF16), preferred_element_type=F32)
        acc = pv if acc is None else acc + pv
        s = jnp.dot(k_ref[ks, :], qt, preferred_element_type=F32)
        s_ref[ks, :] = s
        m_blk = fold(s, jnp.maximum)
        mrun_ref[...] = m_blk if j == 0 else jnp.maximum(mrun_ref[...], m_blk)
    mfin_ref[...] = jnp.max(mrun_ref[...], axis=0, keepdims=True)
    o = (acc / jnp.sum(l_run, axis=0, keepdims=True)).T
    o_ref[...] = (o * _silu(g_ref[...].astype(F32))).astype(o_ref.dtype)


def _mla(q, k, v, proj, B, S, tq, tk):
    nq = S // tq
    n_tiles = B * B_HEADS * nq

    def tile(n):
        return n // (B_HEADS * nq), (n // nq) % B_HEADS, n % nq

    def cur(n):
        return tile(jnp.minimum(n, n_tiles - 1))

    def prev(n):
        return tile(jnp.maximum(n - 1, 0))

    def q_map(n):
        b, h, i = cur(n)
        return b, h, 0, i

    def k_map(n):
        b, h, _ = cur(n)
        return b, h, 0, 0

    def v_map(n):
        b, h, _ = prev(n)
        return b, h, 0, 0

    def g_map(n):
        b, h, i = prev(n)
        return b, i, G_BG + h

    def o_map(n):
        b, h, i = prev(n)
        return b, i, h

    return pl.pallas_call(
        functools.partial(_mla_kernel, nk=S // tk, tk=tk),
        grid=(n_tiles + 1,),
        in_specs=[
            pl.BlockSpec((None, None, QK_DIM, tq), q_map),
            pl.BlockSpec((None, None, S, QK_DIM), k_map),
            pl.BlockSpec((None, None, D_V, S), v_map),
            pl.BlockSpec((None, tq, LANES), g_map),
        ],
        out_specs=pl.BlockSpec((None, tq, D_V), o_map),
        out_shape=jax.ShapeDtypeStruct((B, S, B_WIDTH), BF16),
        scratch_shapes=[pltpu.VMEM((S, tq), F32), pltpu.VMEM((SUBLANES, tq), F32), pltpu.VMEM((1, tq), F32)],
        compiler_params=_params(("arbitrary",)),
        name="latent_attn",
    )(q, k, v, proj)


def _t5_bucket(rel):
    nb = N_BUCKETS // 2
    max_exact = nb // 2
    bucket = jnp.where(rel > 0, nb, 0)
    n = jnp.abs(rel)
    nf = jnp.maximum(n, 1).astype(F32)
    large = max_exact + (jnp.log(nf / max_exact) / math.log(MAX_DISTANCE / max_exact)
                         * (nb - max_exact)).astype(jnp.int32)
    large = jnp.minimum(large, nb - 1)
    return bucket + jnp.where(n < max_exact, n, large)


def _dil_bias_tiles(rel_bias, g, dil, tq, tk):
    qi = jnp.arange(tq)[:, None]
    kj = jnp.arange(tk)[None, :]
    table = rel_bias.astype(F32)[:, g * A_HEADS:(g + 1) * A_HEADS]
    rel = jnp.stack([kj - qi - off for off in (0, DIL_HALF, 2 * DIL_HALF)])
    bucket = _t5_bucket(rel * dil)
    b = jnp.zeros((3, A_HEADS, tq, tk), F32)
    for i in range(N_BUCKETS):
        b = jnp.where((bucket == i)[:, None], table[i][None, :, None, None], b)
    return jnp.where((jnp.abs(rel) <= DIL_HALF)[:, None], b, NEG)


def _dil_kernel(q_ref, k_ref, v_ref, gate_ref, bias_ref, o_ref, t1_ref, t2_ref, k4_ref, v4_ref, acc_ref, lse_ref,
                *, S, tq, tk, unroll):
    g = pl.program_id(2)
    scale = A_HD ** -0.5
    LB = S // DIL_BASE

    def attend(q, k, v, bias):
        s = lax.dot_general(q, k, _NT, preferred_element_type=F32) * scale + bias
        m = jnp.max(s, axis=1, keepdims=True)
        p = jnp.exp(s - m)
        den = jnp.sum(p, axis=1, keepdims=True)
        o = jnp.dot(p.astype(BF16), v, preferred_element_type=F32) / den
        return o, jnp.broadcast_to(m + jnp.log(den), (tq, LANES))

    def to_base_major(dst_ref, src_ref):
        for b in range(DIL_BASE):
            dst_ref[b * LB:(b + 1) * LB, :] = src_ref[pl.ds(b, LB, stride=DIL_BASE), :]

    @pl.when(g == 0)
    def _():
        t1_ref[...] = k_ref[...].astype(F32)
        to_base_major(k4_ref, t1_ref)
        t1_ref[...] = v_ref[...].astype(F32)
        to_base_major(v4_ref, t1_ref)
        nb = S // tq
        tb = tq // DIL_BASE

        def body(n, carry):
            q0 = pl.multiple_of(n * tq, tq)
            ks = pl.multiple_of(jnp.clip(q0 - DIL_HALF, 0, S - tk), DIL_HALF)
            placement = jnp.where(n == 0, 0, jnp.where(n == nb - 1, 2, 1))
            o, lse = attend(q_ref[pl.ds(q0, tq), :], k_ref[pl.ds(ks, tk), :], v_ref[pl.ds(ks, tk), :],
                            bias_ref[placement])
            t1_ref[pl.ds(q0, tq), :] = o
            t2_ref[pl.ds(q0, tq), :] = lse
            for b in range(DIL_BASE):
                rows = pl.ds(pl.multiple_of(b * LB + n * tb, SUBLANES), tb)
                acc_ref[rows, :] = t1_ref[pl.ds(q0 + b, tb, stride=DIL_BASE), :]
                lse_ref[rows, :] = t2_ref[pl.ds(q0 + b, tb, stride=DIL_BASE), :]
            return carry

        lax.fori_loop(0, nb, body, 0, unroll=min(unroll, nb))

    def strided_group(dil):
        e = dil // DIL_BASE
        L = S // dil
        nb = L // tq
        nc = L // DIL_HALF
        wch = tk // DIL_HALF
        qch = tq // DIL_HALF
        t1_ref[...] = q_ref[...].astype(F32)
        to_base_major(t2_ref, t1_ref)

        def rows_of(start, n_rows):
            if e > 1:
                return pl.ds(start, n_rows, stride=e)
            return pl.ds(pl.multiple_of(start, SUBLANES), n_rows)

        def stream(r, carry):
            base = lax.rem(r, DIL_BASE) * LB + lax.div(r, DIL_BASE)

            def chunks(ref):
                return [ref[rows_of(base + c * DIL_HALF * e, DIL_HALF), :].astype(BF16) for c in range(nc)]

            kc, vc = chunks(k4_ref), chunks(v4_ref)
            for n in range(nb):
                c0 = min(max(n * qch - 1, 0), nc - wch)
                placement = 0 if n == 0 else (2 if n == nb - 1 else 1)
                q_rows = rows_of(base + n * tq * e, tq)
                o, lse = attend(t2_ref[q_rows, :].astype(BF16), jnp.concatenate(kc[c0:c0 + wch], axis=0),
                                jnp.concatenate(vc[c0:c0 + wch], axis=0), bias_ref[placement])
                lse_old = lse_ref[q_rows, :]
                m2 = jnp.maximum(lse_old, lse)
                e_old = jnp.exp(lse_old - m2)
                e_new = jnp.exp(lse - m2)
                den = e_old + e_new
                acc_ref[q_rows, :] = (acc_ref[q_rows, :] * e_old + o * e_new) / den
                lse_ref[q_rows, :] = m2 + jnp.log(den)
            return carry

        lax.fori_loop(0, dil, stream, 0, unroll=max(1, min(unroll // nb, dil)))

    for gi in range(1, N_DIL):
        pl.when(g == gi)(functools.partial(strided_group, DIL_PAIRS[gi][1]))

    @pl.when(g == N_DIL - 1)
    def _():
        for b in range(DIL_BASE):
            t1_ref[pl.ds(b, LB, stride=DIL_BASE), :] = acc_ref[b * LB:(b + 1) * LB, :]
        o_ref[...] = (t1_ref[...] * _silu(gate_ref[...].astype(F32))).astype(o_ref.dtype)


def _dilated(proj, bias, B, S, tq, unroll):
    tk = tq + 2 * DIL_HALF
    for _, dil in DIL_PAIRS:
        nb = S // dil // tq
        assert nb >= 2 and nb * tq * dil == S, (S, dil, tq)
        assert dil == 1 or dil % DIL_BASE == 0, dil
    q_group = lambda g: G_AQ[1] * g + (G_AQ[2] - 2 * G_AQ[1]) * (g // 2)
    col = lambda g0: (lambda b, h, g: (b, 0, g0 + h))
    return pl.pallas_call(
        functools.partial(_dil_kernel, S=S, tq=tq, tk=tk, unroll=unroll),
        grid=(B, A_HEADS, N_DIL),
        in_specs=[
            pl.BlockSpec((None, S, LANES), lambda b, h, g: (b, 0, q_group(g) + h)),
            pl.BlockSpec((None, S, LANES), col(G_AK)),
            pl.BlockSpec((None, S, LANES), col(G_AV)),
            pl.BlockSpec((None, S, LANES), col(G_AG)),
            pl.BlockSpec((None, 3, None, tq, tk), lambda b, h, g: (g, 0, h, 0, 0)),
        ],
        out_specs=pl.BlockSpec((None, S, LANES), lambda b, h, g: (b, 0, h)),
        out_shape=jax.ShapeDtypeStruct((B, S, A_WIDTH), BF16),
        scratch_shapes=[pltpu.VMEM((S, LANES), F32) for _ in range(6)],
        compiler_params=_params(("parallel", "arbitrary", "arbitrary")),
        name="dilated_attn",
    )(proj, proj, proj, proj, bias)


def _memkv_kernel(mem_ref, g_ref, w_ref, o_ref):
    x = mem_ref[...]
    ms = jnp.mean(x * x, axis=-1, keepdims=True)
    hn = (x * lax.rsqrt(ms + EPS) * g_ref[...]).astype(BF16)
    o_ref[...] = jnp.dot(hn, w_ref[...], preferred_element_type=F32).astype(o_ref.dtype)


def _memkv(mem, gain, w_bf16):
    B, M, _ = mem.shape
    return pl.pallas_call(
        _memkv_kernel,
        grid=(B,),
        in_specs=[
            pl.BlockSpec((None, M, D_MODEL), lambda b: (b, 0, 0)),
            pl.BlockSpec((1, D_MODEL), lambda b: (0, 0)),
            pl.BlockSpec((D_MODEL, 2 * C_WIDTH), lambda b: (0, 0)),
        ],
        out_specs=pl.BlockSpec((None, M, 2 * C_WIDTH), lambda b: (b, 0, 0)),
        out_shape=jax.ShapeDtypeStruct((B, M, 2 * C_WIDTH), BF16),
        compiler_params=_params(("parallel",)),
        name="mem_kv_proj",
    )(mem, gain, w_bf16)


def _out_kernel(ya_ref, yb_ref, cq_ref, cg_ref, mkv_ref, w_ref, pg_ref, x_ref, o_ref):
    y = jnp.dot(ya_ref[...], w_ref[0:A_WIDTH, :], preferred_element_type=F32)
    y = y + jnp.dot(yb_ref[...], w_ref[A_WIDTH:A_WIDTH + B_WIDTH, :], preferred_element_type=F32)
    scale = C_HD ** -0.5
    ycs = []
    for h in range(C_HEADS):
        cs = slice(h * C_HD, (h + 1) * C_HD)
        s = lax.dot_general(cq_ref[:, cs], mkv_ref[:, cs], _NT, preferred_element_type=F32) * scale
        m = jnp.max(s, axis=1, keepdims=True)
        p = jnp.exp(s - m)
        den = jnp.sum(p, axis=1, keepdims=True)
        mv = mkv_ref[:, C_WIDTH + h * C_HD:C_WIDTH + (h + 1) * C_HD]
        oc = jnp.dot(p.astype(BF16), mv, preferred_element_type=F32) / den
        ycs.append((oc * _silu(cg_ref[:, cs].astype(F32))).astype(BF16))
    y = y + jnp.dot(jnp.concatenate(ycs, axis=1), w_ref[A_WIDTH + B_WIDTH:, :], preferred_element_type=F32)
    ms = jnp.mean(y * y, axis=-1, keepdims=True)
    o_ref[...] = x_ref[...] + y * lax.rsqrt(ms + EPS) * pg_ref[...]


def _outproj(ya, yb, proj, mkv, w_out, post_gain, x, tm):
    B, S, _ = x.shape
    M = mkv.shape[1]
    tok = lambda width, blk: pl.BlockSpec((None, tm, width), lambda b, s: (b, s, blk))
    const = lambda shape: pl.BlockSpec(shape, lambda b, s: (0,) * len(shape))
    return pl.pallas_call(
        _out_kernel,
        grid=(B, S // tm),
        in_specs=[
            tok(A_WIDTH, 0),
            tok(B_WIDTH, 0),
            tok(C_WIDTH, G_CQ * LANES // C_WIDTH),
            tok(C_WIDTH, G_CG * LANES // C_WIDTH),
            pl.BlockSpec((None, M, 2 * C_WIDTH), lambda b, s: (b, 0, 0)),
            const((D_MIX, D_MODEL)),
            const((1, D_MODEL)),
            tok(D_MODEL, 0),
        ],
        out_specs=tok(D_MODEL, 0),
        out_shape=jax.ShapeDtypeStruct((B, S, D_MODEL), x.dtype),
        compiler_params=_params(("parallel", "parallel")),
        name="outproj",
    )(ya, yb, proj, proj, mkv, w_out, post_gain, x)


def _swap_halves(w):
    half = D_ROPE // 2
    return jnp.concatenate([w[..., half:], w[..., :half]], axis=-1)


def _prep_weights(w_in, w_uq, w_ukv, w_mem_kv, w_out):
    o_kr = N_DIL * A_WIDTH + 3 * A_WIDTH + Q_LORA + KV_LORA
    o_bg = o_kr + D_ROPE
    w_in = w_in.astype(BF16)
    kr = w_in[:, o_kr:o_bg]
    w_in_r = jnp.concatenate(
        [w_in[:, :o_kr], w_in[:, o_bg:], kr, _swap_halves(kr),
         jnp.zeros((D_MODEL, D_INP - (G_KR + 1) * LANES), BF16)], axis=1)
    wq = w_uq.reshape(Q_LORA, B_HEADS, D_NOPE + D_ROPE)
    rope = wq[:, :, D_NOPE:]
    wq_ext = jnp.concatenate([wq[:, :, :D_NOPE], rope, _swap_halves(rope)], axis=-1)
    wq_ext = wq_ext.reshape(Q_LORA, B_HEADS * QK_DIM).astype(BF16)
    return w_in_r, wq_ext, w_ukv.astype(BF16), w_mem_kv.astype(BF16), w_out.astype(BF16)


def _rope_tables(S):
    inv = ROPE_BASE ** (-jnp.arange(0, D_ROPE, 2, dtype=F32) / D_ROPE)
    ang = jnp.arange(S, dtype=F32)[:, None] * inv[None, :]
    cos, sin = jnp.cos(ang), jnp.sin(ang)
    zeros = jnp.zeros((S, LANES - D_ROPE), F32)
    return (jnp.concatenate([cos, cos, zeros], axis=1),
            jnp.concatenate([-sin, sin, zeros], axis=1))


def _dil_biases(rel_bias, tq):
    return jnp.stack([_dil_bias_tiles(rel_bias, g, dil, tq, tq + 2 * DIL_HALF)
                      for g, (_, dil) in enumerate(DIL_PAIRS)])


def _layer(x, mem, pre_gain, q_gain, kv_gain, mem_gain, post_gain, biases, weights):
    w_in_r, wq_ext, wkv, wmem, wout = weights
    B, S, _ = x.shape
    t = _tiles(S)
    proj = _inproj(x.reshape(B * S, D_MODEL), pre_gain, w_in_r, t["tm_in"], t["tn_in"], t["rows_in"])
    proj = proj.reshape(B, S, D_INP)
    cos_tab, sin_tab = _rope_tables(S)
    q, k, v = _latent(proj, q_gain, kv_gain, wq_ext, wkv, cos_tab, sin_tab, B, S, t["tm_lat"])
    yb = _mla(q, k, v, proj, B, S, t["tq_mla"], t["tk_mla"])
    ya = _dilated(proj, biases, B, S, t["tq_dil"], t["unroll_dil"])
    mkv = _memkv(mem, mem_gain, wmem)
    return _outproj(ya, yb, proj, mkv, wout, post_gain, x, t["tm_out"])


def kernel(x_prompt, x_sample, mem_prompt, mem_sample, pre_gain, w_in, q_gain, w_uq, kv_gain, w_ukv,
           mem_gain, w_mem_kv, w_out, post_gain, rel_bias):
    depth = pre_gain.shape[0]
    weights = [_prep_weights(w_in[l], w_uq[l], w_ukv[l], w_mem_kv[l], w_out[l]) for l in range(depth)]
    biases = _dil_biases(rel_bias, _tiles(x_prompt.shape[1])["tq_dil"])

    def trunk(x, mem):
        for l in range(depth):
            x = _layer(x, mem, pre_gain[l][None], q_gain[l][None], kv_gain[l][None], mem_gain[l][None],
                       post_gain[l][None], biases, weights[l])
        return x

    return trunk(x_prompt, mem_prompt), trunk(x_sample, mem_sample)
```

```python
import functools
import math

import jax
import jax.numpy as jnp
from jax import lax
from jax.experimental import pallas as pl
from jax.experimental.pallas import tpu as pltpu

D_MODEL = 2048
DIL_PAIRS = ((128, 1), (512, 4), (2048, 16))
N_DIL = 3
DIL_HALF = 64
DIL_BASE = 4
A_HEADS = 4
A_HD = 128
A_WIDTH = A_HEADS * A_HD
B_HEADS = 8
Q_LORA = 512
KV_LORA = 512
D_NOPE = 128
D_ROPE = 64
D_V = 128
B_WIDTH = B_HEADS * D_V
ROPE_BASE = 10000.0
C_HEADS = 4
C_HD = 128
C_WIDTH = C_HEADS * C_HD
D_MIX = A_WIDTH + B_WIDTH + C_WIDTH
N_BUCKETS = 32
MAX_DISTANCE = 1024
EPS = 1e-6
NEG = -1e30

LANES = 128
SUBLANES = 8
VMEM_LIMIT_BYTES = 56 * 1024 * 1024

G_AQ = (0, 4, 8)
G_AK = 12
G_AV = 16
G_AG = 20
G_BCQ = 24
G_BCKV = 28
G_BG = 32
G_CQ = 40
G_CG = 44
G_KR = 48
N_GROUPS = 50
MXU_COLS = 2 * LANES
D_INP = N_GROUPS * LANES
QK_DIM = 2 * LANES

F32 = jnp.float32
BF16 = jnp.bfloat16
_NT = (((1,), (1,)), ((), ()))


def _tiles(S):
    return dict(
        tm_in=min(1024, S),
        rows_in=min(256, S),
        tn_in=5 * MXU_COLS,
        tm_lat=min(1024, S),
        tq_mla=min(512, S),
        tk_mla=min(256, S),
        tq_dil=128,
        unroll_dil=64,
        tm_out=min(512, S),
    )


def _params(sem):
    return pltpu.CompilerParams(dimension_semantics=sem, vmem_limit_bytes=VMEM_LIMIT_BYTES)


def _silu(g):
    return g * (1.0 / (1.0 + jnp.exp(-g)))


def _inproj_kernel(x_ref, g_ref, w_ref, o_ref, h_ref, *, rows):
    @pl.when(pl.program_id(1) == 0)
    def _():
        tm = x_ref.shape[0]
        for r0 in range(0, tm, rows):
            x = x_ref[r0:r0 + rows, :]
            ms = jnp.mean(x * x, axis=-1, keepdims=True)
            h_ref[r0:r0 + rows, :] = (x * lax.rsqrt(ms + EPS) * g_ref[...]).astype(BF16)

    o_ref[...] = jnp.dot(h_ref[...], w_ref[...], preferred_element_type=F32).astype(o_ref.dtype)


def _inproj(x2d, gain, w_bf16, tm, tn, rows):
    T = x2d.shape[0]
    return pl.pallas_call(
        functools.partial(_inproj_kernel, rows=rows),
        grid=(T // tm, D_INP // tn),
        in_specs=[
            pl.BlockSpec((tm, D_MODEL), lambda i, j: (i, 0)),
            pl.BlockSpec((1, D_MODEL), lambda i, j: (0, 0)),
            pl.BlockSpec((D_MODEL, tn), lambda i, j: (0, j)),
        ],
        out_specs=pl.BlockSpec((tm, tn), lambda i, j: (i, j)),
        out_shape=jax.ShapeDtypeStruct((T, D_INP), BF16),
        scratch_shapes=[pltpu.VMEM((tm, D_MODEL), BF16)],
        compiler_params=_params(("parallel", "arbitrary")),
        name="inproj",
    )(x2d, gain, w_bf16)


def _latent_kernel(cq_ref, ckv_ref, kr_ref, qg_ref, kvg_ref, wq_ref, wkv_ref, cos_ref, sin_ref,
                   q_ref, k_ref, v_ref, *, q_scale):
    def rms(c_ref, gain_ref):
        c = c_ref[...].astype(F32)
        ms = jnp.mean(c * c, axis=-1, keepdims=True)
        return (c * lax.rsqrt(ms + EPS) * gain_ref[...]).astype(BF16)

    cos = cos_ref[...]
    sin = sin_ref[...]

    def rotary(g):
        return g * cos + pltpu.roll(g, D_ROPE, 1) * sin

    qf = jnp.dot(rms(cq_ref, qg_ref), wq_ref[...], preferred_element_type=F32)
    kvf = jnp.dot(rms(ckv_ref, kvg_ref), wkv_ref[...], preferred_element_type=F32)
    k_rot = rotary(kr_ref[...].astype(F32)).astype(BF16)
    for h in range(B_HEADS):
        c0 = h * QK_DIM
        q_ref[h, 0:LANES, :] = (qf[:, c0:c0 + LANES] * q_scale).T.astype(BF16)
        q_ref[h, LANES:QK_DIM, :] = (rotary(qf[:, c0 + LANES:c0 + QK_DIM]) * q_scale).T.astype(BF16)
        k_ref[h, :, 0:LANES] = kvf[:, c0:c0 + LANES].astype(BF16)
        k_ref[h, :, LANES:QK_DIM] = k_rot
        v_ref[h, :, :] = kvf[:, c0 + LANES:c0 + QK_DIM].T.astype(BF16)


def _latent(proj, q_gain, kv_gain, wq_ext, wkv, cos_tab, sin_tab, B, S, tm):
    q_scale = (D_NOPE + D_ROPE) ** -0.5 * math.log2(math.e)
    ns = S // tm
    col = lambda g: (lambda b, s: (b, s, g))
    row = lambda b, s: (0, 0)
    return pl.pallas_call(
        functools.partial(_latent_kernel, q_scale=q_scale),
        grid=(B, ns),
        in_specs=[
            pl.BlockSpec((None, tm, Q_LORA), col(G_BCQ * LANES // Q_LORA)),
            pl.BlockSpec((None, tm, KV_LORA), col(G_BCKV * LANES // KV_LORA)),
            pl.BlockSpec((None, tm, LANES), col(G_KR)),
            pl.BlockSpec((1, Q_LORA), row),
            pl.BlockSpec((1, KV_LORA), row),
            pl.BlockSpec((Q_LORA, B_HEADS * QK_DIM), row),
            pl.BlockSpec((KV_LORA, B_HEADS * QK_DIM), row),
            pl.BlockSpec((tm, LANES), lambda b, s: (s, 0)),
            pl.BlockSpec((tm, LANES), lambda b, s: (s, 0)),
        ],
        out_specs=[
            pl.BlockSpec((None, B_HEADS, QK_DIM, tm), lambda b, s: (b, 0, 0, s)),
            pl.BlockSpec((None, B_HEADS, tm, QK_DIM), lambda b, s: (b, 0, s, 0)),
            pl.BlockSpec((None, B_HEADS, D_V, tm), lambda b, s: (b, 0, 0, s)),
        ],
        out_shape=[
            jax.ShapeDtypeStruct((B, B_HEADS, QK_DIM, S), BF16),
            jax.ShapeDtypeStruct((B, B_HEADS, S, QK_DIM), BF16),
            jax.ShapeDtypeStruct((B, B_HEADS, D_V, S), BF16),
        ],
        compiler_params=_params(("parallel", "parallel")),
        name="latent_proj",
    )(proj, proj, proj, q_gain, kv_gain, wq_ext, wkv, cos_tab, sin_tab)


def _mla_kernel(q_ref, k_ref, vt_ref, g_ref, o_ref, s_ref, mrun_ref, mfin_ref, *, nk, tk):
    def fold(x, op):
        out = x[0:SUBLANES]
        for t in range(1, x.shape[0] // SUBLANES):
            out = op(out, x[t * SUBLANES:(t + 1) * SUBLANES])
        return out

    @pl.when(pl.program_id(0) == 0)
    def _():
        s_ref[...] = jnp.zeros(s_ref.shape, F32)
        mfin_ref[...] = jnp.zeros(mfin_ref.shape, F32)

    qt = q_ref[...]
    m_prev = mfin_ref[...]
    l_run = None
    acc = None
    for j in range(nk):
        ks = slice(j * tk, (j + 1) * tk)
        p = jnp.exp2(s_ref[ks, :] - m_prev)
        l_blk = fold(p, jnp.add)
        l_run = l_blk if l_run is None else l_run + l_blk
        pv = jnp.dot(vt_ref[:, ks], p.astype(BF16), preferred_element_type=F32)
        acc = pv if acc is None else acc + pv
        s = jnp.dot(k_ref[ks, :], qt, preferred_element_type=F32)
        s_ref[ks, :] = s
        m_blk = fold(s, jnp.maximum)
        mrun_ref[...] = m_blk if j == 0 else jnp.maximum(mrun_ref[...], m_blk)
    mfin_ref[...] = jnp.max(mrun_ref[...], axis=0, keepdims=True)
    o = (acc / jnp.sum(l_run, axis=0, keepdims=True)).T
    o_ref[...] = (o * _silu(g_ref[...].astype(F32))).astype(o_ref.dtype)


def _mla(q, k, v, proj, B, S, tq, tk):
    nq = S // tq
    n_tiles = B * B_HEADS * nq

    def tile(n):
        return n // (B_HEADS * nq), (n // nq) % B_HEADS, n % nq

    def cur(n):
        return tile(jnp.minimum(n, n_tiles - 1))

    def prev(n):
        return tile(jnp.maximum(n - 1, 0))

    def q_map(n):
        b, h, i = cur(n)
        return b, h, 0, i

    def k_map(n):
        b, h, _ = cur(n)
        return b, h, 0, 0

    def v_map(n):
        b, h, _ = prev(n)
        return b, h, 0, 0

    def g_map(n):
        b, h, i = prev(n)
        return b, i, G_BG + h

    def o_map(n):
        b, h, i = prev(n)
        return b, i, h

    return pl.pallas_call(
        functools.partial(_mla_kernel, nk=S // tk, tk=tk),
        grid=(n_tiles + 1,),
        in_specs=[
            pl.BlockSpec((None, None, QK_DIM, tq), q_map),
            pl.BlockSpec((None, None, S, QK_DIM), k_map),
            pl.BlockSpec((None, None, D_V, S), v_map),
            pl.BlockSpec((None, tq, LANES), g_map),
        ],
        out_specs=pl.BlockSpec((None, tq, D_V), o_map),
        out_shape=jax.ShapeDtypeStruct((B, S, B_WIDTH), BF16),
        scratch_shapes=[pltpu.VMEM((S, tq), F32), pltpu.VMEM((SUBLANES, tq), F32), pltpu.VMEM((1, tq), F32)],
        compiler_params=_params(("arbitrary",)),
        name="latent_attn",
    )(q, k, v, proj)


def _t5_bucket(rel):
    nb = N_BUCKETS // 2
    max_exact = nb // 2
    bucket = jnp.where(rel > 0, nb, 0)
    n = jnp.abs(rel)
    nf = jnp.maximum(n, 1).astype(F32)
    large = max_exact + (jnp.log(nf / max_exact) / math.log(MAX_DISTANCE / max_exact)
                         * (nb - max_exact)).astype(jnp.int32)
    large = jnp.minimum(large, nb - 1)
    return bucket + jnp.where(n < max_exact, n, large)


def _dil_bias_tiles(rel_bias, g, dil, tq, tk):
    qi = jnp.arange(tq)[:, None]
    kj = jnp.arange(tk)[None, :]
    table = rel_bias.astype(F32)[:, g * A_HEADS:(g + 1) * A_HEADS]
    rel = jnp.stack([kj - qi - off for off in (0, DIL_HALF, 2 * DIL_HALF)])
    bucket = _t5_bucket(rel * dil)
    b = jnp.zeros((3, A_HEADS, tq, tk), F32)
    for i in range(N_BUCKETS):
        b = jnp.where((bucket == i)[:, None], table[i][None, :, None, None], b)
    return jnp.where((jnp.abs(rel) <= DIL_HALF)[:, None], b, NEG)


def _dil_kernel(q_ref, k_ref, v_ref, gate_ref, bias_ref, o_ref, t1_ref, t2_ref, k4_ref, v4_ref, acc_ref, lse_ref,
                *, S, tq, tk, unroll):
    g = pl.program_id(2)
    scale = A_HD ** -0.5
    LB = S // DIL_BASE

    def attend(q, k, v, bias):
        s = lax.dot_general(q, k, _NT, preferred_element_type=F32) * scale + bias
        m = jnp.max(s, axis=1, keepdims=True)
        p = jnp.exp(s - m)
        den = jnp.sum(p, axis=1, keepdims=True)
        o = jnp.dot(p.astype(BF16), v, preferred_element_type=F32) / den
        return o, jnp.broadcast_to(m + jnp.log(den), (tq, LANES))

    def to_base_major(dst_ref, src_ref):
        for b in range(DIL_BASE):
            dst_ref[b * LB:(b + 1) * LB, :] = src_ref[pl.ds(b, LB, stride=DIL_BASE), :]

    @pl.when(g == 0)
    def _():
        t1_ref[...] = k_ref[...].astype(F32)
        to_base_major(k4_ref, t1_ref)
        t1_ref[...] = v_ref[...].astype(F32)
        to_base_major(v4_ref, t1_ref)
        nb = S // tq
        tb = tq // DIL_BASE

        def body(n, carry):
            q0 = pl.multiple_of(n * tq, tq)
            ks = pl.multiple_of(jnp.clip(q0 - DIL_HALF, 0, S - tk), DIL_HALF)
            placement = jnp.where(n == 0, 0, jnp.where(n == nb - 1, 2, 1))
            o, lse = attend(q_ref[pl.ds(q0, tq), :], k_ref[pl.ds(ks, tk), :], v_ref[pl.ds(ks, tk), :],
                            bias_ref[placement])
            t1_ref[pl.ds(q0, tq), :] = o
            t2_ref[pl.ds(q0, tq), :] = lse
            for b in range(DIL_BASE):
                rows = pl.ds(pl.multiple_of(b * LB + n * tb, SUBLANES), tb)
                acc_ref[rows, :] = t1_ref[pl.ds(q0 + b, tb, stride=DIL_BASE), :]
                lse_ref[rows, :] = t2_ref[pl.ds(q0 + b, tb, stride=DIL_BASE), :]
            return carry

        lax.fori_loop(0, nb, body, 0, unroll=min(unroll, nb))

    def strided_group(dil):
        e = dil // DIL_BASE
        L = S // dil
        nb = L // tq
        nc = L // DIL_HALF
        wch = tk // DIL_HALF
        qch = tq // DIL_HALF
        t1_ref[...] = q_ref[...].astype(F32)
        to_base_major(t2_ref, t1_ref)

        def rows_of(start, n_rows):
            if e > 1:
                return pl.ds(start, n_rows, stride=e)
            return pl.ds(pl.multiple_of(start, SUBLANES), n_rows)

        def stream(r, carry):
            base = lax.rem(r, DIL_BASE) * LB + lax.div(r, DIL_BASE)

            def chunks(ref):
                return [ref[rows_of(base + c * DIL_HALF * e, DIL_HALF), :].astype(BF16) for c in range(nc)]

            kc, vc = chunks(k4_ref), chunks(v4_ref)
            for n in range(nb):
                c0 = min(max(n * qch - 1, 0), nc - wch)
                placement = 0 if n == 0 else (2 if n == nb - 1 else 1)
                q_rows = rows_of(base + n * tq * e, tq)
                o, lse = attend(t2_ref[q_rows, :].astype(BF16), jnp.concatenate(kc[c0:c0 + wch], axis=0),
                                jnp.concatenate(vc[c0:c0 + wch], axis=0), bias_ref[placement])
                lse_old = lse_ref[q_rows, :]
                m2 = jnp.maximum(lse_old, lse)
                e_old = jnp.exp(lse_old - m2)
                e_new = jnp.exp(lse - m2)
                den = e_old + e_new
                acc_ref[q_rows, :] = (acc_ref[q_rows, :] * e_old + o * e_new) / den
                lse_ref[q_rows, :] = m2 + jnp.log(den)
            return carry

        lax.fori_loop(0, dil, stream, 0, unroll=max(1, min(unroll // nb, dil)))

    for gi in range(1, N_DIL):
        pl.when(g == gi)(functools.partial(strided_group, DIL_PAIRS[gi][1]))

    @pl.when(g == N_DIL - 1)
    def _():
        for b in range(DIL_BASE):
            t1_ref[pl.ds(b, LB, stride=DIL_BASE), :] = acc_ref[b * LB:(b + 1) * LB, :]
        o_ref[...] = (t1_ref[...] * _silu(gate_ref[...].astype(F32))).astype(o_ref.dtype)


def _dilated(proj, bias, B, S, tq, unroll):
    tk = tq + 2 * DIL_HALF
    for _, dil in DIL_PAIRS:
        nb = S // dil // tq
        assert nb >= 2 and nb * tq * dil == S, (S, dil, tq)
        assert dil == 1 or dil % DIL_BASE == 0, dil
    q_group = lambda g: G_AQ[1] * g + (G_AQ[2] - 2 * G_AQ[1]) * (g // 2)
    col = lambda g0: (lambda b, h, g: (b, 0, g0 + h))
    return pl.pallas_call(
        functools.partial(_dil_kernel, S=S, tq=tq, tk=tk, unroll=unroll),
        grid=(B, A_HEADS, N_DIL),
        in_specs=[
            pl.BlockSpec((None, S, LANES), lambda b, h, g: (b, 0, q_group(g) + h)),
            pl.BlockSpec((None, S, LANES), col(G_AK)),
            pl.BlockSpec((None, S, LANES), col(G_AV)),
            pl.BlockSpec((None, S, LANES), col(G_AG)),
            pl.BlockSpec((None, 3, None, tq, tk), lambda b, h, g: (g, 0, h, 0, 0)),
        ],
        out_specs=pl.BlockSpec((None, S, LANES), lambda b, h, g: (b, 0, h)),
        out_shape=jax.ShapeDtypeStruct((B, S, A_WIDTH), BF16),
        scratch_shapes=[pltpu.VMEM((S, LANES), F32) for _ in range(6)],
        compiler_params=_params(("parallel", "arbitrary", "arbitrary")),
        name="dilated_attn",
    )(proj, proj, proj, proj, bias)


def _memkv_kernel(mem_ref, g_ref, w_ref, o_ref):
    x = mem_ref[...]
    ms = jnp.mean(x * x, axis=-1, keepdims=True)
    hn = (x * lax.rsqrt(ms + EPS) * g_ref[...]).astype(BF16)
    o_ref[...] = jnp.dot(hn, w_ref[...], preferred_element_type=F32).astype(o_ref.dtype)


def _memkv(mem, gain, w_bf16):
    B, M, _ = mem.shape
    return pl.pallas_call(
        _memkv_kernel,
        grid=(B,),
        in_specs=[
            pl.BlockSpec((None, M, D_MODEL), lambda b: (b, 0, 0)),
            pl.BlockSpec((1, D_MODEL), lambda b: (0, 0)),
            pl.BlockSpec((D_MODEL, 2 * C_WIDTH), lambda b: (0, 0)),
        ],
        out_specs=pl.BlockSpec((None, M, 2 * C_WIDTH), lambda b: (b, 0, 0)),
        out_shape=jax.ShapeDtypeStruct((B, M, 2 * C_WIDTH), BF16),
        compiler_params=_params(("parallel",)),
        name="mem_kv_proj",
    )(mem, gain, w_bf16)


def _out_kernel(ya_ref, yb_ref, cq_ref, cg_ref, mkv_ref, w_ref, pg_ref, x_ref, o_ref):
    y = jnp.dot(ya_ref[...], w_ref[0:A_WIDTH, :], preferred_element_type=F32)
    y = y + jnp.dot(yb_ref[...], w_ref[A_WIDTH:A_WIDTH + B_WIDTH, :], preferred_element_type=F32)
    scale = C_HD ** -0.5
    ycs = []
    for h in range(C_HEADS):
        cs = slice(h * C_HD, (h + 1) * C_HD)
        s = lax.dot_general(cq_ref[:, cs], mkv_ref[:, cs], _NT, preferred_element_type=F32) * scale
        m = jnp.max(s, axis=1, keepdims=True)
        p = jnp.exp(s - m)
        den = jnp.sum(p, axis=1, keepdims=True)
        mv = mkv_ref[:, C_WIDTH + h * C_HD:C_WIDTH + (h + 1) * C_HD]
        oc = jnp.dot(p.astype(BF16), mv, preferred_element_type=F32) / den
        ycs.append((oc * _silu(cg_ref[:, cs].astype(F32))).astype(BF16))
    y = y + jnp.dot(jnp.concatenate(ycs, axis=1), w_ref[A_WIDTH + B_WIDTH:, :], preferred_element_type=F32)
    ms = jnp.mean(y * y, axis=-1, keepdims=True)
    o_ref[...] = x_ref[...] + y * lax.rsqrt(ms + EPS) * pg_ref[...]


def _outproj(ya, yb, proj, mkv, w_out, post_gain, x, tm):
    B, S, _ = x.shape
    M = mkv.shape[1]
    tok = lambda width, blk: pl.BlockSpec((None, tm, width), lambda b, s: (b, s, blk))
    const = lambda shape: pl.BlockSpec(shape, lambda b, s: (0,) * len(shape))
    return pl.pallas_call(
        _out_kernel,
        grid=(B, S // tm),
        in_specs=[
            tok(A_WIDTH, 0),
            tok(B_WIDTH, 0),
            tok(C_WIDTH, G_CQ * LANES // C_WIDTH),
            tok(C_WIDTH, G_CG * LANES // C_WIDTH),
            pl.BlockSpec((None, M, 2 * C_WIDTH), lambda b, s: (b, 0, 0)),
            const((D_MIX, D_MODEL)),
            const((1, D_MODEL)),
            tok(D_MODEL, 0),
        ],
        out_specs=tok(D_MODEL, 0),
        out_shape=jax.ShapeDtypeStruct((B, S, D_MODEL), x.dtype),
        compiler_params=_params(("parallel", "parallel")),
        name="outproj",
    )(ya, yb, proj, proj, mkv, w_out, post_gain, x)


def _swap_halves(w):
    half = D_ROPE // 2
    return jnp.concatenate([w[..., half:], w[..., :half]], axis=-1)


def _prep_weights(w_in, w_uq, w_ukv, w_mem_kv, w_out):
    o_kr = N_DIL * A_WIDTH + 3 * A_WIDTH + Q_LORA + KV_LORA
    o_bg = o_kr + D_ROPE
    w_in = w_in.astype(BF16)
    kr = w_in[:, o_kr:o_bg]
    w_in_r = jnp.concatenate(
        [w_in[:, :o_kr], w_in[:, o_bg:], kr, _swap_halves(kr),
         jnp.zeros((D_MODEL, D_INP - (G_KR + 1) * LANES), BF16)], axis=1)
    wq = w_uq.reshape(Q_LORA, B_HEADS, D_NOPE + D_ROPE)
    rope = wq[:, :, D_NOPE:]
    wq_ext = jnp.concatenate([wq[:, :, :D_NOPE], rope, _swap_halves(rope)], axis=-1)
    wq_ext = wq_ext.reshape(Q_LORA, B_HEADS * QK_DIM).astype(BF16)
    return w_in_r, wq_ext, w_ukv.astype(BF16), w_mem_kv.astype(BF16), w_out.astype(BF16)


def _rope_tables(S):
    inv = ROPE_BASE ** (-jnp.arange(0, D_ROPE, 2, dtype=F32) / D_ROPE)
    ang = jnp.arange(S, dtype=F32)[:, None] * inv[None, :]
    cos, sin = jnp.cos(ang), jnp.sin(ang)
    zeros = jnp.zeros((S, LANES - D_ROPE), F32)
    return (jnp.concatenate([cos, cos, zeros], axis=1),
            jnp.concatenate([-sin, sin, zeros], axis=1))


def _dil_biases(rel_bias, tq):
    return jnp.stack([_dil_bias_tiles(rel_bias, g, dil, tq, tq + 2 * DIL_HALF)
                      for g, (_, dil) in enumerate(DIL_PAIRS)])


def _layer(x, mem, pre_gain, q_gain, kv_gain, mem_gain, post_gain, biases, weights):
    w_in_r, wq_ext, wkv, wmem, wout = weights
    B, S, _ = x.shape
    t = _tiles(S)
    proj = _inproj(x.reshape(B * S, D_MODEL), pre_gain, w_in_r, t["tm_in"], t["tn_in"], t["rows_in"])
    proj = proj.reshape(B, S, D_INP)
    cos_tab, sin_tab = _rope_tables(S)
    q, k, v = _latent(proj, q_gain, kv_gain, wq_ext, wkv, cos_tab, sin_tab, B, S, t["tm_lat"])
    yb = _mla(q, k, v, proj, B, S, t["tq_mla"], t["tk_mla"])
    ya = _dilated(proj, biases, B, S, t["tq_dil"], t["unroll_dil"])
    mkv = _memkv(mem, mem_gain, wmem)
    return _outproj(ya, yb, proj, mkv, wout, post_gain, x, t["tm_out"])


def kernel(x_prompt, x_sample, mem_prompt, mem_sample, pre_gain, w_in, q_gain, w_uq, kv_gain, w_ukv,
           mem_gain, w_mem_kv, w_out, post_gain, rel_bias):
    depth = pre_gain.shape[0]
    weights = [_prep_weights(w_in[l], w_uq[l], w_ukv[l], w_mem_kv[l], w_out[l]) for l in range(depth)]
    biases = _dil_biases(rel_bias, _tiles(x_prompt.shape[1])["tq_dil"])

    def trunk(x, mem):
        for l in range(depth):
            x = _layer(x, mem, pre_gain[l][None], q_gain[l][None], kv_gain[l][None], mem_gain[l][None],
                       post_gain[l][None], biases, weights[l])
        return x

    return trunk(x_prompt, mem_prompt), trunk(x_sample, mem_sample)
```

```python
import functools
import math

import jax
import jax.numpy as jnp
from jax import lax
from jax.experimental import pallas as pl
from jax.experimental.pallas import tpu as pltpu

D_MODEL = 2048
DIL_PAIRS = ((128, 1), (512, 4), (2048, 16))
N_DIL = 3
DIL_HALF = 64
DIL_BASE = 4
A_HEADS = 4
A_HD = 128
A_WIDTH = A_HEADS * A_HD
B_HEADS = 8
Q_LORA = 512
KV_LORA = 512
D_NOPE = 128
D_ROPE = 64
D_V = 128
B_WIDTH = B_HEADS * D_V
ROPE_BASE = 10000.0
C_HEADS = 4
C_HD = 128
C_WIDTH = C_HEADS * C_HD
D_MIX = A_WIDTH + B_WIDTH + C_WIDTH
N_BUCKETS = 32
MAX_DISTANCE = 1024
EPS = 1e-6
NEG = -1e30

LANES = 128
SUBLANES = 8
VMEM_LIMIT_BYTES = 56 * 1024 * 1024

G_AQ = (0, 4, 8)
G_AK = 12
G_AV = 16
G_AG = 20
G_BCQ = 24
G_BCKV = 28
G_BG = 32
G_CQ = 40
G_CG = 44
G_KR = 48
N_GROUPS = 50
MXU_COLS = 2 * LANES
D_INP = N_GROUPS * LANES
QK_DIM = 2 * LANES

F32 = jnp.float32
BF16 = jnp.bfloat16
_NT = (((1,), (1,)), ((), ()))


def _tiles(S):
    return dict(
        tm_in=min(1024, S),
        rows_in=min(256, S),
        tn_in=5 * MXU_COLS,
        tm_lat=min(1024, S),
        tq_mla=min(512, S),
        tk_mla=min(256, S),
        tq_dil=128,
        unroll_dil=64,
        tm_out=min(512, S),
    )


def _params(sem):
    return pltpu.CompilerParams(dimension_semantics=sem, vmem_limit_bytes=VMEM_LIMIT_BYTES)


def _silu(g):
    return g * (1.0 / (1.0 + jnp.exp(-g)))


def _inproj_kernel(x_ref, g_ref, w_ref, o_ref, h_ref, *, rows):
    @pl.when(pl.program_id(1) == 0)
    def _():
        tm = x_ref.shape[0]
        for r0 in range(0, tm, rows):
            x = x_ref[r0:r0 + rows, :]
            ms = jnp.mean(x * x, axis=-1, keepdims=True)
            h_ref[r0:r0 + rows, :] = (x * lax.rsqrt(ms + EPS) * g_ref[...]).astype(BF16)

    o_ref[...] = jnp.dot(h_ref[...], w_ref[...], preferred_element_type=F32).astype(o_ref.dtype)


def _inproj(x2d, gain, w_bf16, tm, tn, rows):
    T = x2d.shape[0]
    return pl.pallas_call(
        functools.partial(_inproj_kernel, rows=rows),
        grid=(T // tm, D_INP // tn),
        in_specs=[
            pl.BlockSpec((tm, D_MODEL), lambda i, j: (i, 0)),
            pl.BlockSpec((1, D_MODEL), lambda i, j: (0, 0)),
            pl.BlockSpec((D_MODEL, tn), lambda i, j: (0, j)),
        ],
        out_specs=pl.BlockSpec((tm, tn), lambda i, j: (i, j)),
        out_shape=jax.ShapeDtypeStruct((T, D_INP), BF16),
        scratch_shapes=[pltpu.VMEM((tm, D_MODEL), BF16)],
        compiler_params=_params(("parallel", "arbitrary")),
        name="inproj",
    )(x2d, gain, w_bf16)


def _latent_kernel(cq_ref, ckv_ref, kr_ref, qg_ref, kvg_ref, wq_ref, wkv_ref, cos_ref, sin_ref,
                   q_ref, k_ref, v_ref, *, q_scale):
    def rms(c_ref, gain_ref):
        c = c_ref[...].astype(F32)
        ms = jnp.mean(c * c, axis=-1, keepdims=True)
        return (c * lax.rsqrt(ms + EPS) * gain_ref[...]).astype(BF16)

    cos = cos_ref[...]
    sin = sin_ref[...]

    def rotary(g):
        return g * cos + pltpu.roll(g, D_ROPE, 1) * sin

    qf = jnp.dot(rms(cq_ref, qg_ref), wq_ref[...], preferred_element_type=F32)
    kvf = jnp.dot(rms(ckv_ref, kvg_ref), wkv_ref[...], preferred_element_type=F32)
    k_rot = rotary(kr_ref[...].astype(F32)).astype(BF16)
    for h in range(B_HEADS):
        c0 = h * QK_DIM
        q_ref[h, 0:LANES, :] = (qf[:, c0:c0 + LANES] * q_scale).T.astype(BF16)
        q_ref[h, LANES:QK_DIM, :] = (rotary(qf[:, c0 + LANES:c0 + QK_DIM]) * q_scale).T.astype(BF16)
        k_ref[h, :, 0:LANES] = kvf[:, c0:c0 + LANES].astype(BF16)
        k_ref[h, :, LANES:QK_DIM] = k_rot
        v_ref[h, :, :] = kvf[:, c0 + LANES:c0 + QK_DIM].T.astype(BF16)


def _latent(proj, q_gain, kv_gain, wq_ext, wkv, cos_tab, sin_tab, B, S, tm):
    q_scale = (D_NOPE + D_ROPE) ** -0.5 * math.log2(math.e)
    ns = S // tm
    col = lambda g: (lambda b, s: (b, s, g))
    row = lambda b, s: (0, 0)
    return pl.pallas_call(
        functools.partial(_latent_kernel, q_scale=q_scale),
        grid=(B, ns),
        in_specs=[
            pl.BlockSpec((None, tm, Q_LORA), col(G_BCQ * LANES // Q_LORA)),
            pl.BlockSpec((None, tm, KV_LORA), col(G_BCKV * LANES // KV_LORA)),
            pl.BlockSpec((None, tm, LANES), col(G_KR)),
            pl.BlockSpec((1, Q_LORA), row),
            pl.BlockSpec((1, KV_LORA), row),
            pl.BlockSpec((Q_LORA, B_HEADS * QK_DIM), row),
            pl.BlockSpec((KV_LORA, B_HEADS * QK_DIM), row),
            pl.BlockSpec((tm, LANES), lambda b, s: (s, 0)),
            pl.BlockSpec((tm, LANES), lambda b, s: (s, 0)),
        ],
        out_specs=[
            pl.BlockSpec((None, B_HEADS, QK_DIM, tm), lambda b, s: (b, 0, 0, s)),
            pl.BlockSpec((None, B_HEADS, tm, QK_DIM), lambda b, s: (b, 0, s, 0)),
            pl.BlockSpec((None, B_HEADS, D_V, tm), lambda b, s: (b, 0, 0, s)),
        ],
        out_shape=[
            jax.ShapeDtypeStruct((B, B_HEADS, QK_DIM, S), BF16),
            jax.ShapeDtypeStruct((B, B_HEADS, S, QK_DIM), BF16),
            jax.ShapeDtypeStruct((B, B_HEADS, D_V, S), BF16),
        ],
        compiler_params=_params(("parallel", "parallel")),
        name="latent_proj",
    )(proj, proj, proj, q_gain, kv_gain, wq_ext, wkv, cos_tab, sin_tab)


def _mla_kernel(q_ref, k_ref, vt_ref, g_ref, o_ref, s_ref, mrun_ref, mfin_ref, *, nk, tk):
    def fold(x, op):
        out = x[0:SUBLANES]
        for t in range(1, x.shape[0] // SUBLANES):
            out = op(out, x[t * SUBLANES:(t + 1) * SUBLANES])
        return out

    @pl.when(pl.program_id(0) == 0)
    def _():
        s_ref[...] = jnp.zeros(s_ref.shape, F32)
        mfin_ref[...] = jnp.zeros(mfin_ref.shape, F32)

    qt = q_ref[...]
    m_prev = mfin_ref[...]
    l_run = None
    acc = None
    for j in range(nk):
        ks = slice(j * tk, (j + 1) * tk)
        p = jnp.exp2(s_ref[ks, :] - m_prev)
        l_blk = fold(p, jnp.add)
        l_run = l_blk if l_run is None else l_run + l_blk
        pv = jnp.dot(vt_ref[:, ks], p.astype(BF16), preferred_element_type=F32)
        acc = pv if acc is None else acc + pv
        s = jnp.dot(k_ref[ks, :], qt, preferred_element_type=F32)
        s_ref[ks, :] = s
        m_blk = fold(s, jnp.maximum)
        mrun_ref[...] = m_blk if j == 0 else jnp.maximum(mrun_ref[...], m_blk)
    mfin_ref[...] = jnp.max(mrun_ref[...], axis=0, keepdims=True)
    o = (acc / jnp.sum(l_run, axis=0, keepdims=True)).T
    o_ref[...] = (o * _silu(g_ref[...].astype(F32))).astype(o_ref.dtype)


def _mla(q, k, v, proj, B, S, tq, tk):
    nq = S // tq
    n_tiles = B * B_HEADS * nq

    assert nq & (nq - 1) == 0 and B_HEADS & (B_HEADS - 1) == 0
    sh_q = nq.bit_length() - 1
    sh_h = B_HEADS.bit_length() - 1

    def tile(n):
        return (lax.shift_right_logical(n, sh_q + sh_h),
                lax.shift_right_logical(n, sh_q) & (B_HEADS - 1), n & (nq - 1))

    def cur(n):
        return tile(jnp.minimum(n, n_tiles - 1))

    def prev(n):
        return tile(jnp.maximum(n - 1, 0))

    def q_map(n):
        b, h, i = cur(n)
        return b, h, 0, i

    def k_map(n):
        b, h, _ = cur(n)
        return b, h, 0, 0

    def v_map(n):
        b, h, _ = prev(n)
        return b, h, 0, 0

    def g_map(n):
        b, h, i = prev(n)
        return b, i, G_BG + h

    def o_map(n):
        b, h, i = prev(n)
        return b, i, h

    return pl.pallas_call(
        functools.partial(_mla_kernel, nk=S // tk, tk=tk),
        grid=(n_tiles + 1,),
        in_specs=[
            pl.BlockSpec((None, None, QK_DIM, tq), q_map),
            pl.BlockSpec((None, None, S, QK_DIM), k_map),
            pl.BlockSpec((None, None, D_V, S), v_map),
            pl.BlockSpec((None, tq, LANES), g_map),
        ],
        out_specs=pl.BlockSpec((None, tq, D_V), o_map),
        out_shape=jax.ShapeDtypeStruct((B, S, B_WIDTH), BF16),
        scratch_shapes=[pltpu.VMEM((S, tq), F32), pltpu.VMEM((SUBLANES, tq), F32), pltpu.VMEM((1, tq), F32)],
        compiler_params=_params(("arbitrary",)),
        name="latent_attn",
    )(q, k, v, proj)


def _t5_bucket(rel):
    nb = N_BUCKETS // 2
    max_exact = nb // 2
    bucket = jnp.where(rel > 0, nb, 0)
    n = jnp.abs(rel)
    nf = jnp.maximum(n, 1).astype(F32)
    large = max_exact + (jnp.log(nf / max_exact) / math.log(MAX_DISTANCE / max_exact)
                         * (nb - max_exact)).astype(jnp.int32)
    large = jnp.minimum(large, nb - 1)
    return bucket + jnp.where(n < max_exact, n, large)


def _dil_bias_tiles(rel_bias, g, dil, tq, tk):
    qi = jnp.arange(tq)[:, None]
    kj = jnp.arange(tk)[None, :]
    table = rel_bias.astype(F32)[:, g * A_HEADS:(g + 1) * A_HEADS]
    rel = jnp.stack([kj - qi - off for off in (0, DIL_HALF, 2 * DIL_HALF)])
    bucket = _t5_bucket(rel * dil)
    b = jnp.zeros((3, A_HEADS, tq, tk), F32)
    for i in range(N_BUCKETS):
        b = jnp.where((bucket == i)[:, None], table[i][None, :, None, None], b)
    return jnp.where((jnp.abs(rel) <= DIL_HALF)[:, None], b, NEG)


def _dil_kernel(q_ref, k_ref, v_ref, gate_ref, bias_ref, o_ref, t1_ref, t2_ref, k4_ref, v4_ref, acc_ref, lse_ref,
                *, S, tq, tk, unroll):
    g = pl.program_id(2)
    scale = A_HD ** -0.5
    LB = S // DIL_BASE

    def attend(q, k, v, bias):
        s = lax.dot_general(q, k, _NT, preferred_element_type=F32) * scale + bias
        m = jnp.max(s, axis=1, keepdims=True)
        p = jnp.exp(s - m)
        den = jnp.sum(p, axis=1, keepdims=True)
        o = jnp.dot(p.astype(BF16), v, preferred_element_type=F32) / den
        return o, jnp.broadcast_to(m + jnp.log(den), (tq, LANES))

    def to_base_major(dst_ref, src_ref):
        for b in range(DIL_BASE):
            dst_ref[b * LB:(b + 1) * LB, :] = src_ref[pl.ds(b, LB, stride=DIL_BASE), :]

    @pl.when(g == 0)
    def _():
        t1_ref[...] = k_ref[...].astype(F32)
        to_base_major(k4_ref, t1_ref)
        t1_ref[...] = v_ref[...].astype(F32)
        to_base_major(v4_ref, t1_ref)
        nb = S // tq
        tb = tq // DIL_BASE

        def body(n, carry):
            q0 = pl.multiple_of(n * tq, tq)
            ks = pl.multiple_of(jnp.clip(q0 - DIL_HALF, 0, S - tk), DIL_HALF)
            placement = jnp.where(n == 0, 0, jnp.where(n == nb - 1, 2, 1))
            o, lse = attend(q_ref[pl.ds(q0, tq), :], k_ref[pl.ds(ks, tk), :], v_ref[pl.ds(ks, tk), :],
                            bias_ref[placement])
            t1_ref[pl.ds(q0, tq), :] = o
            t2_ref[pl.ds(q0, tq), :] = lse
            for b in range(DIL_BASE):
                rows = pl.ds(pl.multiple_of(b * LB + n * tb, SUBLANES), tb)
                acc_ref[rows, :] = t1_ref[pl.ds(q0 + b, tb, stride=DIL_BASE), :]
                lse_ref[rows, :] = t2_ref[pl.ds(q0 + b, tb, stride=DIL_BASE), :]
            return carry

        lax.fori_loop(0, nb, body, 0, unroll=min(unroll, nb))

    def strided_group(dil):
        e = dil // DIL_BASE
        L = S // dil
        nb = L // tq
        nc = L // DIL_HALF
        wch = tk // DIL_HALF
        qch = tq // DIL_HALF
        t1_ref[...] = q_ref[...].astype(F32)
        to_base_major(t2_ref, t1_ref)

        def rows_of(start, n_rows):
            if e > 1:
                return pl.ds(start, n_rows, stride=e)
            return pl.ds(pl.multiple_of(start, SUBLANES), n_rows)

        def stream(r, carry):
            base = lax.rem(r, DIL_BASE) * LB + lax.div(r, DIL_BASE)

            def chunks(ref):
                return [ref[rows_of(base + c * DIL_HALF * e, DIL_HALF), :].astype(BF16) for c in range(nc)]

            kc, vc = chunks(k4_ref), chunks(v4_ref)
            for n in range(nb):
                c0 = min(max(n * qch - 1, 0), nc - wch)
                placement = 0 if n == 0 else (2 if n == nb - 1 else 1)
                q_rows = rows_of(base + n * tq * e, tq)
                o, lse = attend(t2_ref[q_rows, :].astype(BF16), jnp.concatenate(kc[c0:c0 + wch], axis=0),
                                jnp.concatenate(vc[c0:c0 + wch], axis=0), bias_ref[placement])
                lse_old = lse_ref[q_rows, :]
                m2 = jnp.maximum(lse_old, lse)
                e_old = jnp.exp(lse_old - m2)
                e_new = jnp.exp(lse - m2)
                den = e_old + e_new
                acc_ref[q_rows, :] = (acc_ref[q_rows, :] * e_old + o * e_new) / den
                lse_ref[q_rows, :] = m2 + jnp.log(den)
            return carry

        lax.fori_loop(0, dil, stream, 0, unroll=max(1, min(unroll // nb, dil)))

    for gi in range(1, N_DIL):
        pl.when(g == gi)(functools.partial(strided_group, DIL_PAIRS[gi][1]))

    @pl.when(g == N_DIL - 1)
    def _():
        for b in range(DIL_BASE):
            t1_ref[pl.ds(b, LB, stride=DIL_BASE), :] = acc_ref[b * LB:(b + 1) * LB, :]
        o_ref[...] = (t1_ref[...] * _silu(gate_ref[...].astype(F32))).astype(o_ref.dtype)


def _dilated(proj, bias, B, S, tq, unroll):
    tk = tq + 2 * DIL_HALF
    for _, dil in DIL_PAIRS:
        nb = S // dil // tq
        assert nb >= 2 and nb * tq * dil == S, (S, dil, tq)
        assert dil == 1 or dil % DIL_BASE == 0, dil
    q_group = lambda g: G_AQ[1] * g + (G_AQ[2] - 2 * G_AQ[1]) * (g // 2)
    col = lambda g0: (lambda b, h, g: (b, 0, g0 + h))
    return pl.pallas_call(
        functools.partial(_dil_kernel, S=S, tq=tq, tk=tk, unroll=unroll),
        grid=(B, A_HEADS, N_DIL),
        in_specs=[
            pl.BlockSpec((None, S, LANES), lambda b, h, g: (b, 0, q_group(g) + h)),
            pl.BlockSpec((None, S, LANES), col(G_AK)),
            pl.BlockSpec((None, S, LANES), col(G_AV)),
            pl.BlockSpec((None, S, LANES), col(G_AG)),
            pl.BlockSpec((None, 3, None, tq, tk), lambda b, h, g: (g, 0, h, 0, 0)),
        ],
        out_specs=pl.BlockSpec((None, S, LANES), lambda b, h, g: (b, 0, h)),
        out_shape=jax.ShapeDtypeStruct((B, S, A_WIDTH), BF16),
        scratch_shapes=[pltpu.VMEM((S, LANES), F32) for _ in range(6)],
        compiler_params=_params(("parallel", "arbitrary", "arbitrary")),
        name="dilated_attn",
    )(proj, proj, proj, proj, bias)


def _memkv_kernel(mem_ref, g_ref, w_ref, o_ref):
    x = mem_ref[...]
    ms = jnp.mean(x * x, axis=-1, keepdims=True)
    hn = (x * lax.rsqrt(ms + EPS) * g_ref[...]).astype(BF16)
    o_ref[...] = jnp.dot(hn, w_ref[...], preferred_element_type=F32).astype(o_ref.dtype)


def _memkv(mem, gain, w_bf16):
    B, M, _ = mem.shape
    return pl.pallas_call(
        _memkv_kernel,
        grid=(B,),
        in_specs=[
            pl.BlockSpec((None, M, D_MODEL), lambda b: (b, 0, 0)),
            pl.BlockSpec((1, D_MODEL), lambda b: (0, 0)),
            pl.BlockSpec((D_MODEL, 2 * C_WIDTH), lambda b: (0, 0)),
        ],
        out_specs=pl.BlockSpec((None, M, 2 * C_WIDTH), lambda b: (b, 0, 0)),
        out_shape=jax.ShapeDtypeStruct((B, M, 2 * C_WIDTH), BF16),
        compiler_params=_params(("parallel",)),
        name="mem_kv_proj",
    )(mem, gain, w_bf16)


def _out_kernel(ya_ref, yb_ref, cq_ref, cg_ref, mkv_ref, w_ref, pg_ref, x_ref, o_ref):
    y = jnp.dot(ya_ref[...], w_ref[0:A_WIDTH, :], preferred_element_type=F32)
    y = y + jnp.dot(yb_ref[...], w_ref[A_WIDTH:A_WIDTH + B_WIDTH, :], preferred_element_type=F32)
    scale = C_HD ** -0.5
    ycs = []
    for h in range(C_HEADS):
        cs = slice(h * C_HD, (h + 1) * C_HD)
        s = lax.dot_general(cq_ref[:, cs], mkv_ref[:, cs], _NT, preferred_element_type=F32) * scale
        m = jnp.max(s, axis=1, keepdims=True)
        p = jnp.exp(s - m)
        den = jnp.sum(p, axis=1, keepdims=True)
        mv = mkv_ref[:, C_WIDTH + h * C_HD:C_WIDTH + (h + 1) * C_HD]
        oc = jnp.dot(p.astype(BF16), mv, preferred_element_type=F32) / den
        ycs.append((oc * _silu(cg_ref[:, cs].astype(F32))).astype(BF16))
    y = y + jnp.dot(jnp.concatenate(ycs, axis=1), w_ref[A_WIDTH + B_WIDTH:, :], preferred_element_type=F32)
    ms = jnp.mean(y * y, axis=-1, keepdims=True)
    o_ref[...] = x_ref[...] + y * lax.rsqrt(ms + EPS) * pg_ref[...]


def _outproj(ya, yb, proj, mkv, w_out, post_gain, x, tm):
    B, S, _ = x.shape
    M = mkv.shape[1]
    tok = lambda width, blk: pl.BlockSpec((None, tm, width), lambda b, s: (b, s, blk))
    const = lambda shape: pl.BlockSpec(shape, lambda b, s: (0,) * len(shape))
    return pl.pallas_call(
        _out_kernel,
        grid=(B, S // tm),
        in_specs=[
            tok(A_WIDTH, 0),
            tok(B_WIDTH, 0),
            tok(C_WIDTH, G_CQ * LANES // C_WIDTH),
            tok(C_WIDTH, G_CG * LANES // C_WIDTH),
            pl.BlockSpec((None, M, 2 * C_WIDTH), lambda b, s: (b, 0, 0)),
            const((D_MIX, D_MODEL)),
            const((1, D_MODEL)),
            tok(D_MODEL, 0),
        ],
        out_specs=tok(D_MODEL, 0),
        out_shape=jax.ShapeDtypeStruct((B, S, D_MODEL), x.dtype),
        compiler_params=_params(("parallel", "parallel")),
        name="outproj",
    )(ya, yb, proj, proj, mkv, w_out, post_gain, x)


def _swap_halves(w):
    half = D_ROPE // 2
    return jnp.concatenate([w[..., half:], w[..., :half]], axis=-1)


def _prep_weights(w_in, w_uq, w_ukv, w_mem_kv, w_out):
    o_kr = N_DIL * A_WIDTH + 3 * A_WIDTH + Q_LORA + KV_LORA
    o_bg = o_kr + D_ROPE
    w_in = w_in.astype(BF16)
    kr = w_in[:, o_kr:o_bg]
    w_in_r = jnp.concatenate(
        [w_in[:, :o_kr], w_in[:, o_bg:], kr, _swap_halves(kr),
         jnp.zeros((D_MODEL, D_INP - (G_KR + 1) * LANES), BF16)], axis=1)
    wq = w_uq.reshape(Q_LORA, B_HEADS, D_NOPE + D_ROPE)
    rope = wq[:, :, D_NOPE:]
    wq_ext = jnp.concatenate([wq[:, :, :D_NOPE], rope, _swap_halves(rope)], axis=-1)
    wq_ext = wq_ext.reshape(Q_LORA, B_HEADS * QK_DIM).astype(BF16)
    return w_in_r, wq_ext, w_ukv.astype(BF16), w_mem_kv.astype(BF16), w_out.astype(BF16)


def _rope_tables(S):
    inv = ROPE_BASE ** (-jnp.arange(0, D_ROPE, 2, dtype=F32) / D_ROPE)
    ang = jnp.arange(S, dtype=F32)[:, None] * inv[None, :]
    cos, sin = jnp.cos(ang), jnp.sin(ang)
    zeros = jnp.zeros((S, LANES - D_ROPE), F32)
    return (jnp.concatenate([cos, cos, zeros], axis=1),
            jnp.concatenate([-sin, sin, zeros], axis=1))


def _dil_biases(rel_bias, tq):
    return jnp.stack([_dil_bias_tiles(rel_bias, g, dil, tq, tq + 2 * DIL_HALF)
                      for g, (_, dil) in enumerate(DIL_PAIRS)])


def _layer(x, mem, pre_gain, q_gain, kv_gain, mem_gain, post_gain, biases, weights):
    w_in_r, wq_ext, wkv, wmem, wout = weights
    B, S, _ = x.shape
    t = _tiles(S)
    proj = _inproj(x.reshape(B * S, D_MODEL), pre_gain, w_in_r, t["tm_in"], t["tn_in"], t["rows_in"])
    proj = proj.reshape(B, S, D_INP)
    cos_tab, sin_tab = _rope_tables(S)
    q, k, v = _latent(proj, q_gain, kv_gain, wq_ext, wkv, cos_tab, sin_tab, B, S, t["tm_lat"])
    yb = _mla(q, k, v, proj, B, S, t["tq_mla"], t["tk_mla"])
    ya = _dilated(proj, biases, B, S, t["tq_dil"], t["unroll_dil"])
    mkv = _memkv(mem, mem_gain, wmem)
    return _outproj(ya, yb, proj, mkv, wout, post_gain, x, t["tm_out"])


def kernel(x_prompt, x_sample, mem_prompt, mem_sample, pre_gain, w_in, q_gain, w_uq, kv_gain, w_ukv,
           mem_gain, w_mem_kv, w_out, post_gain, rel_bias):
    depth = pre_gain.shape[0]
    weights = [_prep_weights(w_in[l], w_uq[l], w_ukv[l], w_mem_kv[l], w_out[l]) for l in range(depth)]
    biases = _dil_biases(rel_bias, _tiles(x_prompt.shape[1])["tq_dil"])

    def trunk(x, mem):
        for l in range(depth):
            x = _layer(x, mem, pre_gain[l][None], q_gain[l][None], kv_gain[l][None], mem_gain[l][None],
                       post_gain[l][None], biases, weights[l])
        return x

    return trunk(x_prompt, mem_prompt), trunk(x_sample, mem_sample)
```

```python
import functools
import math

import jax
import jax.numpy as jnp
from jax import lax
from jax.experimental import pallas as pl
from jax.experimental.pallas import tpu as pltpu

D_MODEL = 2048
DIL_PAIRS = ((128, 1), (512, 4), (2048, 16))
N_DIL = 3
DIL_HALF = 64
DIL_BASE = 4
A_HEADS = 4
A_HD = 128
A_WIDTH = A_HEADS * A_HD
B_HEADS = 8
Q_LORA = 512
KV_LORA = 512
D_NOPE = 128
D_ROPE = 64
D_V = 128
B_WIDTH = B_HEADS * D_V
ROPE_BASE = 10000.0
C_HEADS = 4
C_HD = 128
C_WIDTH = C_HEADS * C_HD
D_MIX = A_WIDTH + B_WIDTH + C_WIDTH
N_BUCKETS = 32
MAX_DISTANCE = 1024
EPS = 1e-6
NEG = -1e30

LANES = 128
SUBLANES = 8
VMEM_LIMIT_BYTES = 56 * 1024 * 1024

G_AQ = (0, 4, 8)
G_AK = 12
G_AV = 16
G_AG = 20
G_BCQ = 24
G_BCKV = 28
G_BG = 32
G_CQ = 40
G_CG = 44
G_KR = 48
N_GROUPS = 50
MXU_COLS = 2 * LANES
D_INP = N_GROUPS * LANES
QK_DIM = 2 * LANES

F32 = jnp.float32
BF16 = jnp.bfloat16
_NT = (((1,), (1,)), ((), ()))


def _tiles(S):
    return dict(
        tm_in=min(1024, S),
        rows_in=min(256, S),
        tn_in=5 * MXU_COLS,
        tm_lat=min(1024, S),
        tq_mla=min(512, S),
        tk_mla=min(256, S),
        tq_dil=128,
        unroll_dil=64,
        tm_out=min(512, S),
    )


def _params(sem):
    return pltpu.CompilerParams(dimension_semantics=sem, vmem_limit_bytes=VMEM_LIMIT_BYTES)


def _silu(g):
    return g * (1.0 / (1.0 + jnp.exp(-g)))


def _inproj_kernel(x_ref, g_ref, w_ref, o_ref, h_ref, *, rows):
    @pl.when(pl.program_id(1) == 0)
    def _():
        tm = x_ref.shape[0]
        for r0 in range(0, tm, rows):
            x = x_ref[r0:r0 + rows, :]
            ms = jnp.mean(x * x, axis=-1, keepdims=True)
            h_ref[r0:r0 + rows, :] = (x * lax.rsqrt(ms + EPS) * g_ref[...]).astype(BF16)

    o_ref[...] = jnp.dot(h_ref[...], w_ref[...], preferred_element_type=F32).astype(o_ref.dtype)


def _inproj(x2d, gain, w_bf16, tm, tn, rows):
    T = x2d.shape[0]
    return pl.pallas_call(
        functools.partial(_inproj_kernel, rows=rows),
        grid=(T // tm, D_INP // tn),
        in_specs=[
            pl.BlockSpec((tm, D_MODEL), lambda i, j: (i, 0)),
            pl.BlockSpec((1, D_MODEL), lambda i, j: (0, 0)),
            pl.BlockSpec((D_MODEL, tn), lambda i, j: (0, j)),
        ],
        out_specs=pl.BlockSpec((tm, tn), lambda i, j: (i, j)),
        out_shape=jax.ShapeDtypeStruct((T, D_INP), BF16),
        scratch_shapes=[pltpu.VMEM((tm, D_MODEL), BF16)],
        compiler_params=_params(("parallel", "arbitrary")),
        name="inproj",
    )(x2d, gain, w_bf16)


def _latent_kernel(cq_ref, ckv_ref, kr_ref, qg_ref, kvg_ref, wq_ref, wkv_ref, cos_ref, sin_ref,
                   q_ref, k_ref, v_ref, *, q_scale):
    def rms(c_ref, gain_ref):
        c = c_ref[...].astype(F32)
        ms = jnp.mean(c * c, axis=-1, keepdims=True)
        return (c * lax.rsqrt(ms + EPS) * gain_ref[...]).astype(BF16)

    cos = cos_ref[...]
    sin = sin_ref[...]

    def rotary(g):
        return g * cos + pltpu.roll(g, D_ROPE, 1) * sin

    qf = jnp.dot(rms(cq_ref, qg_ref), wq_ref[...], preferred_element_type=F32)
    kvf = jnp.dot(rms(ckv_ref, kvg_ref), wkv_ref[...], preferred_element_type=F32)
    k_rot = rotary(kr_ref[...].astype(F32)).astype(BF16)
    for h in range(B_HEADS):
        c0 = h * QK_DIM
        q_ref[h, 0:LANES, :] = (qf[:, c0:c0 + LANES] * q_scale).T.astype(BF16)
        q_ref[h, LANES:QK_DIM, :] = (rotary(qf[:, c0 + LANES:c0 + QK_DIM]) * q_scale).T.astype(BF16)
        k_ref[h, :, 0:LANES] = kvf[:, c0:c0 + LANES].astype(BF16)
        k_ref[h, :, LANES:QK_DIM] = k_rot
        v_ref[h, :, :] = kvf[:, c0 + LANES:c0 + QK_DIM].T.astype(BF16)


def _latent(proj, q_gain, kv_gain, wq_ext, wkv, cos_tab, sin_tab, B, S, tm):
    q_scale = (D_NOPE + D_ROPE) ** -0.5 * math.log2(math.e)
    ns = S // tm
    col = lambda g: (lambda b, s: (b, s, g))
    row = lambda b, s: (0, 0)
    return pl.pallas_call(
        functools.partial(_latent_kernel, q_scale=q_scale),
        grid=(B, ns),
        in_specs=[
            pl.BlockSpec((None, tm, Q_LORA), col(G_BCQ * LANES // Q_LORA)),
            pl.BlockSpec((None, tm, KV_LORA), col(G_BCKV * LANES // KV_LORA)),
            pl.BlockSpec((None, tm, LANES), col(G_KR)),
            pl.BlockSpec((1, Q_LORA), row),
            pl.BlockSpec((1, KV_LORA), row),
            pl.BlockSpec((Q_LORA, B_HEADS * QK_DIM), row),
            pl.BlockSpec((KV_LORA, B_HEADS * QK_DIM), row),
            pl.BlockSpec((tm, LANES), lambda b, s: (s, 0)),
            pl.BlockSpec((tm, LANES), lambda b, s: (s, 0)),
        ],
        out_specs=[
            pl.BlockSpec((None, B_HEADS, QK_DIM, tm), lambda b, s: (b, 0, 0, s)),
            pl.BlockSpec((None, B_HEADS, tm, QK_DIM), lambda b, s: (b, 0, s, 0)),
            pl.BlockSpec((None, B_HEADS, D_V, tm), lambda b, s: (b, 0, 0, s)),
        ],
        out_shape=[
            jax.ShapeDtypeStruct((B, B_HEADS, QK_DIM, S), BF16),
            jax.ShapeDtypeStruct((B, B_HEADS, S, QK_DIM), BF16),
            jax.ShapeDtypeStruct((B, B_HEADS, D_V, S), BF16),
        ],
        compiler_params=_params(("parallel", "parallel")),
        name="latent_proj",
    )(proj, proj, proj, q_gain, kv_gain, wq_ext, wkv, cos_tab, sin_tab)


def _mla_kernel(q_ref, k_ref, vt_ref, g_ref, o_ref, s_ref, mrun_ref, mfin_ref, *, nk, tk):
    def fold(x, op):
        out = x[0:SUBLANES]
        for t in range(1, x.shape[0] // SUBLANES):
            out = op(out, x[t * SUBLANES:(t + 1) * SUBLANES])
        return out

    @pl.when(pl.program_id(0) == 0)
    def _():
        s_ref[...] = jnp.zeros(s_ref.shape, F32)
        mfin_ref[...] = jnp.zeros(mfin_ref.shape, F32)

    qt = q_ref[...]
    m_prev = mfin_ref[...]
    l_run = None
    acc = None
    for j in range(nk):
        ks = slice(j * tk, (j + 1) * tk)
        p = jnp.exp2(s_ref[ks, :] - m_prev)
        l_blk = fold(p, jnp.add)
        l_run = l_blk if l_run is None else l_run + l_blk
        pv = jnp.dot(vt_ref[:, ks], p.astype(BF16), preferred_element_type=F32)
        acc = pv if acc is None else acc + pv
        s = jnp.dot(k_ref[ks, :], qt, preferred_element_type=F32)
        s_ref[ks, :] = s
        m_blk = fold(s, jnp.maximum)
        mrun_ref[...] = m_blk if j == 0 else jnp.maximum(mrun_ref[...], m_blk)
    mfin_ref[...] = jnp.max(mrun_ref[...], axis=0, keepdims=True)
    o = (acc / jnp.sum(l_run, axis=0, keepdims=True)).T
    o_ref[...] = (o * _silu(g_ref[...].astype(F32))).astype(o_ref.dtype)


def _mla(q, k, v, proj, B, S, tq, tk):
    nq = S // tq
    n_tiles = B * B_HEADS * nq

    assert nq & (nq - 1) == 0 and B_HEADS & (B_HEADS - 1) == 0
    sh_q = nq.bit_length() - 1
    sh_h = B_HEADS.bit_length() - 1

    def tile(n):
        return (lax.shift_right_logical(n, sh_q + sh_h),
                lax.shift_right_logical(n, sh_q) & (B_HEADS - 1), n & (nq - 1))

    def cur(n):
        return tile(jnp.minimum(n, n_tiles - 1))

    def prev(n):
        return tile(jnp.maximum(n - 1, 0))

    def q_map(n):
        b, h, i = cur(n)
        return b, h, 0, i

    def k_map(n):
        b, h, _ = cur(n)
        return b, h, 0, 0

    def v_map(n):
        b, h, _ = prev(n)
        return b, h, 0, 0

    def g_map(n):
        b, h, i = prev(n)
        return b, i, G_BG + h

    def o_map(n):
        b, h, i = prev(n)
        return b, i, h

    return pl.pallas_call(
        functools.partial(_mla_kernel, nk=S // tk, tk=tk),
        grid=(n_tiles + 1,),
        in_specs=[
            pl.BlockSpec((None, None, QK_DIM, tq), q_map),
            pl.BlockSpec((None, None, S, QK_DIM), k_map),
            pl.BlockSpec((None, None, D_V, S), v_map),
            pl.BlockSpec((None, tq, LANES), g_map),
        ],
        out_specs=pl.BlockSpec((None, tq, D_V), o_map),
        out_shape=jax.ShapeDtypeStruct((B, S, B_WIDTH), BF16),
        scratch_shapes=[pltpu.VMEM((S, tq), F32), pltpu.VMEM((SUBLANES, tq), F32), pltpu.VMEM((1, tq), F32)],
        compiler_params=_params(("arbitrary",)),
        name="latent_attn",
    )(q, k, v, proj)


def _t5_bucket(rel):
    nb = N_BUCKETS // 2
    max_exact = nb // 2
    bucket = jnp.where(rel > 0, nb, 0)
    n = jnp.abs(rel)
    nf = jnp.maximum(n, 1).astype(F32)
    large = max_exact + (jnp.log(nf / max_exact) / math.log(MAX_DISTANCE / max_exact)
                         * (nb - max_exact)).astype(jnp.int32)
    large = jnp.minimum(large, nb - 1)
    return bucket + jnp.where(n < max_exact, n, large)


def _dil_bias_tiles(rel_bias, g, dil, tq, tk):
    qi = jnp.arange(tq)[:, None]
    kj = jnp.arange(tk)[None, :]
    table = rel_bias.astype(F32)[:, g * A_HEADS:(g + 1) * A_HEADS]
    rel = jnp.stack([kj - qi - off for off in (0, DIL_HALF, 2 * DIL_HALF)])
    bucket = _t5_bucket(rel * dil)
    b = jnp.zeros((3, A_HEADS, tq, tk), F32)
    for i in range(N_BUCKETS):
        b = jnp.where((bucket == i)[:, None], table[i][None, :, None, None], b)
    return jnp.where((jnp.abs(rel) <= DIL_HALF)[:, None], b * math.log2(math.e), NEG)


def _dil_kernel(q_ref, k_ref, v_ref, gate_ref, bias_ref, o_ref, t1_ref, t2_ref, k4_ref, v4_ref, acc_ref, lse_ref,
                *, S, tq, tk, unroll):
    g = pl.program_id(2)
    scale = A_HD ** -0.5 * math.log2(math.e)
    LB = S // DIL_BASE

    def attend(q, k, v, bias):
        s = lax.dot_general(q, k, _NT, preferred_element_type=F32) * scale + bias
        m = jnp.max(s, axis=1, keepdims=True)
        p = jnp.exp2(s - m)
        den = jnp.sum(p, axis=1, keepdims=True)
        o = jnp.dot(p.astype(BF16), v, preferred_element_type=F32) / den
        return o, jnp.broadcast_to(m + jnp.log2(den), (tq, LANES))

    def to_base_major(dst_ref, src_ref):
        for b in range(DIL_BASE):
            dst_ref[b * LB:(b + 1) * LB, :] = src_ref[pl.ds(b, LB, stride=DIL_BASE), :]

    @pl.when(g == 0)
    def _():
        t1_ref[...] = k_ref[...].astype(F32)
        to_base_major(k4_ref, t1_ref)
        t1_ref[...] = v_ref[...].astype(F32)
        to_base_major(v4_ref, t1_ref)
        nb = S // tq
        tb = tq // DIL_BASE

        def body(n, carry):
            q0 = pl.multiple_of(n * tq, tq)
            ks = pl.multiple_of(jnp.clip(q0 - DIL_HALF, 0, S - tk), DIL_HALF)
            placement = jnp.where(n == 0, 0, jnp.where(n == nb - 1, 2, 1))
            o, lse = attend(q_ref[pl.ds(q0, tq), :], k_ref[pl.ds(ks, tk), :], v_ref[pl.ds(ks, tk), :],
                            bias_ref[placement])
            t1_ref[pl.ds(q0, tq), :] = o
            t2_ref[pl.ds(q0, tq), :] = lse
            for b in range(DIL_BASE):
                rows = pl.ds(pl.multiple_of(b * LB + n * tb, SUBLANES), tb)
                acc_ref[rows, :] = t1_ref[pl.ds(q0 + b, tb, stride=DIL_BASE), :]
                lse_ref[rows, :] = t2_ref[pl.ds(q0 + b, tb, stride=DIL_BASE), :]
            return carry

        lax.fori_loop(0, nb, body, 0, unroll=min(unroll, nb))

    def strided_group(dil):
        e = dil // DIL_BASE
        L = S // dil
        nb = L // tq
        nc = L // DIL_HALF
        wch = tk // DIL_HALF
        qch = tq // DIL_HALF
        t1_ref[...] = q_ref[...].astype(F32)
        to_base_major(t2_ref, t1_ref)

        def rows_of(start, n_rows):
            if e > 1:
                return pl.ds(start, n_rows, stride=e)
            return pl.ds(pl.multiple_of(start, SUBLANES), n_rows)

        def stream(r, carry):
            base = lax.rem(r, DIL_BASE) * LB + lax.div(r, DIL_BASE)

            def chunks(ref):
                return [ref[rows_of(base + c * DIL_HALF * e, DIL_HALF), :].astype(BF16) for c in range(nc)]

            kc, vc = chunks(k4_ref), chunks(v4_ref)
            for n in range(nb):
                c0 = min(max(n * qch - 1, 0), nc - wch)
                placement = 0 if n == 0 else (2 if n == nb - 1 else 1)
                q_rows = rows_of(base + n * tq * e, tq)
                o, lse = attend(t2_ref[q_rows, :].astype(BF16), jnp.concatenate(kc[c0:c0 + wch], axis=0),
                                jnp.concatenate(vc[c0:c0 + wch], axis=0), bias_ref[placement])
                lse_old = lse_ref[q_rows, :]
                m2 = jnp.maximum(lse_old, lse)
                e_old = jnp.exp2(lse_old - m2)
                e_new = jnp.exp2(lse - m2)
                den = e_old + e_new
                acc_ref[q_rows, :] = (acc_ref[q_rows, :] * e_old + o * e_new) / den
                lse_ref[q_rows, :] = m2 + jnp.log2(den)
            return carry

        lax.fori_loop(0, dil, stream, 0, unroll=max(1, min(unroll // nb, dil)))

    for gi in range(1, N_DIL):
        pl.when(g == gi)(functools.partial(strided_group, DIL_PAIRS[gi][1]))

    @pl.when(g == N_DIL - 1)
    def _():
        for b in range(DIL_BASE):
            t1_ref[pl.ds(b, LB, stride=DIL_BASE), :] = acc_ref[b * LB:(b + 1) * LB, :]
        o_ref[...] = (t1_ref[...] * _silu(gate_ref[...].astype(F32))).astype(o_ref.dtype)


def _dilated(proj, bias, B, S, tq, unroll):
    tk = tq + 2 * DIL_HALF
    for _, dil in DIL_PAIRS:
        nb = S // dil // tq
        assert nb >= 2 and nb * tq * dil == S, (S, dil, tq)
        assert dil == 1 or dil % DIL_BASE == 0, dil
    q_group = lambda g: G_AQ[1] * g + (G_AQ[2] - 2 * G_AQ[1]) * (g // 2)
    col = lambda g0: (lambda b, h, g: (b, 0, g0 + h))
    return pl.pallas_call(
        functools.partial(_dil_kernel, S=S, tq=tq, tk=tk, unroll=unroll),
        grid=(B, A_HEADS, N_DIL),
        in_specs=[
            pl.BlockSpec((None, S, LANES), lambda b, h, g: (b, 0, q_group(g) + h)),
            pl.BlockSpec((None, S, LANES), col(G_AK)),
            pl.BlockSpec((None, S, LANES), col(G_AV)),
            pl.BlockSpec((None, S, LANES), col(G_AG)),
            pl.BlockSpec((None, 3, None, tq, tk), lambda b, h, g: (g, 0, h, 0, 0)),
        ],
        out_specs=pl.BlockSpec((None, S, LANES), lambda b, h, g: (b, 0, h)),
        out_shape=jax.ShapeDtypeStruct((B, S, A_WIDTH), BF16),
        scratch_shapes=[pltpu.VMEM((S, LANES), F32) for _ in range(6)],
        compiler_params=_params(("parallel", "arbitrary", "arbitrary")),
        name="dilated_attn",
    )(proj, proj, proj, proj, bias)


def _memkv_kernel(mem_ref, g_ref, w_ref, o_ref):
    x = mem_ref[...]
    ms = jnp.mean(x * x, axis=-1, keepdims=True)
    hn = (x * lax.rsqrt(ms + EPS) * g_ref[...]).astype(BF16)
    o_ref[...] = jnp.dot(hn, w_ref[...], preferred_element_type=F32).astype(o_ref.dtype)


def _memkv(mem, gain, w_bf16):
    B, M, _ = mem.shape
    return pl.pallas_call(
        _memkv_kernel,
        grid=(B,),
        in_specs=[
            pl.BlockSpec((None, M, D_MODEL), lambda b: (b, 0, 0)),
            pl.BlockSpec((1, D_MODEL), lambda b: (0, 0)),
            pl.BlockSpec((D_MODEL, 2 * C_WIDTH), lambda b: (0, 0)),
        ],
        out_specs=pl.BlockSpec((None, M, 2 * C_WIDTH), lambda b: (b, 0, 0)),
        out_shape=jax.ShapeDtypeStruct((B, M, 2 * C_WIDTH), BF16),
        compiler_params=_params(("parallel",)),
        name="mem_kv_proj",
    )(mem, gain, w_bf16)


def _out_kernel(ya_ref, yb_ref, cq_ref, cg_ref, mkv_ref, w_ref, pg_ref, x_ref, o_ref):
    y = jnp.dot(ya_ref[...], w_ref[0:A_WIDTH, :], preferred_element_type=F32)
    y = y + jnp.dot(yb_ref[...], w_ref[A_WIDTH:A_WIDTH + B_WIDTH, :], preferred_element_type=F32)
    scale = C_HD ** -0.5
    ycs = []
    for h in range(C_HEADS):
        cs = slice(h * C_HD, (h + 1) * C_HD)
        s = lax.dot_general(cq_ref[:, cs], mkv_ref[:, cs], _NT, preferred_element_type=F32) * scale
        m = jnp.max(s, axis=1, keepdims=True)
        p = jnp.exp(s - m)
        den = jnp.sum(p, axis=1, keepdims=True)
        mv = mkv_ref[:, C_WIDTH + h * C_HD:C_WIDTH + (h + 1) * C_HD]
        oc = jnp.dot(p.astype(BF16), mv, preferred_element_type=F32) / den
        ycs.append((oc * _silu(cg_ref[:, cs].astype(F32))).astype(BF16))
    y = y + jnp.dot(jnp.concatenate(ycs, axis=1), w_ref[A_WIDTH + B_WIDTH:, :], preferred_element_type=F32)
    ms = jnp.mean(y * y, axis=-1, keepdims=True)
    o_ref[...] = x_ref[...] + y * lax.rsqrt(ms + EPS) * pg_ref[...]


def _outproj(ya, yb, proj, mkv, w_out, post_gain, x, tm):
    B, S, _ = x.shape
    M = mkv.shape[1]
    tok = lambda width, blk: pl.BlockSpec((None, tm, width), lambda b, s: (b, s, blk))
    const = lambda shape: pl.BlockSpec(shape, lambda b, s: (0,) * len(shape))
    return pl.pallas_call(
        _out_kernel,
        grid=(B, S // tm),
        in_specs=[
            tok(A_WIDTH, 0),
            tok(B_WIDTH, 0),
            tok(C_WIDTH, G_CQ * LANES // C_WIDTH),
            tok(C_WIDTH, G_CG * LANES // C_WIDTH),
            pl.BlockSpec((None, M, 2 * C_WIDTH), lambda b, s: (b, 0, 0)),
            const((D_MIX, D_MODEL)),
            const((1, D_MODEL)),
            tok(D_MODEL, 0),
        ],
        out_specs=tok(D_MODEL, 0),
        out_shape=jax.ShapeDtypeStruct((B, S, D_MODEL), x.dtype),
        compiler_params=_params(("parallel", "parallel")),
        name="outproj",
    )(ya, yb, proj, proj, mkv, w_out, post_gain, x)


def _swap_halves(w):
    half = D_ROPE // 2
    return jnp.concatenate([w[..., half:], w[..., :half]], axis=-1)


def _prep_weights(w_in, w_uq, w_ukv, w_mem_kv, w_out):
    o_kr = N_DIL * A_WIDTH + 3 * A_WIDTH + Q_LORA + KV_LORA
    o_bg = o_kr + D_ROPE
    w_in = w_in.astype(BF16)
    kr = w_in[:, o_kr:o_bg]
    w_in_r = jnp.concatenate(
        [w_in[:, :o_kr], w_in[:, o_bg:], kr, _swap_halves(kr),
         jnp.zeros((D_MODEL, D_INP - (G_KR + 1) * LANES), BF16)], axis=1)
    wq = w_uq.reshape(Q_LORA, B_HEADS, D_NOPE + D_ROPE)
    rope = wq[:, :, D_NOPE:]
    wq_ext = jnp.concatenate([wq[:, :, :D_NOPE], rope, _swap_halves(rope)], axis=-1)
    wq_ext = wq_ext.reshape(Q_LORA, B_HEADS * QK_DIM).astype(BF16)
    return w_in_r, wq_ext, w_ukv.astype(BF16), w_mem_kv.astype(BF16), w_out.astype(BF16)


def _rope_tables(S):
    inv = ROPE_BASE ** (-jnp.arange(0, D_ROPE, 2, dtype=F32) / D_ROPE)
    ang = jnp.arange(S, dtype=F32)[:, None] * inv[None, :]
    cos, sin = jnp.cos(ang), jnp.sin(ang)
    zeros = jnp.zeros((S, LANES - D_ROPE), F32)
    return (jnp.concatenate([cos, cos, zeros], axis=1),
            jnp.concatenate([-sin, sin, zeros], axis=1))


def _dil_biases(rel_bias, tq):
    return jnp.stack([_dil_bias_tiles(rel_bias, g, dil, tq, tq + 2 * DIL_HALF)
                      for g, (_, dil) in enumerate(DIL_PAIRS)])


def _layer(x, mem, pre_gain, q_gain, kv_gain, mem_gain, post_gain, biases, weights):
    w_in_r, wq_ext, wkv, wmem, wout = weights
    B, S, _ = x.shape
    t = _tiles(S)
    proj = _inproj(x.reshape(B * S, D_MODEL), pre_gain, w_in_r, t["tm_in"], t["tn_in"], t["rows_in"])
    proj = proj.reshape(B, S, D_INP)
    cos_tab, sin_tab = _rope_tables(S)
    q, k, v = _latent(proj, q_gain, kv_gain, wq_ext, wkv, cos_tab, sin_tab, B, S, t["tm_lat"])
    yb = _mla(q, k, v, proj, B, S, t["tq_mla"], t["tk_mla"])
    ya = _dilated(proj, biases, B, S, t["tq_dil"], t["unroll_dil"])
    mkv = _memkv(mem, mem_gain, wmem)
    return _outproj(ya, yb, proj, mkv, wout, post_gain, x, t["tm_out"])


def kernel(x_prompt, x_sample, mem_prompt, mem_sample, pre_gain, w_in, q_gain, w_uq, kv_gain, w_ukv,
           mem_gain, w_mem_kv, w_out, post_gain, rel_bias):
    depth = pre_gain.shape[0]
    weights = [_prep_weights(w_in[l], w_uq[l], w_ukv[l], w_mem_kv[l], w_out[l]) for l in range(depth)]
    biases = _dil_biases(rel_bias, _tiles(x_prompt.shape[1])["tq_dil"])

    def trunk(x, mem):
        for l in range(depth):
            x = _layer(x, mem, pre_gain[l][None], q_gain[l][None], kv_gain[l][None], mem_gain[l][None],
                       post_gain[l][None], biases, weights[l])
        return x

    return trunk(x_prompt, mem_prompt), trunk(x_sample, mem_sample)
```

```python
import functools
import math

import jax
import jax.numpy as jnp
from jax import lax
from jax.experimental import pallas as pl
from jax.experimental.pallas import tpu as pltpu

D_MODEL = 2048
DIL_PAIRS = ((128, 1), (512, 4), (2048, 16))
N_DIL = 3
DIL_HALF = 64
DIL_BASE = 4
A_HEADS = 4
A_HD = 128
A_WIDTH = A_HEADS * A_HD
B_HEADS = 8
Q_LORA = 512
KV_LORA = 512
D_NOPE = 128
D_ROPE = 64
D_V = 128
B_WIDTH = B_HEADS * D_V
ROPE_BASE = 10000.0
C_HEADS = 4
C_HD = 128
C_WIDTH = C_HEADS * C_HD
D_MIX = A_WIDTH + B_WIDTH + C_WIDTH
N_BUCKETS = 32
MAX_DISTANCE = 1024
EPS = 1e-6
NEG = -1e30

LANES = 128
SUBLANES = 8
VMEM_LIMIT_BYTES = 56 * 1024 * 1024

G_AQ = (0, 4, 8)
G_AK = 12
G_AV = 16
G_AG = 20
G_BCQ = 24
G_BCKV = 28
G_BG = 32
G_CQ = 40
G_CG = 44
G_KR = 48
N_GROUPS = 50
MXU_COLS = 2 * LANES
D_INP = N_GROUPS * LANES
QK_DIM = 2 * LANES

F32 = jnp.float32
BF16 = jnp.bfloat16
_NT = (((1,), (1,)), ((), ()))


def _tiles(S):
    return dict(
        tm_in=min(512, S),
        rows_in=min(256, S),
        tn_in=5 * MXU_COLS,
        tm_lat=min(1024, S),
        tq_mla=min(512, S),
        tk_mla=min(256, S),
        tq_dil=128,
        unroll_dil=64,
        tm_out=min(512, S),
    )


def _params(sem):
    return pltpu.CompilerParams(dimension_semantics=sem, vmem_limit_bytes=VMEM_LIMIT_BYTES)


def _silu(g):
    return g * (1.0 / (1.0 + jnp.exp(-g)))


def _inproj_kernel(x_ref, g_ref, w_hbm, o_ref, w_ref, h_ref, sem, *, rows, tn):
    def w_copy():
        return pltpu.make_async_copy(w_hbm, w_ref, sem)

    @pl.when(pl.program_id(0) == 0)
    def _():
        w_copy().start()
        w_copy().wait()

    tm = x_ref.shape[0]
    for r0 in range(0, tm, rows):
        x = x_ref[r0:r0 + rows, :]
        ms = jnp.mean(x * x, axis=-1, keepdims=True)
        h_ref[r0:r0 + rows, :] = (x * lax.rsqrt(ms + EPS) * g_ref[...]).astype(BF16)
    for c0 in range(0, D_INP, tn):
        o_ref[:, c0:c0 + tn] = jnp.dot(h_ref[...], w_ref[:, c0:c0 + tn],
                                       preferred_element_type=F32).astype(o_ref.dtype)


def _inproj(x2d, gain, w_bf16, tm, tn, rows):
    T = x2d.shape[0]
    return pl.pallas_call(
        functools.partial(_inproj_kernel, rows=rows, tn=tn),
        grid=(T // tm,),
        in_specs=[
            pl.BlockSpec((tm, D_MODEL), lambda i: (i, 0)),
            pl.BlockSpec((1, D_MODEL), lambda i: (0, 0)),
            pl.BlockSpec(memory_space=pl.ANY),
        ],
        out_specs=pl.BlockSpec((tm, D_INP), lambda i: (i, 0)),
        out_shape=jax.ShapeDtypeStruct((T, D_INP), BF16),
        scratch_shapes=[pltpu.VMEM((D_MODEL, D_INP), BF16), pltpu.VMEM((tm, D_MODEL), BF16),
                        pltpu.SemaphoreType.DMA(())],
        compiler_params=_params(("arbitrary",)),
        name="inproj",
    )(x2d, gain, w_bf16)


def _latent_kernel(cq_ref, ckv_ref, kr_ref, qg_ref, kvg_ref, wq_ref, wkv_ref, cos_ref, sin_ref,
                   q_ref, k_ref, v_ref, *, q_scale):
    def rms(c_ref, gain_ref):
        c = c_ref[...].astype(F32)
        ms = jnp.mean(c * c, axis=-1, keepdims=True)
        return (c * lax.rsqrt(ms + EPS) * gain_ref[...]).astype(BF16)

    cos = cos_ref[...]
    sin = sin_ref[...]

    def rotary(g):
        return g * cos + pltpu.roll(g, D_ROPE, 1) * sin

    qf = jnp.dot(rms(cq_ref, qg_ref), wq_ref[...], preferred_element_type=F32)
    kvf = jnp.dot(rms(ckv_ref, kvg_ref), wkv_ref[...], preferred_element_type=F32)
    k_rot = rotary(kr_ref[...].astype(F32)).astype(BF16)
    for h in range(B_HEADS):
        c0 = h * QK_DIM
        q_ref[h, 0:LANES, :] = (qf[:, c0:c0 + LANES] * q_scale).T.astype(BF16)
        q_ref[h, LANES:QK_DIM, :] = (rotary(qf[:, c0 + LANES:c0 + QK_DIM]) * q_scale).T.astype(BF16)
        k_ref[h, :, 0:LANES] = kvf[:, c0:c0 + LANES].astype(BF16)
        k_ref[h, :, LANES:QK_DIM] = k_rot
        v_ref[h, :, :] = kvf[:, c0 + LANES:c0 + QK_DIM].T.astype(BF16)


def _latent(proj, q_gain, kv_gain, wq_ext, wkv, cos_tab, sin_tab, B, S, tm):
    q_scale = (D_NOPE + D_ROPE) ** -0.5 * math.log2(math.e)
    ns = S // tm
    col = lambda g: (lambda b, s: (b, s, g))
    row = lambda b, s: (0, 0)
    return pl.pallas_call(
        functools.partial(_latent_kernel, q_scale=q_scale),
        grid=(B, ns),
        in_specs=[
            pl.BlockSpec((None, tm, Q_LORA), col(G_BCQ * LANES // Q_LORA)),
            pl.BlockSpec((None, tm, KV_LORA), col(G_BCKV * LANES // KV_LORA)),
            pl.BlockSpec((None, tm, LANES), col(G_KR)),
            pl.BlockSpec((1, Q_LORA), row),
            pl.BlockSpec((1, KV_LORA), row),
            pl.BlockSpec((Q_LORA, B_HEADS * QK_DIM), row),
            pl.BlockSpec((KV_LORA, B_HEADS * QK_DIM), row),
            pl.BlockSpec((tm, LANES), lambda b, s: (s, 0)),
            pl.BlockSpec((tm, LANES), lambda b, s: (s, 0)),
        ],
        out_specs=[
            pl.BlockSpec((None, B_HEADS, QK_DIM, tm), lambda b, s: (b, 0, 0, s)),
            pl.BlockSpec((None, B_HEADS, tm, QK_DIM), lambda b, s: (b, 0, s, 0)),
            pl.BlockSpec((None, B_HEADS, D_V, tm), lambda b, s: (b, 0, 0, s)),
        ],
        out_shape=[
            jax.ShapeDtypeStruct((B, B_HEADS, QK_DIM, S), BF16),
            jax.ShapeDtypeStruct((B, B_HEADS, S, QK_DIM), BF16),
            jax.ShapeDtypeStruct((B, B_HEADS, D_V, S), BF16),
        ],
        compiler_params=_params(("parallel", "parallel")),
        name="latent_proj",
    )(proj, proj, proj, q_gain, kv_gain, wq_ext, wkv, cos_tab, sin_tab)


def _mla_kernel(q_ref, k_ref, vt_ref, g_ref, o_ref, s_ref, mrun_ref, mfin_ref, *, nk, tk):
    def fold(x, op):
        out = x[0:SUBLANES]
        for t in range(1, x.shape[0] // SUBLANES):
            out = op(out, x[t * SUBLANES:(t + 1) * SUBLANES])
        return out

    @pl.when(pl.program_id(0) == 0)
    def _():
        s_ref[...] = jnp.zeros(s_ref.shape, F32)
        mfin_ref[...] = jnp.zeros(mfin_ref.shape, F32)

    qt = q_ref[...]
    m_prev = mfin_ref[...]
    l_run = None
    acc = None
    for j in range(nk):
        ks = slice(j * tk, (j + 1) * tk)
        p = jnp.exp2(s_ref[ks, :] - m_prev)
        l_blk = fold(p, jnp.add)
        l_run = l_blk if l_run is None else l_run + l_blk
        pv = jnp.dot(vt_ref[:, ks], p.astype(BF16), preferred_element_type=F32)
        acc = pv if acc is None else acc + pv
        s = jnp.dot(k_ref[ks, :], qt, preferred_element_type=F32)
        s_ref[ks, :] = s
        m_blk = fold(s, jnp.maximum)
        mrun_ref[...] = m_blk if j == 0 else jnp.maximum(mrun_ref[...], m_blk)
    mfin_ref[...] = jnp.max(mrun_ref[...], axis=0, keepdims=True)
    o = (acc / jnp.sum(l_run, axis=0, keepdims=True)).T
    o_ref[...] = (o * _silu(g_ref[...].astype(F32))).astype(o_ref.dtype)


def _mla(q, k, v, proj, B, S, tq, tk):
    nq = S // tq
    n_tiles = B * B_HEADS * nq

    assert nq & (nq - 1) == 0 and B_HEADS & (B_HEADS - 1) == 0
    sh_q = nq.bit_length() - 1
    sh_h = B_HEADS.bit_length() - 1

    def tile(n):
        return (lax.shift_right_logical(n, sh_q + sh_h),
                lax.shift_right_logical(n, sh_q) & (B_HEADS - 1), n & (nq - 1))

    def cur(n):
        return tile(jnp.minimum(n, n_tiles - 1))

    def prev(n):
        return tile(jnp.maximum(n - 1, 0))

    def q_map(n):
        b, h, i = cur(n)
        return b, h, 0, i

    def k_map(n):
        b, h, _ = cur(n)
        return b, h, 0, 0

    def v_map(n):
        b, h, _ = prev(n)
        return b, h, 0, 0

    def g_map(n):
        b, h, i = prev(n)
        return b, i, G_BG + h

    def o_map(n):
        b, h, i = prev(n)
        return b, i, h

    return pl.pallas_call(
        functools.partial(_mla_kernel, nk=S // tk, tk=tk),
        grid=(n_tiles + 1,),
        in_specs=[
            pl.BlockSpec((None, None, QK_DIM, tq), q_map),
            pl.BlockSpec((None, None, S, QK_DIM), k_map),
            pl.BlockSpec((None, None, D_V, S), v_map),
            pl.BlockSpec((None, tq, LANES), g_map),
        ],
        out_specs=pl.BlockSpec((None, tq, D_V), o_map),
        out_shape=jax.ShapeDtypeStruct((B, S, B_WIDTH), BF16),
        scratch_shapes=[pltpu.VMEM((S, tq), F32), pltpu.VMEM((SUBLANES, tq), F32), pltpu.VMEM((1, tq), F32)],
        compiler_params=_params(("arbitrary",)),
        name="latent_attn",
    )(q, k, v, proj)


def _t5_bucket(rel):
    nb = N_BUCKETS // 2
    max_exact = nb // 2
    bucket = jnp.where(rel > 0, nb, 0)
    n = jnp.abs(rel)
    nf = jnp.maximum(n, 1).astype(F32)
    large = max_exact + (jnp.log(nf / max_exact) / math.log(MAX_DISTANCE / max_exact)
                         * (nb - max_exact)).astype(jnp.int32)
    large = jnp.minimum(large, nb - 1)
    return bucket + jnp.where(n < max_exact, n, large)


def _dil_bias_tiles(rel_bias, g, dil, tq, tk):
    qi = jnp.arange(tq)[:, None]
    kj = jnp.arange(tk)[None, :]
    table = rel_bias.astype(F32)[:, g * A_HEADS:(g + 1) * A_HEADS]
    rel = jnp.stack([kj - qi - off for off in (0, DIL_HALF, 2 * DIL_HALF)])
    bucket = _t5_bucket(rel * dil)
    b = jnp.zeros((3, A_HEADS, tq, tk), F32)
    for i in range(N_BUCKETS):
        b = jnp.where((bucket == i)[:, None], table[i][None, :, None, None], b)
    return jnp.where((jnp.abs(rel) <= DIL_HALF)[:, None], b * math.log2(math.e), NEG)


def _dil_kernel(q_ref, k_ref, v_ref, gate_ref, bias_ref, o_ref, t1_ref, t2_ref, k4_ref, v4_ref, acc_ref, lse_ref,
                *, S, tq, tk, unroll):
    g = pl.program_id(2)
    scale = A_HD ** -0.5 * math.log2(math.e)
    LB = S // DIL_BASE

    def attend(q, k, v, bias):
        s = lax.dot_general(q, k, _NT, preferred_element_type=F32) * scale + bias
        m = jnp.max(s, axis=1, keepdims=True)
        p = jnp.exp2(s - m)
        den = jnp.sum(p, axis=1, keepdims=True)
        o = jnp.dot(p.astype(BF16), v, preferred_element_type=F32) / den
        return o, jnp.broadcast_to(m + jnp.log2(den), (tq, LANES))

    def to_base_major(dst_ref, src_ref):
        for b in range(DIL_BASE):
            dst_ref[b * LB:(b + 1) * LB, :] = src_ref[pl.ds(b, LB, stride=DIL_BASE), :]

    @pl.when(g == 0)
    def _():
        t1_ref[...] = k_ref[...].astype(F32)
        to_base_major(k4_ref, t1_ref)
        t1_ref[...] = v_ref[...].astype(F32)
        to_base_major(v4_ref, t1_ref)
        nb = S // tq
        tb = tq // DIL_BASE

        def body(n, carry):
            q0 = pl.multiple_of(n * tq, tq)
            ks = pl.multiple_of(jnp.clip(q0 - DIL_HALF, 0, S - tk), DIL_HALF)
            placement = jnp.where(n == 0, 0, jnp.where(n == nb - 1, 2, 1))
            o, lse = attend(q_ref[pl.ds(q0, tq), :], k_ref[pl.ds(ks, tk), :], v_ref[pl.ds(ks, tk), :],
                            bias_ref[placement])
            t1_ref[pl.ds(q0, tq), :] = o
            t2_ref[pl.ds(q0, tq), :] = lse
            for b in range(DIL_BASE):
                rows = pl.ds(pl.multiple_of(b * LB + n * tb, SUBLANES), tb)
                acc_ref[rows, :] = t1_ref[pl.ds(q0 + b, tb, stride=DIL_BASE), :]
                lse_ref[rows, :] = t2_ref[pl.ds(q0 + b, tb, stride=DIL_BASE), :]
            return carry

        lax.fori_loop(0, nb, body, 0, unroll=min(unroll, nb))

    def strided_group(dil):
        e = dil // DIL_BASE
        L = S // dil
        nb = L // tq
        nc = L // DIL_HALF
        wch = tk // DIL_HALF
        qch = tq // DIL_HALF
        t1_ref[...] = q_ref[...].astype(F32)
        to_base_major(t2_ref, t1_ref)

        def rows_of(start, n_rows):
            if e > 1:
                return pl.ds(start, n_rows, stride=e)
            return pl.ds(pl.multiple_of(start, SUBLANES), n_rows)

        def stream(r, carry):
            base = lax.rem(r, DIL_BASE) * LB + lax.div(r, DIL_BASE)

            def chunks(ref):
                return [ref[rows_of(base + c * DIL_HALF * e, DIL_HALF), :].astype(BF16) for c in range(nc)]

            kc, vc = chunks(k4_ref), chunks(v4_ref)
            for n in range(nb):
                c0 = min(max(n * qch - 1, 0), nc - wch)
                placement = 0 if n == 0 else (2 if n == nb - 1 else 1)
                q_rows = rows_of(base + n * tq * e, tq)
                o, lse = attend(t2_ref[q_rows, :].astype(BF16), jnp.concatenate(kc[c0:c0 + wch], axis=0),
                                jnp.concatenate(vc[c0:c0 + wch], axis=0), bias_ref[placement])
                lse_old = lse_ref[q_rows, :]
                m2 = jnp.maximum(lse_old, lse)
                e_old = jnp.exp2(lse_old - m2)
                e_new = jnp.exp2(lse - m2)
                den = e_old + e_new
                acc_ref[q_rows, :] = (acc_ref[q_rows, :] * e_old + o * e_new) / den
                lse_ref[q_rows, :] = m2 + jnp.log2(den)
            return carry

        lax.fori_loop(0, dil, stream, 0, unroll=max(1, min(unroll // nb, dil)))

    for gi in range(1, N_DIL):
        pl.when(g == gi)(functools.partial(strided_group, DIL_PAIRS[gi][1]))

    @pl.when(g == N_DIL - 1)
    def _():
        for b in range(DIL_BASE):
            t1_ref[pl.ds(b, LB, stride=DIL_BASE), :] = acc_ref[b * LB:(b + 1) * LB, :]
        o_ref[...] = (t1_ref[...] * _silu(gate_ref[...].astype(F32))).astype(o_ref.dtype)


def _dilated(proj, bias, B, S, tq, unroll):
    tk = tq + 2 * DIL_HALF
    for _, dil in DIL_PAIRS:
        nb = S // dil // tq
        assert nb >= 2 and nb * tq * dil == S, (S, dil, tq)
        assert dil == 1 or dil % DIL_BASE == 0, dil
    q_group = lambda g: G_AQ[1] * g + (G_AQ[2] - 2 * G_AQ[1]) * (g // 2)
    col = lambda g0: (lambda b, h, g: (b, 0, g0 + h))
    return pl.pallas_call(
        functools.partial(_dil_kernel, S=S, tq=tq, tk=tk, unroll=unroll),
        grid=(B, A_HEADS, N_DIL),
        in_specs=[
            pl.BlockSpec((None, S, LANES), lambda b, h, g: (b, 0, q_group(g) + h)),
            pl.BlockSpec((None, S, LANES), col(G_AK)),
            pl.BlockSpec((None, S, LANES), col(G_AV)),
            pl.BlockSpec((None, S, LANES), col(G_AG)),
            pl.BlockSpec((None, 3, None, tq, tk), lambda b, h, g: (g, 0, h, 0, 0)),
        ],
        out_specs=pl.BlockSpec((None, S, LANES), lambda b, h, g: (b, 0, h)),
        out_shape=jax.ShapeDtypeStruct((B, S, A_WIDTH), BF16),
        scratch_shapes=[pltpu.VMEM((S, LANES), F32) for _ in range(6)],
        compiler_params=_params(("parallel", "arbitrary", "arbitrary")),
        name="dilated_attn",
    )(proj, proj, proj, proj, bias)


def _memkv_kernel(mem_ref, g_ref, w_ref, o_ref):
    x = mem_ref[...]
    ms = jnp.mean(x * x, axis=-1, keepdims=True)
    hn = (x * lax.rsqrt(ms + EPS) * g_ref[...]).astype(BF16)
    o_ref[...] = jnp.dot(hn, w_ref[...], preferred_element_type=F32).astype(o_ref.dtype)


def _memkv(mem, gain, w_bf16):
    B, M, _ = mem.shape
    return pl.pallas_call(
        _memkv_kernel,
        grid=(B,),
        in_specs=[
            pl.BlockSpec((None, M, D_MODEL), lambda b: (b, 0, 0)),
            pl.BlockSpec((1, D_MODEL), lambda b: (0, 0)),
            pl.BlockSpec((D_MODEL, 2 * C_WIDTH), lambda b: (0, 0)),
        ],
        out_specs=pl.BlockSpec((None, M, 2 * C_WIDTH), lambda b: (b, 0, 0)),
        out_shape=jax.ShapeDtypeStruct((B, M, 2 * C_WIDTH), BF16),
        compiler_params=_params(("parallel",)),
        name="mem_kv_proj",
    )(mem, gain, w_bf16)


def _out_kernel(ya_ref, yb_ref, cq_ref, cg_ref, mkv_ref, w_ref, pg_ref, x_ref, o_ref):
    y = jnp.dot(ya_ref[...], w_ref[0:A_WIDTH, :], preferred_element_type=F32)
    y = y + jnp.dot(yb_ref[...], w_ref[A_WIDTH:A_WIDTH + B_WIDTH, :], preferred_element_type=F32)
    scale = C_HD ** -0.5
    ycs = []
    for h in range(C_HEADS):
        cs = slice(h * C_HD, (h + 1) * C_HD)
        s = lax.dot_general(cq_ref[:, cs], mkv_ref[:, cs], _NT, preferred_element_type=F32) * scale
        m = jnp.max(s, axis=1, keepdims=True)
        p = jnp.exp(s - m)
        den = jnp.sum(p, axis=1, keepdims=True)
        mv = mkv_ref[:, C_WIDTH + h * C_HD:C_WIDTH + (h + 1) * C_HD]
        oc = jnp.dot(p.astype(BF16), mv, preferred_element_type=F32) / den
        ycs.append((oc * _silu(cg_ref[:, cs].astype(F32))).astype(BF16))
    y = y + jnp.dot(jnp.concatenate(ycs, axis=1), w_ref[A_WIDTH + B_WIDTH:, :], preferred_element_type=F32)
    ms = jnp.mean(y * y, axis=-1, keepdims=True)
    o_ref[...] = x_ref[...] + y * lax.rsqrt(ms + EPS) * pg_ref[...]


def _outproj(ya, yb, proj, mkv, w_out, post_gain, x, tm):
    B, S, _ = x.shape
    M = mkv.shape[1]
    tok = lambda width, blk: pl.BlockSpec((None, tm, width), lambda b, s: (b, s, blk))
    const = lambda shape: pl.BlockSpec(shape, lambda b, s: (0,) * len(shape))
    return pl.pallas_call(
        _out_kernel,
        grid=(B, S // tm),
        in_specs=[
            tok(A_WIDTH, 0),
            tok(B_WIDTH, 0),
            tok(C_WIDTH, G_CQ * LANES // C_WIDTH),
            tok(C_WIDTH, G_CG * LANES // C_WIDTH),
            pl.BlockSpec((None, M, 2 * C_WIDTH), lambda b, s: (b, 0, 0)),
            const((D_MIX, D_MODEL)),
            const((1, D_MODEL)),
            tok(D_MODEL, 0),
        ],
        out_specs=tok(D_MODEL, 0),
        out_shape=jax.ShapeDtypeStruct((B, S, D_MODEL), x.dtype),
        compiler_params=_params(("parallel", "parallel")),
        name="outproj",
    )(ya, yb, proj, proj, mkv, w_out, post_gain, x)


def _swap_halves(w):
    half = D_ROPE // 2
    return jnp.concatenate([w[..., half:], w[..., :half]], axis=-1)


def _prep_weights(w_in, w_uq, w_ukv, w_mem_kv, w_out):
    o_kr = N_DIL * A_WIDTH + 3 * A_WIDTH + Q_LORA + KV_LORA
    o_bg = o_kr + D_ROPE
    w_in = w_in.astype(BF16)
    kr = w_in[:, o_kr:o_bg]
    w_in_r = jnp.concatenate(
        [w_in[:, :o_kr], w_in[:, o_bg:], kr, _swap_halves(kr),
         jnp.zeros((D_MODEL, D_INP - (G_KR + 1) * LANES), BF16)], axis=1)
    wq = w_uq.reshape(Q_LORA, B_HEADS, D_NOPE + D_ROPE)
    rope = wq[:, :, D_NOPE:]
    wq_ext = jnp.concatenate([wq[:, :, :D_NOPE], rope, _swap_halves(rope)], axis=-1)
    wq_ext = wq_ext.reshape(Q_LORA, B_HEADS * QK_DIM).astype(BF16)
    return w_in_r, wq_ext, w_ukv.astype(BF16), w_mem_kv.astype(BF16), w_out.astype(BF16)


def _rope_tables(S):
    inv = ROPE_BASE ** (-jnp.arange(0, D_ROPE, 2, dtype=F32) / D_ROPE)
    ang = jnp.arange(S, dtype=F32)[:, None] * inv[None, :]
    cos, sin = jnp.cos(ang), jnp.sin(ang)
    zeros = jnp.zeros((S, LANES - D_ROPE), F32)
    return (jnp.concatenate([cos, cos, zeros], axis=1),
            jnp.concatenate([-sin, sin, zeros], axis=1))


def _dil_biases(rel_bias, tq):
    return jnp.stack([_dil_bias_tiles(rel_bias, g, dil, tq, tq + 2 * DIL_HALF)
                      for g, (_, dil) in enumerate(DIL_PAIRS)])


def _layer(x, mem, pre_gain, q_gain, kv_gain, mem_gain, post_gain, biases, weights):
    w_in_r, wq_ext, wkv, wmem, wout = weights
    B, S, _ = x.shape
    t = _tiles(S)
    proj = _inproj(x.reshape(B * S, D_MODEL), pre_gain, w_in_r, t["tm_in"], t["tn_in"], t["rows_in"])
    proj = proj.reshape(B, S, D_INP)
    cos_tab, sin_tab = _rope_tables(S)
    q, k, v = _latent(proj, q_gain, kv_gain, wq_ext, wkv, cos_tab, sin_tab, B, S, t["tm_lat"])
    yb = _mla(q, k, v, proj, B, S, t["tq_mla"], t["tk_mla"])
    ya = _dilated(proj, biases, B, S, t["tq_dil"], t["unroll_dil"])
    mkv = _memkv(mem, mem_gain, wmem)
    return _outproj(ya, yb, proj, mkv, wout, post_gain, x, t["tm_out"])


def kernel(x_prompt, x_sample, mem_prompt, mem_sample, pre_gain, w_in, q_gain, w_uq, kv_gain, w_ukv,
           mem_gain, w_mem_kv, w_out, post_gain, rel_bias):
    depth = pre_gain.shape[0]
    weights = [_prep_weights(w_in[l], w_uq[l], w_ukv[l], w_mem_kv[l], w_out[l]) for l in range(depth)]
    biases = _dil_biases(rel_bias, _tiles(x_prompt.shape[1])["tq_dil"])

    def trunk(x, mem):
        for l in range(depth):
            x = _layer(x, mem, pre_gain[l][None], q_gain[l][None], kv_gain[l][None], mem_gain[l][None],
                       post_gain[l][None], biases, weights[l])
        return x

    return trunk(x_prompt, mem_prompt), trunk(x_sample, mem_sample)
```
